```python
import math
import jax, jax.numpy as jnp
from jax import lax
import numpy as np

D_MODEL = 2048
BATCH = 4
SEQ = 4096
DEPTH = 1

CHUNK = 64
NORM_EPS = 1e-6
NEG_INF = -1e30
MAX_STREAM_OFFSET = 8192

GDN_HEADS = 16
GDN_DK = 128
GDN_DV = 128
GDN_KEY = GDN_HEADS * GDN_DK
GDN_VAL = GDN_HEADS * GDN_DV
GDN_CONV_DIM = 2 * GDN_KEY + GDN_VAL
CONV_K = 4

DIFF_HEADS = 8
DIFF_D = 128
DIFF_QK = 2 * DIFF_HEADS * DIFF_D
DIFF_V = DIFF_HEADS * 2 * DIFF_D
ROT_DIM = DIFF_D // 4
ROPE_THETA = 500000.0
Q_BLOCK = 128

N_BRANCH = 2
OFF_QKV_A = 0
OFF_Z_A = OFF_QKV_A + GDN_CONV_DIM
OFF_BETA_A = OFF_Z_A + GDN_VAL
OFF_ALPHA_A = OFF_BETA_A + GDN_HEADS
OFF_Q_B = OFF_ALPHA_A + GDN_HEADS
OFF_K_B = OFF_Q_B + DIFF_QK
OFF_V_B = OFF_K_B + DIFF_QK
OFF_GATE = OFF_V_B + DIFF_V
IN_COLS = OFF_GATE + N_BRANCH * D_MODEL

N_EXPERTS = 32
TOP_K = 4
D_FF = D_MODEL
SWIGLU_ALPHA = 1.702
SWIGLU_LIMIT = 7.0
EXPERT_BLOCK = 512

kernel_name = 'streaming_hybrid_gdn_diffattn_moe'


def rms_norm(x, g):
    xf = x.astype(jnp.float32)
    y = xf * lax.rsqrt(jnp.mean(xf * xf, axis=-1, keepdims=True) + NORM_EPS)
    return y.astype(x.dtype) * g


def l2_norm(x):
    return x * lax.rsqrt(jnp.sum(x * x, axis=-1, keepdims=True) + NORM_EPS)


def lambda_init(layer_idx):
    return 0.8 - 0.6 * math.exp(-0.3 * layer_idx)


def causal_conv(u, w):
    S = u.shape[1]
    K = w.shape[0]
    up = jnp.pad(u, ((0, 0), (K - 1, 0), (0, 0)))
    return sum(up[:, j:j + S] * w[j] for j in range(K))


def gated_delta_rule(q, k, v, g, beta):
    B, S, H, DK = q.shape
    DV = v.shape[-1]
    NC = S // CHUNK
    q = q * DK ** -0.5

    def chunks(t):
        return t.reshape(B, NC, CHUNK, H, t.shape[-1]).transpose(0, 1, 3, 2, 4)

    qc, kc, vc = chunks(q), chunks(k), chunks(v)
    gc = g.reshape(B, NC, CHUNK, H).transpose(0, 1, 3, 2)
    bc = beta.reshape(B, NC, CHUNK, H).transpose(0, 1, 3, 2)
    gcum = jnp.cumsum(gc, axis=-1)
    tril = jnp.tril(jnp.ones((CHUNK, CHUNK), bool))
    strict = jnp.tril(jnp.ones((CHUNK, CHUNK), bool), -1)
    diff = gcum[..., :, None] - gcum[..., None, :]
    decay = jnp.where(tril, jnp.exp(jnp.where(tril, diff, 0.0)), 0.0)
    kb = kc * bc[..., None]
    eye = jnp.eye(CHUNK, dtype=jnp.float32)
    a_mat = jnp.where(strict, jnp.einsum('bnhik,bnhjk->bnhij', kb, kc) * decay, 0.0) + eye
    t_mat = lax.linalg.triangular_solve(a_mat, jnp.broadcast_to(eye, a_mat.shape),
                                        left_side=True, lower=True, unit_diagonal=True)
    u = jnp.einsum('bnhij,bnhjv->bnhiv', t_mat, vc * bc[..., None])
    w = jnp.einsum('bnhij,bnhjk->bnhik', t_mat, kb * jnp.exp(gcum)[..., None])
    qk = jnp.einsum('bnhik,bnhjk->bnhij', qc, kc) * decay
    qg = qc * jnp.exp(gcum)[..., None]
    kdec = kc * jnp.exp(gcum[..., -1:] - gcum)[..., None]
    glast = jnp.exp(gcum[..., -1])

    def step(state, xs):
        qk_i, qg_i, w_i, u_i, kd_i, gl_i = xs
        v_new = u_i - jnp.einsum('bhck,bhkv->bhcv', w_i, state)
        o = jnp.einsum('bhck,bhkv->bhcv', qg_i, state) + jnp.einsum('bhij,bhjv->bhiv', qk_i, v_new)
        state = state * gl_i[..., None, None] + jnp.einsum('bhck,bhcv->bhkv', kd_i, v_new)
        return state, o

    xs = tuple(jnp.moveaxis(t, 1, 0) for t in (qk, qg, w, u, kdec, glast))
    state0 = jnp.zeros((B, H, DK, DV), jnp.float32)
    _, o = lax.scan(step, state0, xs)
    return o.transpose(1, 0, 3, 2, 4).reshape(B, S, H, DV)


def partial_rope(t, positions):
    half = ROT_DIM // 2
    inv_freq = jnp.power(ROPE_THETA, -jnp.arange(half, dtype=jnp.float32) * 2.0 / ROT_DIM)
    ang = positions.astype(jnp.float32)[..., None] * inv_freq
    cos = jnp.cos(ang)[:, :, None, None, :]
    sin = jnp.sin(ang)[:, :, None, None, :]
    x1 = t[..., :half].astype(jnp.float32)
    x2 = t[..., half:ROT_DIM].astype(jnp.float32)
    rot = jnp.concatenate([x1 * cos - x2 * sin, x2 * cos + x1 * sin], axis=-1).astype(t.dtype)
    return jnp.concatenate([rot, t[..., ROT_DIM:]], axis=-1)


def diff_attention(q, k, v, lam):
    S = q.shape[1]
    scale = DIFF_D ** -0.5
    chunk_id = jnp.arange(S) // CHUNK
    outs = []
    for start in range(0, S, Q_BLOCK):
        end = start + Q_BLOCK
        s = jnp.einsum('bqhcd,bkhcd->bchqk', q[:, start:end], k[:, :end]).astype(jnp.float32) * scale
        mask = chunk_id[start:end, None] >= chunk_id[None, :end]
        p = jax.nn.softmax(jnp.where(mask, s, NEG_INF), axis=-1)
        p = p[:, 0] - lam * p[:, 1]
        outs.append(jnp.einsum('bhqk,bkhv->bqhv', p.astype(v.dtype), v[:, :end]))
    return jnp.concatenate(outs, axis=1)


def mixer_block(h, positions, layer_idx, w_in, b_gate, conv_w, a_log, dt_bias, gdn_norm_g,
                q_norm_g, k_norm_g, lambda_q1, lambda_k1, lambda_q2, lambda_k2, subln_g,
                w_branch_a, w_branch_b, w_out):
    B, S, D = h.shape
    proj = h @ w_in

    qkv = jax.nn.silu(causal_conv(proj[..., OFF_QKV_A:OFF_Z_A], conv_w)).astype(jnp.float32)
    qa = l2_norm(qkv[..., :GDN_KEY].reshape(B, S, GDN_HEADS, GDN_DK))
    ka = l2_norm(qkv[..., GDN_KEY:2 * GDN_KEY].reshape(B, S, GDN_HEADS, GDN_DK))
    va = qkv[..., 2 * GDN_KEY:].reshape(B, S, GDN_HEADS, GDN_DV)
    z = proj[..., OFF_Z_A:OFF_BETA_A].reshape(B, S, GDN_HEADS, GDN_DV).astype(jnp.float32)
    beta = jax.nn.sigmoid(proj[..., OFF_BETA_A:OFF_ALPHA_A].astype(jnp.float32))
    g = -jnp.exp(a_log.astype(jnp.float32)) * jax.nn.softplus(
        proj[..., OFF_ALPHA_A:OFF_Q_B].astype(jnp.float32) + dt_bias.astype(jnp.float32))
    oa = gated_delta_rule(qa, ka, va, g, beta)
    oa = rms_norm(oa, gdn_norm_g.astype(jnp.float32)) * jax.nn.silu(z)
    ya = oa.astype(h.dtype).reshape(B, S, GDN_VAL) @ w_branch_a

    qb = proj[..., OFF_Q_B:OFF_K_B].reshape(B, S, DIFF_HEADS, 2, DIFF_D)
    kb = proj[..., OFF_K_B:OFF_V_B].reshape(B, S, DIFF_HEADS, 2, DIFF_D)
    vb = proj[..., OFF_V_B:OFF_GATE].reshape(B, S, DIFF_HEADS, 2 * DIFF_D)
    qb = partial_rope(rms_norm(qb, q_norm_g), positions)
    kb = partial_rope(rms_norm(kb, k_norm_g), positions)
    lam_init = lambda_init(layer_idx)
    lam = (jnp.exp(jnp.sum(lambda_q1.astype(jnp.float32) * lambda_k1.astype(jnp.float32)))
           - jnp.exp(jnp.sum(lambda_q2.astype(jnp.float32) * lambda_k2.astype(jnp.float32))) + lam_init)
    ob = diff_attention(qb, kb, vb, lam)
    ob = rms_norm(ob, subln_g) * (1.0 - lam_init)
    yb = ob.reshape(B, S, DIFF_V) @ w_branch_b

    gates = jax.nn.sigmoid(proj[..., OFF_GATE:] + b_gate)
    merged = gates[..., :D] * ya + gates[..., D:] * yb
    return merged @ w_out


def moe_ffn(h, w_router, b_router, w_gate_up, b_gate_up, w_down, b_down):
    B, S, D = h.shape
    N = B * S
    ht = h.reshape(N, D)
    logits = (ht @ w_router + b_router).astype(jnp.float32)
    top_vals, top_idx = lax.top_k(logits, TOP_K)
    top_w = jax.nn.softmax(top_vals, axis=-1)
    A = N * TOP_K
    flat_e = top_idx.reshape(A).astype(jnp.int32)
    flat_w = top_w.reshape(A)
    order = jnp.argsort(flat_e, stable=True)
    sorted_e = flat_e[order]
    counts = jnp.zeros((N_EXPERTS,), jnp.int32).at[flat_e].add(1)
    padded = (counts + EXPERT_BLOCK - 1) // EXPERT_BLOCK * EXPERT_BLOCK
    pad_end = jnp.cumsum(padded)
    pad_start = pad_end - padded
    start = jnp.cumsum(counts) - counts
    dest = pad_start[sorted_e] + jnp.arange(A, dtype=jnp.int32) - start[sorted_e]
    P = A + N_EXPERTS * EXPERT_BLOCK
    NB = P // EXPERT_BLOCK
    slot_token = jnp.zeros((P,), jnp.int32).at[dest].set((order // TOP_K).astype(jnp.int32))
    slot_gate = jnp.zeros((P,), jnp.float32).at[dest].set(flat_w[order])
    block_expert = jnp.minimum(
        jnp.searchsorted(pad_end, jnp.arange(NB, dtype=jnp.int32) * EXPERT_BLOCK, side='right'),
        N_EXPERTS - 1)

    def expert_block(args):
        e, tok = args
        xb = ht[tok]
        gu = xb @ w_gate_up[e] + b_gate_up[e]
        gate = jnp.minimum(gu[:, :D_FF], SWIGLU_LIMIT)
        up = jnp.clip(gu[:, D_FF:], -SWIGLU_LIMIT, SWIGLU_LIMIT)
        act = (up + 1.0) * gate * jax.nn.sigmoid(SWIGLU_ALPHA * gate)
        return act @ w_down[e] + b_down[e]

    out = lax.map(expert_block, (block_expert, slot_token.reshape(NB, EXPERT_BLOCK)))
    y = jnp.zeros((N, D), jnp.float32).at[slot_token].add(
        out.reshape(P, D).astype(jnp.float32) * slot_gate[:, None])
    return y.astype(h.dtype).reshape(B, S, D)


def setup_inputs(seed: int = 0) -> dict:
    key = jax.random.key(seed)
    ks = jax.random.split(key, 32)
    f32 = jnp.float32
    L, D = DEPTH, D_MODEL

    def nrm(k, shape, scale):
        return jax.random.normal(k, shape, f32) * scale

    x = nrm(ks[0], (BATCH, SEQ, D), 1.0)
    positions = (jnp.arange(SEQ, dtype=jnp.int32)[None, :]
                 + jax.random.randint(ks[1], (BATCH, 1), 0, MAX_STREAM_OFFSET, dtype=jnp.int32))
    return {
        'x': x,
        'positions': positions,
        'norm1_g': 1.0 + nrm(ks[2], (L, D), 0.02),
        'w_in': nrm(ks[3], (L, D, IN_COLS), D ** -0.5),
        'b_gate': nrm(ks[4], (L, N_BRANCH * D), 0.02),
        'conv_w': nrm(ks[5], (L, CONV_K, GDN_CONV_DIM), CONV_K ** -0.5),
        'a_log': jnp.log(jax.random.uniform(ks[6], (L, GDN_HEADS), f32, 1.0, 16.0)),
        'dt_bias': 1.0 + nrm(ks[7], (L, GDN_HEADS), 0.1),
        'gdn_norm_g': 1.0 + nrm(ks[8], (L, GDN_DV), 0.02),
        'q_norm_g': 1.0 + nrm(ks[9], (L, DIFF_D), 0.02),
        'k_norm_g': 1.0 + nrm(ks[10], (L, DIFF_D), 0.02),
        'lambda_q1': nrm(ks[11], (L, DIFF_D), 0.1),
        'lambda_k1': nrm(ks[12], (L, DIFF_D), 0.1),
        'lambda_q2': nrm(ks[13], (L, DIFF_D), 0.1),
        'lambda_k2': nrm(ks[14], (L, DIFF_D), 0.1),
        'subln_g': 1.0 + nrm(ks[15], (L, 2 * DIFF_D), 0.02),
        'w_branch_a': nrm(ks[16], (L, GDN_VAL, D), GDN_VAL ** -0.5),
        'w_branch_b': nrm(ks[17], (L, DIFF_V, D), DIFF_V ** -0.5),
        'w_out': nrm(ks[18], (L, D, D), D ** -0.5),
        'norm2_g': 1.0 + nrm(ks[19], (L, D), 0.02),
        'w_router': nrm(ks[20], (L, D, N_EXPERTS), D ** -0.5),
        'b_router': nrm(ks[21], (L, N_EXPERTS), 0.01),
        'w_gate_up': nrm(ks[22], (L, N_EXPERTS, D, 2 * D_FF), D ** -0.5),
        'b_gate_up': nrm(ks[23], (L, N_EXPERTS, 2 * D_FF), 0.01),
        'w_down': nrm(ks[24], (L, N_EXPERTS, D_FF, D), D_FF ** -0.5),
        'b_down': nrm(ks[25], (L, N_EXPERTS, D), 0.01),
    }


def reference(x, positions, norm1_g, w_in, b_gate, conv_w, a_log, dt_bias, gdn_norm_g,
              q_norm_g, k_norm_g, lambda_q1, lambda_k1, lambda_q2, lambda_k2, subln_g,
              w_branch_a, w_branch_b, w_out, norm2_g, w_router, b_router,
              w_gate_up, b_gate_up, w_down, b_down):
    for l in range(DEPTH):
        h = rms_norm(x, norm1_g[l])
        x = x + mixer_block(h, positions, l, w_in[l], b_gate[l], conv_w[l], a_log[l], dt_bias[l],
                            gdn_norm_g[l], q_norm_g[l], k_norm_g[l], lambda_q1[l], lambda_k1[l],
                            lambda_q2[l], lambda_k2[l], subln_g[l], w_branch_a[l], w_branch_b[l],
                            w_out[l])
        h2 = rms_norm(x, norm2_g[l])
        x = x + moe_ffn(h2, w_router[l], b_router[l], w_gate_up[l], b_gate_up[l],
                        w_down[l], b_down[l])
    return x
```

```python
import functools
import math

import numpy as np
import jax
import jax.numpy as jnp
from jax import lax
from jax.experimental import pallas as pl
from jax.experimental.pallas import tpu as pltpu

F32 = jnp.float32
BF16 = jnp.bfloat16

NORM_EPS = 1e-6
NEG_INF = -1e30
CHUNK = 64
HEAD_DIM = 128
ROT_DIM = HEAD_DIM // 4
ROPE_THETA = 500000.0
TOP_K = 4
SWIGLU_ALPHA = 1.702
SWIGLU_LIMIT = 7.0
LANES = 128
VMEM_LIMIT_BYTES = 60 * 1024 * 1024


def _pick_tile(n, target, quantum):
    if n <= target:
        return n
    t = (target // quantum) * quantum
    while t > quantum and n % t:
        t -= quantum
    assert n % t == 0, (n, target, quantum)
    return t


def _params(*sem):
    return pltpu.CompilerParams(dimension_semantics=sem, vmem_limit_bytes=VMEM_LIMIT_BYTES)


def _sigmoid(x):
    return 1.0 / (1.0 + jnp.exp(-x))


def _softplus(x):
    return jnp.maximum(x, 0.0) + jnp.log(1.0 + jnp.exp(-jnp.abs(x)))


def _dot(a, b):
    return jnp.dot(a.astype(BF16), b.astype(BF16), preferred_element_type=F32)


def _dot_nt(a, b):
    return lax.dot_general(a.astype(BF16), b.astype(BF16), (((1,), (1,)), ((), ())),
                           preferred_element_type=F32)


def _dot_tn(a, b):
    return lax.dot_general(a.astype(BF16), b.astype(BF16), (((0,), (0,)), ((), ())),
                           preferred_element_type=F32)


def _rmsnorm_kernel(x_ref, g_ref, o_ref):
    x = x_ref[...]
    y = x * lax.rsqrt(jnp.mean(x * x, axis=-1, keepdims=True) + NORM_EPS)
    o_ref[...] = (y * g_ref[...]).astype(o_ref.dtype)


def _rmsnorm(x2d, g, out_dtype):
    n, d = x2d.shape
    tm = _pick_tile(n, 512, 8)
    return pl.pallas_call(
        _rmsnorm_kernel,
        grid=(n // tm,),
        in_specs=[pl.BlockSpec((tm, d), lambda i: (i, 0)), pl.BlockSpec((1, d), lambda i: (0, 0))],
        out_specs=pl.BlockSpec((tm, d), lambda i: (i, 0)),
        out_shape=jax.ShapeDtypeStruct((n, d), out_dtype),
        compiler_params=_params("parallel"),
        name="rmsnorm",
    )(x2d, g.reshape(1, d))


def _matmul_kernel(*refs, has_res):
    if has_res:
        x_ref, w_ref, r_ref, o_ref, wb_ref = refs
    else:
        x_ref, w_ref, o_ref, wb_ref = refs

    @pl.when(pl.program_id(1) == 0)
    def _():
        wb_ref[...] = w_ref[...].astype(BF16)

    acc = jnp.dot(x_ref[...], wb_ref[...], preferred_element_type=F32)
    if has_res:
        acc = acc + r_ref[...]
    o_ref[...] = acc.astype(o_ref.dtype)


def _matmul(x, w, col0, ncols, out_dtype, res=None, name="matmul"):
    m, k = x.shape
    tn = _pick_tile(ncols, 1024, LANES)
    tm = _pick_tile(m, 512, 8)
    assert col0 % tn == 0
    cb = col0 // tn
    in_specs = [pl.BlockSpec((tm, k), lambda j, i: (i, 0)),
                pl.BlockSpec((k, tn), lambda j, i: (0, cb + j))]
    args = [x, w]
    if res is not None:
        in_specs.append(pl.BlockSpec((tm, tn), lambda j, i: (i, j)))
        args.append(res)
    return pl.pallas_call(
        functools.partial(_matmul_kernel, has_res=res is not None),
        grid=(ncols // tn, m // tm),
        in_specs=in_specs,
        out_specs=pl.BlockSpec((tm, tn), lambda j, i: (i, j)),
        out_shape=jax.ShapeDtypeStruct((m, ncols), out_dtype),
        scratch_shapes=[pltpu.VMEM((k, tn), BF16)],
        compiler_params=_params("parallel", "arbitrary"),
        name=name,
    )(*args)


def _gdn_kernel(q_ref, k_ref, v_ref, z_ref, ba_ref, cwq_ref, cwk_ref, cwv_ref, alog_ref, dtb_ref,
                ng_ref, o_ref, state_ref, tq_ref, tk_ref, tv_ref, eq_ref, ek_ref, ev_ref, *, hg):
    C = CHUNK
    D = HEAD_DIM

    @pl.when(pl.program_id(2) == 0)
    def _():
        state_ref[...] = jnp.zeros_like(state_ref)
        tq_ref[...] = jnp.zeros_like(tq_ref)
        tk_ref[...] = jnp.zeros_like(tk_ref)
        tv_ref[...] = jnp.zeros_like(tv_ref)

    def conv_silu(u_ref, tail_ref, ext_ref, w_ref):
        u = u_ref[0]
        ext_ref[0:8, :] = tail_ref[...]
        ext_ref[8:8 + C, :] = u
        w = w_ref[...]
        y = ext_ref[5:5 + C, :] * w[0:1, :]
        y = y + ext_ref[6:6 + C, :] * w[1:2, :]
        y = y + ext_ref[7:7 + C, :] * w[2:3, :]
        y = y + u * w[3:4, :]
        tail_ref[...] = u[C - 8:C, :]
        return y * _sigmoid(y)

    qc = conv_silu(q_ref, tq_ref, eq_ref, cwq_ref)
    kc = conv_silu(k_ref, tk_ref, ek_ref, cwk_ref)
    vc = conv_silu(v_ref, tv_ref, ev_ref, cwv_ref)

    ba = ba_ref[0]
    beta_full = _sigmoid(ba)
    g_full = -jnp.exp(alog_ref[0]) * _softplus(ba + dtb_ref[0])
    row = lax.broadcasted_iota(jnp.int32, (C, C), 0)
    col = lax.broadcasted_iota(jnp.int32, (C, C), 1)
    tril = row >= col
    strict = row > col
    eye = (row == col).astype(F32)
    gcum_full = jnp.dot(tril.astype(F32), g_full, preferred_element_type=F32,
                        precision=lax.Precision.HIGHEST)
    gcum_t = gcum_full.T
    ng = ng_ref[...]

    for i in range(hg):
        sl = slice(i * D, (i + 1) * D)
        qh, kh, vh = qc[:, sl], kc[:, sl], vc[:, sl]
        qn = qh * lax.rsqrt(jnp.sum(qh * qh, axis=-1, keepdims=True) + NORM_EPS) * (D ** -0.5)
        kn = kh * lax.rsqrt(jnp.sum(kh * kh, axis=-1, keepdims=True) + NORM_EPS)
        beta_c = beta_full[:, i:i + 1]
        gc_col = gcum_full[:, hg + i:hg + i + 1]
        gc_row = gcum_t[hg + i:hg + i + 1, :]
        g_last = gc_col[C - 1:C, :]
        decay = jnp.where(tril, jnp.exp(jnp.where(tril, gc_col - gc_row, 0.0)), 0.0)
        eg = jnp.exp(gc_col)
        kb = kn * beta_c
        kq = _dot_nt(jnp.concatenate([kb, qn], axis=0), kn)
        nmat = jnp.where(strict, kq[:C] * decay, 0.0)
        qk = kq[C:] * decay
        p = eye - nmat
        mpow = _dot(nmat, nmat)
        for _ in range(4):
            x = _dot(jnp.concatenate([p, mpow], axis=0), mpow)
            p = p + x[:C]
            mpow = x[C:]
        tmat = p + _dot(p, mpow)
        uw = _dot(tmat, jnp.concatenate([vh * beta_c, kb * eg], axis=1))
        u = uw[:, :D]
        w = uw[:, D:]
        s = state_ref[i]
        ws = _dot(jnp.concatenate([w, qn * eg], axis=0), s)
        v_new = u - ws[:C]
        o = ws[C:] + _dot(qk, v_new)
        kdec = kn * jnp.exp(g_last - gc_col)
        state_ref[i] = s * jnp.exp(g_last) + _dot_tn(kdec, v_new)
        on = o * lax.rsqrt(jnp.mean(o * o, axis=-1, keepdims=True) + NORM_EPS) * ng
        zh = z_ref[0, :, sl]
        o_ref[0, :, sl] = (on * (zh * _sigmoid(zh))).astype(o_ref.dtype)


def _gdn(proj_a, ba, conv_w, alog_p, dtb_p, norm_g, heads, hg):
    b, s, _ = proj_a.shape
    ngroups = heads // hg
    w = hg * HEAD_DIM
    nc = s // CHUNK

    def colspec(off):
        return pl.BlockSpec((1, CHUNK, w), lambda bi, gi, ci, off=off: (bi, ci, off * ngroups + gi))

    def cwspec(off):
        return pl.BlockSpec((4, w), lambda bi, gi, ci, off=off: (0, off * ngroups + gi))

    vec = pl.BlockSpec((1, 1, LANES), lambda bi, gi, ci: (gi, 0, 0))
    return pl.pallas_call(
        functools.partial(_gdn_kernel, hg=hg),
        grid=(b, ngroups, nc),
        in_specs=[colspec(0), colspec(1), colspec(2), colspec(3),
                  pl.BlockSpec((1, CHUNK, LANES), lambda bi, gi, ci: (bi, ci, gi)),
                  cwspec(0), cwspec(1), cwspec(2), vec, vec,
                  pl.BlockSpec((1, HEAD_DIM), lambda bi, gi, ci: (0, 0))],
        out_specs=pl.BlockSpec((1, CHUNK, w), lambda bi, gi, ci: (bi, ci, gi)),
        out_shape=jax.ShapeDtypeStruct((b, s, heads * HEAD_DIM), BF16),
        scratch_shapes=[pltpu.VMEM((hg, HEAD_DIM, HEAD_DIM), F32)]
        + [pltpu.VMEM((8, w), F32)] * 3 + [pltpu.VMEM((8 + CHUNK, w), F32)] * 3,
        compiler_params=_params("parallel", "parallel", "arbitrary"),
        name="gated_delta_rule",
    )(proj_a, proj_a, proj_a, proj_a, ba, conv_w, conv_w, conv_w, alog_p, dtb_p,
      norm_g.reshape(1, HEAD_DIM))


def _qknorm_rope_kernel(x_ref, pos_ref, g_ref, freq_ref, o_ref, *, scale):
    pos = pos_ref[0]
    ang = pos * freq_ref[...]
    cos = jnp.cos(ang)
    sin = jnp.sin(ang)
    lane = lax.broadcasted_iota(jnp.int32, ang.shape, 1)
    first_half = lane < ROT_DIM // 2
    sin_signed = jnp.where(first_half, -sin, sin)
    g = g_ref[...]
    ngroups = x_ref.shape[-1] // HEAD_DIM
    for j in range(ngroups):
        sl = slice(j * HEAD_DIM, (j + 1) * HEAD_DIM)
        x = x_ref[0, :, sl]
        xn = x * lax.rsqrt(jnp.mean(x * x, axis=-1, keepdims=True) + NORM_EPS) * g
        partner = jnp.where(first_half, pltpu.roll(xn, HEAD_DIM - ROT_DIM // 2, 1),
                            pltpu.roll(xn, ROT_DIM // 2, 1))
        o_ref[0, :, sl] = ((xn * cos + partner * sin_signed) * scale).astype(o_ref.dtype)


def _qknorm_rope(proj_b, col_block, width, pos_f, g, scale, name):
    b, s, _ = proj_b.shape
    ts = _pick_tile(s, 256, 8)
    half = ROT_DIM // 2
    inv_freq = np.power(np.float32(ROPE_THETA),
                        -np.arange(half, dtype=np.float32) * np.float32(2.0) / np.float32(ROT_DIM))
    freq = np.zeros((1, LANES), np.float32)
    freq[0, :half] = inv_freq
    freq[0, half:ROT_DIM] = inv_freq
    return pl.pallas_call(
        functools.partial(_qknorm_rope_kernel, scale=scale),
        grid=(b, s // ts),
        in_specs=[pl.BlockSpec((1, ts, width), lambda bi, si: (bi, si, col_block)),
                  pl.BlockSpec((1, ts, 1), lambda bi, si: (bi, si, 0)),
                  pl.BlockSpec((1, HEAD_DIM), lambda bi, si: (0, 0)),
                  pl.BlockSpec((1, LANES), lambda bi, si: (0, 0))],
        out_specs=pl.BlockSpec((1, ts, width), lambda bi, si: (bi, si, 0)),
        out_shape=jax.ShapeDtypeStruct((b, s, width), BF16),
        compiler_params=_params("parallel", "parallel"),
        name=name,
    )(proj_b, pos_f, g.reshape(1, HEAD_DIM), jnp.asarray(freq))


def _diff_attn_kernel(q_ref, k_ref, v_ref, lq1_ref, lk1_ref, lq2_ref, lk2_ref, sg_ref, o_ref,
                      acc1_ref, acc2_ref, m1_ref, l1_ref, m2_ref, l2_ref, *, tq, lam_init):
    D = HEAD_DIM
    qi = pl.program_id(2)
    q1 = q_ref[0, :, :D]
    q2 = q_ref[0, :, D:]
    acc1_ref[...] = jnp.zeros_like(acc1_ref)
    acc2_ref[...] = jnp.zeros_like(acc2_ref)
    m1_ref[...] = jnp.full_like(m1_ref, NEG_INF)
    m2_ref[...] = jnp.full_like(m2_ref, NEG_INF)
    l1_ref[...] = jnp.zeros_like(l1_ref)
    l2_ref[...] = jnp.zeros_like(l2_ref)

    def update(s, vblk, acc_ref, m_ref, l_ref):
        m_old = m_ref[...]
        m_new = jnp.maximum(m_old, jnp.max(s, axis=-1, keepdims=True))
        alpha = jnp.exp(m_old - m_new)
        p = jnp.exp(s - m_new)
        l_ref[...] = alpha * l_ref[...] + jnp.sum(p, axis=-1, keepdims=True)
        acc_ref[...] = alpha * acc_ref[...] + jnp.dot(p.astype(BF16), vblk, preferred_element_type=F32)
        m_ref[...] = m_new

    def step(start, mask):
        kblk = k_ref[0, pl.ds(start, tq), :]
        vblk = v_ref[0, pl.ds(start, tq), :]
        s1 = _dot_nt(q1, kblk[:, :D])
        s2 = _dot_nt(q2, kblk[:, D:])
        if mask is not None:
            s1 = jnp.where(mask, s1, NEG_INF)
            s2 = jnp.where(mask, s2, NEG_INF)
        update(s1, vblk, acc1_ref, m1_ref, l1_ref)
        update(s2, vblk, acc2_ref, m2_ref, l2_ref)

    def body(j, carry):
        step(pl.multiple_of(j * tq, tq), None)
        return carry

    lax.fori_loop(0, qi, body, 0)
    row = lax.broadcasted_iota(jnp.int32, (tq, tq), 0) // CHUNK
    col = lax.broadcasted_iota(jnp.int32, (tq, tq), 1) // CHUNK
    step(pl.multiple_of(qi * tq, tq), row >= col)

    lam = (jnp.exp(jnp.sum(lq1_ref[...] * lk1_ref[...], axis=-1, keepdims=True))
           - jnp.exp(jnp.sum(lq2_ref[...] * lk2_ref[...], axis=-1, keepdims=True)) + lam_init)
    o = acc1_ref[...] / l1_ref[...] - lam * (acc2_ref[...] / l2_ref[...])
    on = o * lax.rsqrt(jnp.mean(o * o, axis=-1, keepdims=True) + NORM_EPS)
    o_ref[0] = (on * sg_ref[...] * (1.0 - lam_init)).astype(o_ref.dtype)


def _diff_attn(q, k, v_src, v_col_block, lq1, lk1, lq2, lk2, subln_g, heads, lam_init):
    b, s, _ = q.shape
    hw = 2 * HEAD_DIM
    tq = _pick_tile(s, 256, CHUNK)
    vec = pl.BlockSpec((1, HEAD_DIM), lambda bi, hi, qi: (0, 0))
    return pl.pallas_call(
        functools.partial(_diff_attn_kernel, tq=tq, lam_init=lam_init),
        grid=(b, heads, s // tq),
        in_specs=[pl.BlockSpec((1, tq, hw), lambda bi, hi, qi: (bi, qi, hi)),
                  pl.BlockSpec((1, s, hw), lambda bi, hi, qi: (bi, 0, hi)),
                  pl.BlockSpec((1, s, hw), lambda bi, hi, qi: (bi, 0, v_col_block + hi)),
                  vec, vec, vec, vec,
                  pl.BlockSpec((1, hw), lambda bi, hi, qi: (0, 0))],
        out_specs=pl.BlockSpec((1, tq, hw), lambda bi, hi, qi: (bi, qi, hi)),
        out_shape=jax.ShapeDtypeStruct((b, s, heads * hw), BF16),
        scratch_shapes=[pltpu.VMEM((tq, hw), F32)] * 2 + [pltpu.VMEM((tq, 1), F32)] * 4,
        compiler_params=_params("parallel", "parallel", "arbitrary"),
        name="diff_attention",
    )(q, k, v_src, lq1.reshape(1, -1), lk1.reshape(1, -1), lq2.reshape(1, -1), lk2.reshape(1, -1),
      subln_g.reshape(1, hw))


def _merge_kernel(oa_ref, ob_ref, wa_ref, wb_ref, ga_ref, gb_ref, ba_ref, bb_ref, o_ref, wa_s, wb_s):
    @pl.when(pl.program_id(1) == 0)
    def _():
        wa_s[...] = wa_ref[...].astype(BF16)
        wb_s[...] = wb_ref[...].astype(BF16)

    ya = jnp.dot(oa_ref[...], wa_s[...], preferred_element_type=F32)
    yb = jnp.dot(ob_ref[...], wb_s[...], preferred_element_type=F32)
    merged = _sigmoid(ga_ref[...] + ba_ref[...]) * ya + _sigmoid(gb_ref[...] + bb_ref[...]) * yb
    o_ref[...] = merged.astype(o_ref.dtype)


def _merge(oa, ob, w_a, w_b, proj_b, gate_col0, b_gate, d):
    m, ka = oa.shape
    kb = ob.shape[1]
    tn = _pick_tile(d, 512, LANES)
    tm = _pick_tile(m, 512, 8)
    assert gate_col0 % tn == 0
    ga = gate_col0 // tn
    gb = (gate_col0 + d) // tn
    nb = d // tn
    bg = b_gate.reshape(1, 2 * d)
    return pl.pallas_call(
        _merge_kernel,
        grid=(nb, m // tm),
        in_specs=[pl.BlockSpec((tm, ka), lambda j, i: (i, 0)),
                  pl.BlockSpec((tm, kb), lambda j, i: (i, 0)),
                  pl.BlockSpec((ka, tn), lambda j, i: (0, j)),
                  pl.BlockSpec((kb, tn), lambda j, i: (0, j)),
                  pl.BlockSpec((tm, tn), lambda j, i: (i, ga + j)),
                  pl.BlockSpec((tm, tn), lambda j, i: (i, gb + j)),
                  pl.BlockSpec((1, tn), lambda j, i: (0, j)),
                  pl.BlockSpec((1, tn), lambda j, i: (0, nb + j))],
        out_specs=pl.BlockSpec((tm, tn), lambda j, i: (i, j)),
        out_shape=jax.ShapeDtypeStruct((m, d), BF16),
        scratch_shapes=[pltpu.VMEM((ka, tn), BF16), pltpu.VMEM((kb, tn), BF16)],
        compiler_params=_params("parallel", "arbitrary"),
        name="branch_merge",
    )(oa, ob, w_a, w_b, proj_b, proj_b, bg, bg)


def _router_kernel(x_ref, g_ref, wr_ref, br_ref, h_ref, ti_ref, tw_ref, rk_ref, cnt_ref, carry_ref,
                   *, n_experts):
    @pl.when(pl.program_id(0) == 0)
    def _():
        carry_ref[...] = jnp.zeros_like(carry_ref)

    x = x_ref[...]
    h = x * lax.rsqrt(jnp.mean(x * x, axis=-1, keepdims=True) + NORM_EPS) * g_ref[...]
    h_ref[...] = h
    logits = jnp.dot(h, wr_ref[...], preferred_element_type=F32,
                     precision=lax.Precision.HIGHEST) + br_ref[...]
    tm = x.shape[0]
    lane = lax.broadcasted_iota(jnp.int32, (tm, LANES), 1)
    lane_f = lane.astype(F32)
    cur = jnp.where(lane < n_experts, logits, -jnp.inf)
    vals, hots = [], []
    ti = jnp.zeros((tm, LANES), F32)
    for kk in range(TOP_K):
        mx = jnp.max(cur, axis=-1, keepdims=True)
        idx = jnp.min(jnp.where(cur == mx, lane_f, float(LANES)), axis=-1, keepdims=True)
        hot = lane_f == idx
        vals.append(mx)
        hots.append(hot)
        ti = jnp.where(lane == kk, idx, ti)
        cur = jnp.where(hot, -jnp.inf, cur)
    exps = [jnp.exp(v - vals[0]) for v in vals]
    denom = exps[0] + exps[1] + exps[2] + exps[3]
    tw = jnp.zeros((tm, LANES), F32)
    for kk in range(TOP_K):
        tw = jnp.where(lane == kk, exps[kk] / denom, tw)
    multihot = jnp.zeros((tm, LANES), F32)
    for hot in hots:
        multihot = multihot + hot.astype(F32)
    r = lax.broadcasted_iota(jnp.int32, (tm, tm), 0)
    c = lax.broadcasted_iota(jnp.int32, (tm, tm), 1)
    prefix = jnp.dot((r > c).astype(BF16), multihot.astype(BF16), preferred_element_type=F32)
    prefix = prefix + carry_ref[...]
    rk = jnp.zeros((tm, LANES), F32)
    for kk in range(TOP_K):
        rk_k = jnp.sum(jnp.where(hots[kk], prefix, 0.0), axis=-1, keepdims=True)
        rk = jnp.where(lane == kk, rk_k, rk)
    carry = carry_ref[...] + jnp.sum(multihot, axis=0, keepdims=True)
    carry_ref[...] = carry
    cnt_ref[...] = carry.astype(jnp.int32)
    ti_ref[...] = ti.astype(jnp.int32)
    tw_ref[...] = tw
    rk_ref[...] = rk.astype(jnp.int32)


def _router(x1, g, w_router, b_router):
    n, d = x1.shape
    e = w_router.shape[1]
    tm = _pick_tile(n, 256, 8)
    wr = jnp.zeros((d, LANES), F32).at[:, :e].set(w_router)
    br = jnp.zeros((1, LANES), F32).at[0, :e].set(b_router)
    row = pl.BlockSpec((tm, LANES), lambda i: (i, 0))
    return pl.pallas_call(
        functools.partial(_router_kernel, n_experts=e),
        grid=(n // tm,),
        in_specs=[pl.BlockSpec((tm, d), lambda i: (i, 0)),
                  pl.BlockSpec((1, d), lambda i: (0, 0)),
                  pl.BlockSpec((d, LANES), lambda i: (0, 0)),
                  pl.BlockSpec((1, LANES), lambda i: (0, 0))],
        out_specs=[pl.BlockSpec((tm, d), lambda i: (i, 0)), row, row, row,
                   pl.BlockSpec((1, LANES), lambda i: (0, 0))],
        out_shape=[jax.ShapeDtypeStruct((n, d), F32),
                   jax.ShapeDtypeStruct((n, LANES), jnp.int32),
                   jax.ShapeDtypeStruct((n, LANES), F32),
                   jax.ShapeDtypeStruct((n, LANES), jnp.int32),
                   jax.ShapeDtypeStruct((1, LANES), jnp.int32)],
        scratch_shapes=[pltpu.VMEM((1, LANES), F32)],
        compiler_params=_params("arbitrary"),
        name="router_topk",
    )(x1, g.reshape(1, d), wr, br)


def _gather_rows_kernel(tok_ref, h_ref, o_ref, sem, *, rows):
    b = pl.program_id(0)
    base = b * rows

    def issue(r, carry):
        t = tok_ref[0, 0, r]
        pltpu.make_async_copy(h_ref.at[pl.ds(t, 1)], o_ref.at[pl.ds(base + r, 1)], sem).start()
        return carry

    lax.fori_loop(0, rows, issue, 0)

    def drain(r, carry):
        pltpu.make_async_copy(h_ref.at[pl.ds(0, 1)], o_ref.at[pl.ds(0, 1)], sem).wait()
        return carry

    lax.fori_loop(0, rows, drain, 0)


def _gather_rows(h, slot_token, rows):
    n, d = h.shape
    p = slot_token.shape[0]
    nblk = p // rows
    return pl.pallas_call(
        functools.partial(_gather_rows_kernel, rows=rows),
        grid=(nblk,),
        in_specs=[pl.BlockSpec((1, 1, rows), lambda i: (i, 0, 0), memory_space=pltpu.SMEM),
                  pl.BlockSpec(memory_space=pl.ANY)],
        out_specs=pl.BlockSpec(memory_space=pl.ANY),
        out_shape=jax.ShapeDtypeStruct((p, d), h.dtype),
        scratch_shapes=[pltpu.SemaphoreType.DMA(())],
        compiler_params=_params("arbitrary"),
        name="gather_rows",
    )(slot_token.reshape(nblk, 1, rows), h)


def _expert_kernel(be_ref, x_ref, wg_ref, wu_ref, wd_ref, bg_ref, bu_ref, bd_ref, o_ref, xb_ref):
    f = pl.program_id(1)

    @pl.when(f == 0)
    def _():
        xb_ref[...] = x_ref[...].astype(BF16)
        o_ref[...] = jnp.broadcast_to(bd_ref[0], o_ref.shape)

    xb = xb_ref[...]
    g = jnp.dot(xb, wg_ref[0].astype(BF16), preferred_element_type=F32) + bg_ref[0]
    u = jnp.dot(xb, wu_ref[0].astype(BF16), preferred_element_type=F32) + bu_ref[0]
    gate = jnp.minimum(g, SWIGLU_LIMIT)
    up = jnp.clip(u, -SWIGLU_LIMIT, SWIGLU_LIMIT)
    act = (up + 1.0) * gate * _sigmoid(SWIGLU_ALPHA * gate)
    o_ref[...] += jnp.dot(act.astype(BF16), wd_ref[0].astype(BF16), preferred_element_type=F32)


def _experts(xs, block_expert, w_gate_up, b_gate_up, w_down, b_down, rows):
    p, d = xs.shape
    e, _, dff2 = w_gate_up.shape
    dff = dff2 // 2
    tf = _pick_tile(dff, 512, LANES)
    nf = dff // tf
    nblk = p // rows
    grid_spec = pltpu.PrefetchScalarGridSpec(
        num_scalar_prefetch=1,
        grid=(nblk, nf),
        in_specs=[pl.BlockSpec((rows, d), lambda b, f, be: (b, 0)),
                  pl.BlockSpec((1, d, tf), lambda b, f, be: (be[b], 0, f)),
                  pl.BlockSpec((1, d, tf), lambda b, f, be: (be[b], 0, nf + f)),
                  pl.BlockSpec((1, tf, d), lambda b, f, be: (be[b], f, 0)),
                  pl.BlockSpec((1, 1, tf), lambda b, f, be: (be[b], 0, f)),
                  pl.BlockSpec((1, 1, tf), lambda b, f, be: (be[b], 0, nf + f)),
                  pl.BlockSpec((1, 1, d), lambda b, f, be: (be[b], 0, 0))],
        out_specs=pl.BlockSpec((rows, d), lambda b, f, be: (b, 0)),
        scratch_shapes=[pltpu.VMEM((rows, d), BF16)],
    )
    return pl.pallas_call(
        _expert_kernel,
        grid_spec=grid_spec,
        out_shape=jax.ShapeDtypeStruct((p, d), F32),
        compiler_params=_params("parallel", "arbitrary"),
        name="expert_mlp",
    )(block_expert, xs, w_gate_up, w_gate_up, w_down, b_gate_up.reshape(e, 1, dff2),
      b_gate_up.reshape(e, 1, dff2), b_down.reshape(e, 1, d))


def _combine_kernel(dest_ref, x_ref, tw_ref, ys_ref, o_ref, buf_ref, sem, *, tm):
    def issue(r, carry):
        t = r // TOP_K
        kk = r % TOP_K
        pltpu.make_async_copy(ys_ref.at[pl.ds(dest_ref[0, 0, r], 1)],
                              buf_ref.at[kk, pl.ds(t, 1)], sem).start()
        return carry

    lax.fori_loop(0, tm * TOP_K, issue, 0)

    def drain(r, carry):
        pltpu.make_async_copy(ys_ref.at[pl.ds(0, 1)], buf_ref.at[0, pl.ds(0, 1)], sem).wait()
        return carry

    lax.fori_loop(0, tm * TOP_K, drain, 0)
    tw = tw_ref[...]
    y = buf_ref[0] * tw[:, 0:1]
    for kk in range(1, TOP_K):
        y = y + buf_ref[kk] * tw[:, kk:kk + 1]
    o_ref[...] = x_ref[...] + y


def _combine(x1, topw, dest, ys):
    n, d = x1.shape
    tm = _pick_tile(n, 128, 8)
    nt = n // tm
    return pl.pallas_call(
        functools.partial(_combine_kernel, tm=tm),
        grid=(nt,),
        in_specs=[pl.BlockSpec((1, 1, tm * TOP_K), lambda i: (i, 0, 0), memory_space=pltpu.SMEM),
                  pl.BlockSpec((tm, d), lambda i: (i, 0)),
                  pl.BlockSpec((tm, LANES), lambda i: (i, 0)),
                  pl.BlockSpec(memory_space=pl.ANY)],
        out_specs=pl.BlockSpec((tm, d), lambda i: (i, 0)),
        out_shape=jax.ShapeDtypeStruct((n, d), F32),
        scratch_shapes=[pltpu.VMEM((TOP_K, tm, d), F32), pltpu.SemaphoreType.DMA(())],
        compiler_params=_params("arbitrary"),
        name="moe_combine",
    )(dest.reshape(nt, 1, tm * TOP_K), x1, topw, ys)


GDN_HEADS_PER_STEP = 4
EXPERT_ROWS = 512


def _layer(x, positions, layer_idx, norm1_g, w_in, b_gate, conv_w, a_log, dt_bias, gdn_norm_g,
           q_norm_g, k_norm_g, lambda_q1, lambda_k1, lambda_q2, lambda_k2, subln_g,
           w_branch_a, w_branch_b, w_out, norm2_g, w_router, b_router,
           w_gate_up, b_gate_up, w_down, b_down):
    b, s, d = x.shape
    n = b * s
    gdn_heads = a_log.shape[0]
    gdn_dim = gdn_heads * HEAD_DIM
    diff_v = w_branch_b.shape[0]
    diff_heads = diff_v // (2 * HEAD_DIM)
    diff_qk = 2 * diff_heads * HEAD_DIM
    off_beta = 4 * gdn_dim
    off_alpha = off_beta + gdn_heads
    off_qb = off_alpha + gdn_heads
    off_gate = off_qb + 2 * diff_qk + diff_v
    assert w_in.shape[1] == off_gate + 2 * d
    n_experts = w_router.shape[1]

    x2 = x.reshape(n, d)
    h = _rmsnorm(x2, norm1_g, BF16)

    proj_a = _matmul(h, w_in, 0, off_beta, F32, name="proj_gdn")
    hg = min(GDN_HEADS_PER_STEP, gdn_heads)
    ngroups = gdn_heads // hg
    cols = []
    for gi in range(ngroups):
        cols += [off_beta + gi * hg + i for i in range(hg)] + [off_alpha + gi * hg + i for i in range(hg)]
        cols += [off_beta] * (LANES - 2 * hg)
    lane_valid = np.tile(np.arange(LANES) < 2 * hg, ngroups)
    w_ba = jnp.where(lane_valid[None, :], w_in[:, np.asarray(cols)], 0.0)
    ba = _matmul(h, w_ba, 0, ngroups * LANES, F32, name="proj_beta_alpha")
    w_b = w_in[:, off_qb:]
    proj_b = _matmul(h, w_b, 0, w_b.shape[1], F32, name="proj_attn_gate")

    alog_p = jnp.zeros((ngroups, 1, LANES), F32).at[:, 0, hg:2 * hg].set(a_log.reshape(ngroups, hg))
    dtb_p = jnp.zeros((ngroups, 1, LANES), F32).at[:, 0, hg:2 * hg].set(dt_bias.reshape(ngroups, hg))
    oa = _gdn(proj_a.reshape(b, s, off_beta), ba.reshape(b, s, ngroups * LANES), conv_w,
              alog_p, dtb_p, gdn_norm_g, gdn_heads, hg)

    pb3 = proj_b.reshape(b, s, proj_b.shape[1])
    pos_f = positions.astype(F32).reshape(b, s, 1)
    qr = _qknorm_rope(pb3, 0, diff_qk, pos_f, q_norm_g, HEAD_DIM ** -0.5, "q_norm_rope")
    kr = _qknorm_rope(pb3, 1, diff_qk, pos_f, k_norm_g, 1.0, "k_norm_rope")
    vb = pb3[:, :, 2 * diff_qk:2 * diff_qk + diff_v].astype(BF16)
    lam_init = 0.8 - 0.6 * math.exp(-0.3 * layer_idx)
    ob = _diff_attn(qr, kr, vb, 0, lambda_q1, lambda_k1, lambda_q2, lambda_k2, subln_g,
                    diff_heads, lam_init)

    merged = _merge(oa.reshape(n, gdn_dim), ob.reshape(n, diff_v), w_branch_a, w_branch_b,
                    proj_b, 2 * diff_qk + diff_v, b_gate, d)
    x1 = _matmul(merged, w_out, 0, d, F32, res=x2, name="out_proj")

    h2, topi, topw, rank, counts = _router(x1, norm2_g, w_router, b_router)
    rows = EXPERT_ROWS
    top_e = topi[:, :TOP_K]
    cnt = counts[0, :n_experts]
    padded = (cnt + rows - 1) // rows * rows
    pad_end = jnp.cumsum(padded)
    pad_start = pad_end - padded
    dest = (pad_start[top_e] + rank[:, :TOP_K]).reshape(n * TOP_K)
    p_rows = n * TOP_K + n_experts * rows
    nblk = p_rows // rows
    slot_token = jnp.zeros((p_rows,), jnp.int32).at[dest].set(
        jnp.repeat(jnp.arange(n, dtype=jnp.int32), TOP_K))
    block_expert = jnp.minimum(
        jnp.searchsorted(pad_end, jnp.arange(nblk, dtype=jnp.int32) * rows, side="right"),
        n_experts - 1).astype(jnp.int32)
    xs = _gather_rows(h2, slot_token, rows)
    ys = _experts(xs, block_expert, w_gate_up, b_gate_up, w_down, b_down, rows)
    out = _combine(x1, topw, dest, ys)
    return out.reshape(b, s, d)


def kernel(x, positions, norm1_g, w_in, b_gate, conv_w, a_log, dt_bias, gdn_norm_g, q_norm_g, k_norm_g, lambda_q1, lambda_k1, lambda_q2, lambda_k2, subln_g, w_branch_a, w_branch_b, w_out, norm2_g, w_router, b_router, w_gate_up, b_gate_up, w_down, b_down):
    depth = norm1_g.shape[0]
    for l in range(depth):
        x = _layer(x, positions, l, norm1_g[l], w_in[l], b_gate[l], conv_w[l], a_log[l], dt_bias[l],
                   gdn_norm_g[l], q_norm_g[l], k_norm_g[l], lambda_q1[l], lambda_k1[l], lambda_q2[l],
                   lambda_k2[l], subln_g[l], w_branch_a[l], w_branch_b[l], w_out[l], norm2_g[l],
                   w_router[l], b_router[l], w_gate_up[l], b_gate_up[l], w_down[l], b_down[l])
    return x
```

```python
import functools
import math

import numpy as np
import jax
import jax.numpy as jnp
from jax import lax
from jax.experimental import pallas as pl
from jax.experimental.pallas import tpu as pltpu

F32 = jnp.float32
BF16 = jnp.bfloat16

NORM_EPS = 1e-6
NEG_INF = -1e30
CHUNK = 64
HEAD_DIM = 128
ROT_DIM = HEAD_DIM // 4
ROPE_THETA = 500000.0
TOP_K = 4
SWIGLU_ALPHA = 1.702
SWIGLU_LIMIT = 7.0
LANES = 128
VMEM_LIMIT_BYTES = 60 * 1024 * 1024


def _pick_tile(n, target, quantum):
    if n <= target:
        return n
    t = (target // quantum) * quantum
    while t > quantum and n % t:
        t -= quantum
    assert n % t == 0, (n, target, quantum)
    return t


def _params(*sem):
    return pltpu.CompilerParams(dimension_semantics=sem, vmem_limit_bytes=VMEM_LIMIT_BYTES)


def _sigmoid(x):
    return 1.0 / (1.0 + jnp.exp(-x))


def _softplus(x):
    return jnp.maximum(x, 0.0) + jnp.log(1.0 + jnp.exp(-jnp.abs(x)))


def _dot(a, b):
    return jnp.dot(a.astype(BF16), b.astype(BF16), preferred_element_type=F32)


def _dot_nt(a, b):
    return lax.dot_general(a.astype(BF16), b.astype(BF16), (((1,), (1,)), ((), ())),
                           preferred_element_type=F32)


def _dot_tn(a, b):
    return lax.dot_general(a.astype(BF16), b.astype(BF16), (((0,), (0,)), ((), ())),
                           preferred_element_type=F32)


def _rmsnorm_kernel(x_ref, g_ref, o_ref):
    x = x_ref[...]
    y = x * lax.rsqrt(jnp.mean(x * x, axis=-1, keepdims=True) + NORM_EPS)
    o_ref[...] = (y * g_ref[...]).astype(o_ref.dtype)


def _rmsnorm(x2d, g, out_dtype):
    n, d = x2d.shape
    tm = _pick_tile(n, 512, 8)
    return pl.pallas_call(
        _rmsnorm_kernel,
        grid=(n // tm,),
        in_specs=[pl.BlockSpec((tm, d), lambda i: (i, 0)), pl.BlockSpec((1, d), lambda i: (0, 0))],
        out_specs=pl.BlockSpec((tm, d), lambda i: (i, 0)),
        out_shape=jax.ShapeDtypeStruct((n, d), out_dtype),
        compiler_params=_params("parallel"),
        name="rmsnorm",
    )(x2d, g.reshape(1, d))


def _matmul_kernel(*refs, has_res):
    if has_res:
        x_ref, w_ref, r_ref, o_ref, wb_ref = refs
    else:
        x_ref, w_ref, o_ref, wb_ref = refs

    @pl.when(pl.program_id(1) == 0)
    def _():
        wb_ref[...] = w_ref[...].astype(BF16)

    acc = jnp.dot(x_ref[...], wb_ref[...], preferred_element_type=F32)
    if has_res:
        acc = acc + r_ref[...]
    o_ref[...] = acc.astype(o_ref.dtype)


def _matmul(x, w, col0, ncols, out_dtype, res=None, name="matmul"):
    m, k = x.shape
    tn = _pick_tile(ncols, 1024, LANES)
    tm = _pick_tile(m, 512, 8)
    assert col0 % tn == 0
    cb = col0 // tn
    in_specs = [pl.BlockSpec((tm, k), lambda j, i: (i, 0)),
                pl.BlockSpec((k, tn), lambda j, i: (0, cb + j))]
    args = [x, w]
    if res is not None:
        in_specs.append(pl.BlockSpec((tm, tn), lambda j, i: (i, j)))
        args.append(res)
    return pl.pallas_call(
        functools.partial(_matmul_kernel, has_res=res is not None),
        grid=(ncols // tn, m // tm),
        in_specs=in_specs,
        out_specs=pl.BlockSpec((tm, tn), lambda j, i: (i, j)),
        out_shape=jax.ShapeDtypeStruct((m, ncols), out_dtype),
        scratch_shapes=[pltpu.VMEM((k, tn), BF16)],
        compiler_params=_params("parallel", "arbitrary"),
        name=name,
    )(*args)


def _gdn_kernel(q_ref, k_ref, v_ref, z_ref, ba_ref, cwq_ref, cwk_ref, cwv_ref, alog_ref, dtb_ref,
                ng_ref, o_ref, state_ref, tq_ref, tk_ref, tv_ref, eq_ref, ek_ref, ev_ref, *, hg):
    C = CHUNK
    D = HEAD_DIM

    @pl.when(pl.program_id(2) == 0)
    def _():
        state_ref[...] = jnp.zeros_like(state_ref)
        tq_ref[...] = jnp.zeros_like(tq_ref)
        tk_ref[...] = jnp.zeros_like(tk_ref)
        tv_ref[...] = jnp.zeros_like(tv_ref)

    def conv_silu(u_ref, tail_ref, ext_ref, w_ref):
        u = u_ref[0]
        ext_ref[0:8, :] = tail_ref[...]
        ext_ref[8:8 + C, :] = u
        w = w_ref[...]
        y = ext_ref[5:5 + C, :] * w[0:1, :]
        y = y + ext_ref[6:6 + C, :] * w[1:2, :]
        y = y + ext_ref[7:7 + C, :] * w[2:3, :]
        y = y + u * w[3:4, :]
        tail_ref[...] = u[C - 8:C, :]
        return y * _sigmoid(y)

    qc = conv_silu(q_ref, tq_ref, eq_ref, cwq_ref)
    kc = conv_silu(k_ref, tk_ref, ek_ref, cwk_ref)
    vc = conv_silu(v_ref, tv_ref, ev_ref, cwv_ref)

    ba = ba_ref[0]
    beta_full = _sigmoid(ba)
    g_full = -jnp.exp(alog_ref[0]) * _softplus(ba + dtb_ref[0])
    row = lax.broadcasted_iota(jnp.int32, (C, C), 0)
    col = lax.broadcasted_iota(jnp.int32, (C, C), 1)
    tril = row >= col
    strict = row > col
    eye = (row == col).astype(F32)
    gcum_full = jnp.dot(tril.astype(F32), g_full, preferred_element_type=F32,
                        precision=lax.Precision.HIGHEST)
    gcum_t = gcum_full.T
    ng = ng_ref[...]

    for i in range(hg):
        sl = slice(i * D, (i + 1) * D)
        qh, kh, vh = qc[:, sl], kc[:, sl], vc[:, sl]
        qn = qh * lax.rsqrt(jnp.sum(qh * qh, axis=-1, keepdims=True) + NORM_EPS) * (D ** -0.5)
        kn = kh * lax.rsqrt(jnp.sum(kh * kh, axis=-1, keepdims=True) + NORM_EPS)
        beta_c = beta_full[:, i:i + 1]
        gc_col = gcum_full[:, hg + i:hg + i + 1]
        gc_row = gcum_t[hg + i:hg + i + 1, :]
        g_last = gc_col[C - 1:C, :]
        decay = jnp.where(tril, jnp.exp(jnp.where(tril, gc_col - gc_row, 0.0)), 0.0)
        eg = jnp.exp(gc_col)
        kb = kn * beta_c
        kq = _dot_nt(jnp.concatenate([kb, qn], axis=0), kn)
        nmat = jnp.where(strict, kq[:C] * decay, 0.0)
        qk = kq[C:] * decay
        p = eye - nmat
        mpow = _dot(nmat, nmat)
        for _ in range(4):
            x = _dot(jnp.concatenate([p, mpow], axis=0), mpow)
            p = p + x[:C]
            mpow = x[C:]
        tmat = p + _dot(p, mpow)
        uw = _dot(tmat, jnp.concatenate([vh * beta_c, kb * eg], axis=1))
        u = uw[:, :D]
        w = uw[:, D:]
        s = state_ref[i]
        ws = _dot(jnp.concatenate([w, qn * eg], axis=0), s)
        v_new = u - ws[:C]
        o = ws[C:] + _dot(qk, v_new)
        kdec = kn * jnp.exp(g_last - gc_col)
        state_ref[i] = s * jnp.exp(g_last) + _dot_tn(kdec, v_new)
        on = o * lax.rsqrt(jnp.mean(o * o, axis=-1, keepdims=True) + NORM_EPS) * ng
        zh = z_ref[0, :, sl]
        o_ref[0, :, sl] = (on * (zh * _sigmoid(zh))).astype(o_ref.dtype)


def _gdn(proj_a, ba, conv_w, alog_p, dtb_p, norm_g, heads, hg):
    b, s, _ = proj_a.shape
    ngroups = heads // hg
    w = hg * HEAD_DIM
    nc = s // CHUNK

    def colspec(off):
        return pl.BlockSpec((1, CHUNK, w), lambda bi, gi, ci, off=off: (bi, ci, off * ngroups + gi))

    def cwspec(off):
        return pl.BlockSpec((4, w), lambda bi, gi, ci, off=off: (0, off * ngroups + gi))

    vec = pl.BlockSpec((1, 1, LANES), lambda bi, gi, ci: (gi, 0, 0))
    return pl.pallas_call(
        functools.partial(_gdn_kernel, hg=hg),
        grid=(b, ngroups, nc),
        in_specs=[colspec(0), colspec(1), colspec(2), colspec(3),
                  pl.BlockSpec((1, CHUNK, LANES), lambda bi, gi, ci: (bi, ci, gi)),
                  cwspec(0), cwspec(1), cwspec(2), vec, vec,
                  pl.BlockSpec((1, HEAD_DIM), lambda bi, gi, ci: (0, 0))],
        out_specs=pl.BlockSpec((1, CHUNK, w), lambda bi, gi, ci: (bi, ci, gi)),
        out_shape=jax.ShapeDtypeStruct((b, s, heads * HEAD_DIM), BF16),
        scratch_shapes=[pltpu.VMEM((hg, HEAD_DIM, HEAD_DIM), F32)]
        + [pltpu.VMEM((8, w), F32)] * 3 + [pltpu.VMEM((8 + CHUNK, w), F32)] * 3,
        compiler_params=_params("parallel", "parallel", "arbitrary"),
        name="gated_delta_rule",
    )(proj_a, proj_a, proj_a, proj_a, ba, conv_w, conv_w, conv_w, alog_p, dtb_p,
      norm_g.reshape(1, HEAD_DIM))


def _qknorm_rope_kernel(x_ref, pos_ref, g_ref, freq_ref, o_ref, *, scale):
    pos = pos_ref[0]
    ang = pos * freq_ref[...]
    cos = jnp.cos(ang)
    sin = jnp.sin(ang)
    lane = lax.broadcasted_iota(jnp.int32, ang.shape, 1)
    first_half = lane < ROT_DIM // 2
    sin_signed = jnp.where(first_half, -sin, sin)
    g = g_ref[...]
    ngroups = x_ref.shape[-1] // HEAD_DIM
    for j in range(ngroups):
        sl = slice(j * HEAD_DIM, (j + 1) * HEAD_DIM)
        x = x_ref[0, :, sl]
        xn = x * lax.rsqrt(jnp.mean(x * x, axis=-1, keepdims=True) + NORM_EPS) * g
        partner = jnp.where(first_half, pltpu.roll(xn, HEAD_DIM - ROT_DIM // 2, 1),
                            pltpu.roll(xn, ROT_DIM // 2, 1))
        o_ref[0, :, sl] = ((xn * cos + partner * sin_signed) * scale).astype(o_ref.dtype)


def _qknorm_rope(proj_b, col_block, width, pos_f, g, scale, name):
    b, s, _ = proj_b.shape
    ts = _pick_tile(s, 256, 8)
    half = ROT_DIM // 2
    inv_freq = np.power(np.float32(ROPE_THETA),
                        -np.arange(half, dtype=np.float32) * np.float32(2.0) / np.float32(ROT_DIM))
    freq = np.zeros((1, LANES), np.float32)
    freq[0, :half] = inv_freq
    freq[0, half:ROT_DIM] = inv_freq
    return pl.pallas_call(
        functools.partial(_qknorm_rope_kernel, scale=scale),
        grid=(b, s // ts),
        in_specs=[pl.BlockSpec((1, ts, width), lambda bi, si: (bi, si, col_block)),
                  pl.BlockSpec((1, ts, 1), lambda bi, si: (bi, si, 0)),
                  pl.BlockSpec((1, HEAD_DIM), lambda bi, si: (0, 0)),
                  pl.BlockSpec((1, LANES), lambda bi, si: (0, 0))],
        out_specs=pl.BlockSpec((1, ts, width), lambda bi, si: (bi, si, 0)),
        out_shape=jax.ShapeDtypeStruct((b, s, width), BF16),
        compiler_params=_params("parallel", "parallel"),
        name=name,
    )(proj_b, pos_f, g.reshape(1, HEAD_DIM), jnp.asarray(freq))


def _diff_attn_kernel(q_ref, k_ref, v_ref, lq1_ref, lk1_ref, lq2_ref, lk2_ref, sg_ref, o_ref,
                      acc1_ref, acc2_ref, m1_ref, l1_ref, m2_ref, l2_ref, *, tq, lam_init):
    D = HEAD_DIM
    qi = pl.program_id(2)
    q1 = q_ref[0, :, :D]
    q2 = q_ref[0, :, D:]
    acc1_ref[...] = jnp.zeros_like(acc1_ref)
    acc2_ref[...] = jnp.zeros_like(acc2_ref)
    m1_ref[...] = jnp.full_like(m1_ref, NEG_INF)
    m2_ref[...] = jnp.full_like(m2_ref, NEG_INF)
    l1_ref[...] = jnp.zeros_like(l1_ref)
    l2_ref[...] = jnp.zeros_like(l2_ref)

    def update(s, vblk, acc_ref, m_ref, l_ref):
        m_old = m_ref[...]
        m_new = jnp.maximum(m_old, jnp.max(s, axis=-1, keepdims=True))
        alpha = jnp.exp(m_old - m_new)
        p = jnp.exp(s - m_new)
        l_ref[...] = alpha * l_ref[...] + jnp.sum(p, axis=-1, keepdims=True)
        acc_ref[...] = alpha * acc_ref[...] + jnp.dot(p.astype(BF16), vblk, preferred_element_type=F32)
        m_ref[...] = m_new

    def step(start, mask):
        kblk = k_ref[0, pl.ds(start, tq), :]
        vblk = v_ref[0, pl.ds(start, tq), :]
        s1 = _dot_nt(q1, kblk[:, :D])
        s2 = _dot_nt(q2, kblk[:, D:])
        if mask is not None:
            s1 = jnp.where(mask, s1, NEG_INF)
            s2 = jnp.where(mask, s2, NEG_INF)
        update(s1, vblk, acc1_ref, m1_ref, l1_ref)
        update(s2, vblk, acc2_ref, m2_ref, l2_ref)

    def body(j, carry):
        step(pl.multiple_of(j * tq, tq), None)
        return carry

    lax.fori_loop(0, qi, body, 0)
    row = lax.broadcasted_iota(jnp.int32, (tq, tq), 0) // CHUNK
    col = lax.broadcasted_iota(jnp.int32, (tq, tq), 1) // CHUNK
    step(pl.multiple_of(qi * tq, tq), row >= col)

    lam = (jnp.exp(jnp.sum(lq1_ref[...] * lk1_ref[...], axis=-1, keepdims=True))
           - jnp.exp(jnp.sum(lq2_ref[...] * lk2_ref[...], axis=-1, keepdims=True)) + lam_init)
    o = acc1_ref[...] / l1_ref[...] - lam * (acc2_ref[...] / l2_ref[...])
    on = o * lax.rsqrt(jnp.mean(o * o, axis=-1, keepdims=True) + NORM_EPS)
    o_ref[0] = (on * sg_ref[...] * (1.0 - lam_init)).astype(o_ref.dtype)


def _diff_attn(q, k, v_src, v_col_block, lq1, lk1, lq2, lk2, subln_g, heads, lam_init):
    b, s, _ = q.shape
    hw = 2 * HEAD_DIM
    tq = _pick_tile(s, 256, CHUNK)
    vec = pl.BlockSpec((1, HEAD_DIM), lambda bi, hi, qi: (0, 0))
    return pl.pallas_call(
        functools.partial(_diff_attn_kernel, tq=tq, lam_init=lam_init),
        grid=(b, heads, s // tq),
        in_specs=[pl.BlockSpec((1, tq, hw), lambda bi, hi, qi: (bi, qi, hi)),
                  pl.BlockSpec((1, s, hw), lambda bi, hi, qi: (bi, 0, hi)),
                  pl.BlockSpec((1, s, hw), lambda bi, hi, qi: (bi, 0, v_col_block + hi)),
                  vec, vec, vec, vec,
                  pl.BlockSpec((1, hw), lambda bi, hi, qi: (0, 0))],
        out_specs=pl.BlockSpec((1, tq, hw), lambda bi, hi, qi: (bi, qi, hi)),
        out_shape=jax.ShapeDtypeStruct((b, s, heads * hw), BF16),
        scratch_shapes=[pltpu.VMEM((tq, hw), F32)] * 2 + [pltpu.VMEM((tq, 1), F32)] * 4,
        compiler_params=_params("parallel", "parallel", "arbitrary"),
        name="diff_attention",
    )(q, k, v_src, lq1.reshape(1, -1), lk1.reshape(1, -1), lq2.reshape(1, -1), lk2.reshape(1, -1),
      subln_g.reshape(1, hw))


def _merge_kernel(oa_ref, ob_ref, wa_ref, wb_ref, ga_ref, gb_ref, ba_ref, bb_ref, o_ref, wa_s, wb_s):
    @pl.when(pl.program_id(1) == 0)
    def _():
        wa_s[...] = wa_ref[...].astype(BF16)
        wb_s[...] = wb_ref[...].astype(BF16)

    ya = jnp.dot(oa_ref[...], wa_s[...], preferred_element_type=F32)
    yb = jnp.dot(ob_ref[...], wb_s[...], preferred_element_type=F32)
    merged = _sigmoid(ga_ref[...] + ba_ref[...]) * ya + _sigmoid(gb_ref[...] + bb_ref[...]) * yb
    o_ref[...] = merged.astype(o_ref.dtype)


def _merge(oa, ob, w_a, w_b, proj_b, gate_col0, b_gate, d):
    m, ka = oa.shape
    kb = ob.shape[1]
    tn = _pick_tile(d, 512, LANES)
    tm = _pick_tile(m, 512, 8)
    assert gate_col0 % tn == 0
    ga = gate_col0 // tn
    gb = (gate_col0 + d) // tn
    nb = d // tn
    bg = b_gate.reshape(1, 2 * d)
    return pl.pallas_call(
        _merge_kernel,
        grid=(nb, m // tm),
        in_specs=[pl.BlockSpec((tm, ka), lambda j, i: (i, 0)),
                  pl.BlockSpec((tm, kb), lambda j, i: (i, 0)),
                  pl.BlockSpec((ka, tn), lambda j, i: (0, j)),
                  pl.BlockSpec((kb, tn), lambda j, i: (0, j)),
                  pl.BlockSpec((tm, tn), lambda j, i: (i, ga + j)),
                  pl.BlockSpec((tm, tn), lambda j, i: (i, gb + j)),
                  pl.BlockSpec((1, tn), lambda j, i: (0, j)),
                  pl.BlockSpec((1, tn), lambda j, i: (0, nb + j))],
        out_specs=pl.BlockSpec((tm, tn), lambda j, i: (i, j)),
        out_shape=jax.ShapeDtypeStruct((m, d), BF16),
        scratch_shapes=[pltpu.VMEM((ka, tn), BF16), pltpu.VMEM((kb, tn), BF16)],
        compiler_params=_params("parallel", "arbitrary"),
        name="branch_merge",
    )(oa, ob, w_a, w_b, proj_b, proj_b, bg, bg)


def _router_kernel(x_ref, g_ref, wr_ref, br_ref, h_ref, ti_ref, tw_ref, rk_ref, cnt_ref, carry_ref,
                   *, n_experts):
    @pl.when(pl.program_id(0) == 0)
    def _():
        carry_ref[...] = jnp.zeros_like(carry_ref)

    x = x_ref[...]
    h = x * lax.rsqrt(jnp.mean(x * x, axis=-1, keepdims=True) + NORM_EPS) * g_ref[...]
    h_ref[...] = h
    logits = jnp.dot(h, wr_ref[...], preferred_element_type=F32,
                     precision=lax.Precision.HIGHEST) + br_ref[...]
    tm = x.shape[0]
    lane = lax.broadcasted_iota(jnp.int32, (tm, LANES), 1)
    lane_f = lane.astype(F32)
    cur = jnp.where(lane < n_experts, logits, -jnp.inf)
    vals, hots = [], []
    ti = jnp.zeros((tm, LANES), F32)
    for kk in range(TOP_K):
        mx = jnp.max(cur, axis=-1, keepdims=True)
        idx = jnp.min(jnp.where(cur == mx, lane_f, float(LANES)), axis=-1, keepdims=True)
        hot = lane_f == idx
        vals.append(mx)
        hots.append(hot)
        ti = jnp.where(lane == kk, idx, ti)
        cur = jnp.where(hot, -jnp.inf, cur)
    exps = [jnp.exp(v - vals[0]) for v in vals]
    denom = exps[0] + exps[1] + exps[2] + exps[3]
    tw = jnp.zeros((tm, LANES), F32)
    for kk in range(TOP_K):
        tw = jnp.where(lane == kk, exps[kk] / denom, tw)
    multihot = jnp.zeros((tm, LANES), F32)
    for hot in hots:
        multihot = multihot + hot.astype(F32)
    r = lax.broadcasted_iota(jnp.int32, (tm, tm), 0)
    c = lax.broadcasted_iota(jnp.int32, (tm, tm), 1)
    prefix = jnp.dot((r > c).astype(BF16), multihot.astype(BF16), preferred_element_type=F32)
    prefix = prefix + carry_ref[...]
    rk = jnp.zeros((tm, LANES), F32)
    for kk in range(TOP_K):
        rk_k = jnp.sum(jnp.where(hots[kk], prefix, 0.0), axis=-1, keepdims=True)
        rk = jnp.where(lane == kk, rk_k, rk)
    carry = carry_ref[...] + jnp.sum(multihot, axis=0, keepdims=True)
    carry_ref[...] = carry
    cnt_ref[...] = carry.astype(jnp.int32)
    ti_ref[...] = ti.astype(jnp.int32)
    tw_ref[...] = tw
    rk_ref[...] = rk.astype(jnp.int32)


def _router(x1, g, w_router, b_router):
    n, d = x1.shape
    e = w_router.shape[1]
    tm = _pick_tile(n, 256, 8)
    wr = jnp.zeros((d, LANES), F32).at[:, :e].set(w_router)
    br = jnp.zeros((1, LANES), F32).at[0, :e].set(b_router)
    row = pl.BlockSpec((tm, LANES), lambda i: (i, 0))
    return pl.pallas_call(
        functools.partial(_router_kernel, n_experts=e),
        grid=(n // tm,),
        in_specs=[pl.BlockSpec((tm, d), lambda i: (i, 0)),
                  pl.BlockSpec((1, d), lambda i: (0, 0)),
                  pl.BlockSpec((d, LANES), lambda i: (0, 0)),
                  pl.BlockSpec((1, LANES), lambda i: (0, 0))],
        out_specs=[pl.BlockSpec((tm, d), lambda i: (i, 0)), row, row, row,
                   pl.BlockSpec((1, LANES), lambda i: (0, 0))],
        out_shape=[jax.ShapeDtypeStruct((n, d), F32),
                   jax.ShapeDtypeStruct((n, LANES), jnp.int32),
                   jax.ShapeDtypeStruct((n, LANES), F32),
                   jax.ShapeDtypeStruct((n, LANES), jnp.int32),
                   jax.ShapeDtypeStruct((1, LANES), jnp.int32)],
        scratch_shapes=[pltpu.VMEM((1, LANES), F32)],
        compiler_params=_params("arbitrary"),
        name="router_topk",
    )(x1, g.reshape(1, d), wr, br)


def _row_copy(src_ref, src_row, dst_ref, dst_row, sem):
    return pltpu.make_async_copy(src_ref.at[pl.ds(src_row, 1)], dst_ref.at[pl.ds(dst_row, 1)], sem)


def _expert_kernel(be_ref, tok_ref, tokn_ref, h_ref, wg_ref, wu_ref, wd_ref, bg_ref, bu_ref, bd_ref,
                   o_ref, xg_ref, xb_ref, sem, *, rows, nblk):
    b = pl.program_id(0)
    f = pl.program_id(1)
    slot = b % 2

    def gather(t_ref, s):
        def body(r, carry):
            _row_copy(h_ref, t_ref[0, 0, r], xg_ref.at[s], r, sem.at[s]).start()
            return carry
        lax.fori_loop(0, rows, body, 0, unroll=8)

    @pl.when((f == 0) & (b == 0))
    def _():
        gather(tok_ref, 0)

    @pl.when((f == 0) & (b + 1 < nblk))
    def _():
        gather(tokn_ref, 1 - slot)

    @pl.when(f == 0)
    def _():
        def drain(r, carry):
            _row_copy(h_ref, 0, xg_ref.at[slot], 0, sem.at[slot]).wait()
            return carry
        lax.fori_loop(0, rows, drain, 0, unroll=8)
        xb_ref[...] = xg_ref[slot].astype(BF16)
        o_ref[...] = jnp.broadcast_to(bd_ref[0], o_ref.shape)

    xb = xb_ref[...]
    g = jnp.dot(xb, wg_ref[0].astype(BF16), preferred_element_type=F32) + bg_ref[0]
    u = jnp.dot(xb, wu_ref[0].astype(BF16), preferred_element_type=F32) + bu_ref[0]
    gate = jnp.minimum(g, SWIGLU_LIMIT)
    up = jnp.clip(u, -SWIGLU_LIMIT, SWIGLU_LIMIT)
    act = (up + 1.0) * gate * _sigmoid(SWIGLU_ALPHA * gate)
    o_ref[...] += jnp.dot(act.astype(BF16), wd_ref[0].astype(BF16), preferred_element_type=F32)


def _experts(h, slot_token, block_expert, w_gate_up, b_gate_up, w_down, b_down, rows):
    n, d = h.shape
    p = slot_token.shape[0]
    e, _, dff2 = w_gate_up.shape
    dff = dff2 // 2
    tf = _pick_tile(dff, 512, LANES)
    nf = dff // tf
    nblk = p // rows
    tok3 = slot_token.reshape(nblk, 1, rows)
    grid_spec = pltpu.PrefetchScalarGridSpec(
        num_scalar_prefetch=1,
        grid=(nblk, nf),
        in_specs=[pl.BlockSpec((1, 1, rows), lambda b, f, be: (b, 0, 0), memory_space=pltpu.SMEM),
                  pl.BlockSpec((1, 1, rows), lambda b, f, be: (jnp.minimum(b + 1, nblk - 1), 0, 0),
                               memory_space=pltpu.SMEM),
                  pl.BlockSpec(memory_space=pl.ANY),
                  pl.BlockSpec((1, d, tf), lambda b, f, be: (be[b], 0, f)),
                  pl.BlockSpec((1, d, tf), lambda b, f, be: (be[b], 0, nf + f)),
                  pl.BlockSpec((1, tf, d), lambda b, f, be: (be[b], f, 0)),
                  pl.BlockSpec((1, 1, tf), lambda b, f, be: (be[b], 0, f)),
                  pl.BlockSpec((1, 1, tf), lambda b, f, be: (be[b], 0, nf + f)),
                  pl.BlockSpec((1, 1, d), lambda b, f, be: (be[b], 0, 0))],
        out_specs=pl.BlockSpec((rows, d), lambda b, f, be: (b, 0)),
        scratch_shapes=[pltpu.VMEM((2, rows, d), F32), pltpu.VMEM((rows, d), BF16),
                        pltpu.SemaphoreType.DMA((2,))],
    )
    return pl.pallas_call(
        functools.partial(_expert_kernel, rows=rows, nblk=nblk),
        grid_spec=grid_spec,
        out_shape=jax.ShapeDtypeStruct((p, d), F32),
        compiler_params=_params("arbitrary", "arbitrary"),
        name="expert_mlp",
    )(block_expert, tok3, tok3, h, w_gate_up, w_gate_up, w_down, b_gate_up.reshape(e, 1, dff2),
      b_gate_up.reshape(e, 1, dff2), b_down.reshape(e, 1, d))


def _combine_kernel(dest_ref, destn_ref, x_ref, tw_ref, ys_ref, o_ref, buf_ref, sem, *, tm, nt):
    i = pl.program_id(0)
    slot = i % 2

    def gather(d_ref, s):
        def body(t, carry):
            for kk in range(TOP_K):
                _row_copy(ys_ref, d_ref[0, 0, t * TOP_K + kk], buf_ref.at[s, kk], t, sem.at[s]).start()
            return carry
        lax.fori_loop(0, tm, body, 0, unroll=2)

    @pl.when(i == 0)
    def _():
        gather(dest_ref, 0)

    @pl.when(i + 1 < nt)
    def _():
        gather(destn_ref, 1 - slot)

    def drain(r, carry):
        _row_copy(ys_ref, 0, buf_ref.at[slot, 0], 0, sem.at[slot]).wait()
        return carry

    lax.fori_loop(0, tm * TOP_K, drain, 0, unroll=8)
    tw = tw_ref[...]
    y = buf_ref[slot, 0] * tw[:, 0:1]
    for kk in range(1, TOP_K):
        y = y + buf_ref[slot, kk] * tw[:, kk:kk + 1]
    o_ref[...] = x_ref[...] + y


def _combine(x1, topw, dest, ys):
    n, d = x1.shape
    tm = _pick_tile(n, 128, 8)
    nt = n // tm
    dest3 = dest.reshape(nt, 1, tm * TOP_K)
    return pl.pallas_call(
        functools.partial(_combine_kernel, tm=tm, nt=nt),
        grid=(nt,),
        in_specs=[pl.BlockSpec((1, 1, tm * TOP_K), lambda i: (i, 0, 0), memory_space=pltpu.SMEM),
                  pl.BlockSpec((1, 1, tm * TOP_K), lambda i: (jnp.minimum(i + 1, nt - 1), 0, 0),
                               memory_space=pltpu.SMEM),
                  pl.BlockSpec((tm, d), lambda i: (i, 0)),
                  pl.BlockSpec((tm, LANES), lambda i: (i, 0)),
                  pl.BlockSpec(memory_space=pl.ANY)],
        out_specs=pl.BlockSpec((tm, d), lambda i: (i, 0)),
        out_shape=jax.ShapeDtypeStruct((n, d), F32),
        scratch_shapes=[pltpu.VMEM((2, TOP_K, tm, d), F32), pltpu.SemaphoreType.DMA((2,))],
        compiler_params=_params("arbitrary"),
        name="moe_combine",
    )(dest3, dest3, x1, topw, ys)


GDN_HEADS_PER_STEP = 4
EXPERT_ROWS = 512


def _layer(x, positions, layer_idx, norm1_g, w_in, b_gate, conv_w, a_log, dt_bias, gdn_norm_g,
           q_norm_g, k_norm_g, lambda_q1, lambda_k1, lambda_q2, lambda_k2, subln_g,
           w_branch_a, w_branch_b, w_out, norm2_g, w_router, b_router,
           w_gate_up, b_gate_up, w_down, b_down):
    b, s, d = x.shape
    n = b * s
    gdn_heads = a_log.shape[0]
    gdn_dim = gdn_heads * HEAD_DIM
    diff_v = w_branch_b.shape[0]
    diff_heads = diff_v // (2 * HEAD_DIM)
    diff_qk = 2 * diff_heads * HEAD_DIM
    off_beta = 4 * gdn_dim
    off_alpha = off_beta + gdn_heads
    off_qb = off_alpha + gdn_heads
    off_gate = off_qb + 2 * diff_qk + diff_v
    assert w_in.shape[1] == off_gate + 2 * d
    n_experts = w_router.shape[1]

    x2 = x.reshape(n, d)
    h = _rmsnorm(x2, norm1_g, BF16)

    proj_a = _matmul(h, w_in, 0, off_beta, F32, name="proj_gdn")
    hg = min(GDN_HEADS_PER_STEP, gdn_heads)
    ngroups = gdn_heads // hg
    cols = []
    for gi in range(ngroups):
        cols += [off_beta + gi * hg + i for i in range(hg)] + [off_alpha + gi * hg + i for i in range(hg)]
        cols += [off_beta] * (LANES - 2 * hg)
    lane_valid = np.tile(np.arange(LANES) < 2 * hg, ngroups)
    w_ba = jnp.where(lane_valid[None, :], w_in[:, np.asarray(cols)], 0.0)
    ba = _matmul(h, w_ba, 0, ngroups * LANES, F32, name="proj_beta_alpha")
    w_b = w_in[:, off_qb:]
    proj_b = _matmul(h, w_b, 0, w_b.shape[1], F32, name="proj_attn_gate")

    alog_p = jnp.zeros((ngroups, 1, LANES), F32).at[:, 0, hg:2 * hg].set(a_log.reshape(ngroups, hg))
    dtb_p = jnp.zeros((ngroups, 1, LANES), F32).at[:, 0, hg:2 * hg].set(dt_bias.reshape(ngroups, hg))
    oa = _gdn(proj_a.reshape(b, s, off_beta), ba.reshape(b, s, ngroups * LANES), conv_w,
              alog_p, dtb_p, gdn_norm_g, gdn_heads, hg)

    pb3 = proj_b.reshape(b, s, proj_b.shape[1])
    pos_f = positions.astype(F32).reshape(b, s, 1)
    qr = _qknorm_rope(pb3, 0, diff_qk, pos_f, q_norm_g, HEAD_DIM ** -0.5, "q_norm_rope")
    kr = _qknorm_rope(pb3, 1, diff_qk, pos_f, k_norm_g, 1.0, "k_norm_rope")
    vb = pb3[:, :, 2 * diff_qk:2 * diff_qk + diff_v].astype(BF16)
    lam_init = 0.8 - 0.6 * math.exp(-0.3 * layer_idx)
    ob = _diff_attn(qr, kr, vb, 0, lambda_q1, lambda_k1, lambda_q2, lambda_k2, subln_g,
                    diff_heads, lam_init)

    merged = _merge(oa.reshape(n, gdn_dim), ob.reshape(n, diff_v), w_branch_a, w_branch_b,
                    proj_b, 2 * diff_qk + diff_v, b_gate, d)
    x1 = _matmul(merged, w_out, 0, d, F32, res=x2, name="out_proj")

    h2, topi, topw, rank, counts = _router(x1, norm2_g, w_router, b_router)
    rows = EXPERT_ROWS
    top_e = topi[:, :TOP_K]
    cnt = counts[0, :n_experts]
    padded = (cnt + rows - 1) // rows * rows
    pad_end = jnp.cumsum(padded)
    pad_start = pad_end - padded
    dest = (pad_start[top_e] + rank[:, :TOP_K]).reshape(n * TOP_K)
    p_rows = n * TOP_K + n_experts * rows
    nblk = p_rows // rows
    slot_token = jnp.zeros((p_rows,), jnp.int32).at[dest].set(
        jnp.repeat(jnp.arange(n, dtype=jnp.int32), TOP_K))
    block_expert = jnp.minimum(
        jnp.searchsorted(pad_end, jnp.arange(nblk, dtype=jnp.int32) * rows, side="right"),
        n_experts - 1).astype(jnp.int32)
    ys = _experts(h2, slot_token, block_expert, w_gate_up, b_gate_up, w_down, b_down, rows)
    out = _combine(x1, topw, dest, ys)
    return out.reshape(b, s, d)


def kernel(x, positions, norm1_g, w_in, b_gate, conv_w, a_log, dt_bias, gdn_norm_g, q_norm_g, k_norm_g, lambda_q1, lambda_k1, lambda_q2, lambda_k2, subln_g, w_branch_a, w_branch_b, w_out, norm2_g, w_router, b_router, w_gate_up, b_gate_up, w_down, b_down):
    depth = norm1_g.shape[0]
    for l in range(depth):
        x = _layer(x, positions, l, norm1_g[l], w_in[l], b_gate[l], conv_w[l], a_log[l], dt_bias[l],
                   gdn_norm_g[l], q_norm_g[l], k_norm_g[l], lambda_q1[l], lambda_k1[l], lambda_q2[l],
                   lambda_k2[l], subln_g[l], w_branch_a[l], w_branch_b[l], w_out[l], norm2_g[l],
                   w_router[l], b_router[l], w_gate_up[l], b_gate_up[l], w_down[l], b_down[l])
    return x
```

```python
import functools
import math

import numpy as np
import jax
import jax.numpy as jnp
from jax import lax
from jax.experimental import pallas as pl
from jax.experimental.pallas import tpu as pltpu

F32 = jnp.float32
BF16 = jnp.bfloat16

NORM_EPS = 1e-6
NEG_INF = -1e30
CHUNK = 64
HEAD_DIM = 128
ROT_DIM = HEAD_DIM // 4
ROPE_THETA = 500000.0
TOP_K = 4
SWIGLU_ALPHA = 1.702
SWIGLU_LIMIT = 7.0
LANES = 128
VMEM_LIMIT_BYTES = 60 * 1024 * 1024


def _pick_tile(n, target, quantum):
    if n <= target:
        return n
    t = (target // quantum) * quantum
    while t > quantum and n % t:
        t -= quantum
    assert n % t == 0, (n, target, quantum)
    return t


def _params(*sem):
    return pltpu.CompilerParams(dimension_semantics=sem, vmem_limit_bytes=VMEM_LIMIT_BYTES)


def _sigmoid(x):
    return 1.0 / (1.0 + jnp.exp(-x))


def _softplus(x):
    return jnp.maximum(x, 0.0) + jnp.log(1.0 + jnp.exp(-jnp.abs(x)))


def _dot(a, b):
    return jnp.dot(a.astype(BF16), b.astype(BF16), preferred_element_type=F32)


def _dot_nt(a, b):
    return lax.dot_general(a.astype(BF16), b.astype(BF16), (((1,), (1,)), ((), ())),
                           preferred_element_type=F32)


def _dot_tn(a, b):
    return lax.dot_general(a.astype(BF16), b.astype(BF16), (((0,), (0,)), ((), ())),
                           preferred_element_type=F32)


def _rmsnorm_kernel(x_ref, g_ref, o_ref):
    x = x_ref[...]
    y = x * lax.rsqrt(jnp.mean(x * x, axis=-1, keepdims=True) + NORM_EPS)
    o_ref[...] = (y * g_ref[...]).astype(o_ref.dtype)


def _rmsnorm(x2d, g, out_dtype):
    n, d = x2d.shape
    tm = _pick_tile(n, 512, 8)
    return pl.pallas_call(
        _rmsnorm_kernel,
        grid=(n // tm,),
        in_specs=[pl.BlockSpec((tm, d), lambda i: (i, 0)), pl.BlockSpec((1, d), lambda i: (0, 0))],
        out_specs=pl.BlockSpec((tm, d), lambda i: (i, 0)),
        out_shape=jax.ShapeDtypeStruct((n, d), out_dtype),
        compiler_params=_params("parallel"),
        name="rmsnorm",
    )(x2d, g.reshape(1, d))


def _matmul_kernel(*refs, has_res):
    if has_res:
        x_ref, w_ref, r_ref, o_ref, wb_ref = refs
    else:
        x_ref, w_ref, o_ref, wb_ref = refs

    @pl.when(pl.program_id(1) == 0)
    def _():
        wb_ref[...] = w_ref[...].astype(BF16)

    acc = jnp.dot(x_ref[...], wb_ref[...], preferred_element_type=F32)
    if has_res:
        acc = acc + r_ref[...]
    o_ref[...] = acc.astype(o_ref.dtype)


def _matmul(x, w, col0, ncols, out_dtype, res=None, name="matmul"):
    m, k = x.shape
    tn = _pick_tile(ncols, 1024, LANES)
    tm = _pick_tile(m, 512, 8)
    assert col0 % tn == 0
    cb = col0 // tn
    in_specs = [pl.BlockSpec((tm, k), lambda j, i: (i, 0)),
                pl.BlockSpec((k, tn), lambda j, i: (0, cb + j))]
    args = [x, w]
    if res is not None:
        in_specs.append(pl.BlockSpec((tm, tn), lambda j, i: (i, j)))
        args.append(res)
    return pl.pallas_call(
        functools.partial(_matmul_kernel, has_res=res is not None),
        grid=(ncols // tn, m // tm),
        in_specs=in_specs,
        out_specs=pl.BlockSpec((tm, tn), lambda j, i: (i, j)),
        out_shape=jax.ShapeDtypeStruct((m, ncols), out_dtype),
        scratch_shapes=[pltpu.VMEM((k, tn), BF16)],
        compiler_params=_params("parallel", "arbitrary"),
        name=name,
    )(*args)


def _gdn_kernel(q_ref, k_ref, v_ref, z_ref, ba_ref, cwq_ref, cwk_ref, cwv_ref, alog_ref, dtb_ref,
                ng_ref, o_ref, state_ref, tq_ref, tk_ref, tv_ref, eq_ref, ek_ref, ev_ref, *, hg):
    C = CHUNK
    D = HEAD_DIM

    @pl.when(pl.program_id(2) == 0)
    def _():
        state_ref[...] = jnp.zeros_like(state_ref)
        tq_ref[...] = jnp.zeros_like(tq_ref)
        tk_ref[...] = jnp.zeros_like(tk_ref)
        tv_ref[...] = jnp.zeros_like(tv_ref)

    def conv_silu(u_ref, tail_ref, ext_ref, w_ref):
        u = u_ref[0]
        ext_ref[0:8, :] = tail_ref[...]
        ext_ref[8:8 + C, :] = u
        w = w_ref[...]
        y = ext_ref[5:5 + C, :] * w[0:1, :]
        y = y + ext_ref[6:6 + C, :] * w[1:2, :]
        y = y + ext_ref[7:7 + C, :] * w[2:3, :]
        y = y + u * w[3:4, :]
        tail_ref[...] = u[C - 8:C, :]
        return y * _sigmoid(y)

    qc = conv_silu(q_ref, tq_ref, eq_ref, cwq_ref)
    kc = conv_silu(k_ref, tk_ref, ek_ref, cwk_ref)
    vc = conv_silu(v_ref, tv_ref, ev_ref, cwv_ref)

    G = GDN_GROUP
    R = G * C
    shift_c = C.bit_length() - 1

    ba = ba_ref[0]
    beta_full = _sigmoid(ba)
    g_full = -jnp.exp(alog_ref[0]) * _softplus(ba + dtb_ref[0])
    r64 = lax.broadcasted_iota(jnp.int32, (C, C), 0)
    c64 = lax.broadcasted_iota(jnp.int32, (C, C), 1)
    gcum_full = jnp.dot((r64 >= c64).astype(F32), g_full, preferred_element_type=F32,
                        precision=lax.Precision.HIGHEST)
    gcum_t = gcum_full.T
    row = lax.broadcasted_iota(jnp.int32, (R, R), 0)
    col = lax.broadcasted_iota(jnp.int32, (R, R), 1)
    same_head = jnp.right_shift(row, shift_c) == jnp.right_shift(col, shift_c)
    tril = same_head & (row >= col)
    strict = same_head & (row > col)
    eye = (row == col).astype(F32)
    lane2 = lax.broadcasted_iota(jnp.int32, (1, 2 * D), 1)
    r2 = lax.broadcasted_iota(jnp.int32, (2 * D, 2 * D), 0)
    c2 = lax.broadcasted_iota(jnp.int32, (2 * D, 2 * D), 1)
    pair_block = (r2 < D) == (c2 < D)

    def lanes_of(x, i0):
        return jnp.concatenate([x[i0 * C:(i0 + 1) * C], x[(i0 + 1) * C:(i0 + 2) * C]], axis=1)

    for h0 in range(0, hg, G):
        heads = range(h0, h0 + G)

        def rows_of(x):
            return jnp.concatenate([x[:, i * D:(i + 1) * D] for i in heads], axis=0)

        beta_r = jnp.concatenate([beta_full[:, i:i + 1] for i in heads], axis=0)
        gc_r = jnp.concatenate([gcum_full[:, hg + i:hg + i + 1] for i in heads], axis=0)
        gc_l = jnp.concatenate([gcum_t[hg + i:hg + i + 1, :] for i in heads], axis=1)
        g_last = [gcum_full[C - 1:C, hg + i:hg + i + 1] for i in heads]
        gl_r = jnp.concatenate([jnp.broadcast_to(g, (C, 1)) for g in g_last], axis=0)

        q_r, k_r, v_r = rows_of(qc), rows_of(kc), rows_of(vc)
        qn = q_r * lax.rsqrt(jnp.sum(q_r * q_r, axis=-1, keepdims=True) + NORM_EPS) * (D ** -0.5)
        kn = k_r * lax.rsqrt(jnp.sum(k_r * k_r, axis=-1, keepdims=True) + NORM_EPS)
        decay = jnp.where(tril, jnp.exp(jnp.where(tril, gc_r - gc_l, 0.0)), 0.0)
        eg = jnp.exp(gc_r)
        kb = kn * beta_r
        gram = _dot_nt(jnp.concatenate([kb, qn], axis=0), kn)
        nmat = jnp.where(strict, gram[:R] * decay, 0.0)
        qk = gram[R:] * decay
        p = eye - nmat
        mpow = _dot(nmat, nmat)
        for _ in range(4):
            x = _dot(jnp.concatenate([p, mpow], axis=0), mpow)
            p = p + x[:R]
            mpow = x[R:]
        tmat = p + _dot(p, mpow)
        uw = _dot(tmat, jnp.concatenate([v_r * beta_r, kb * eg], axis=1))
        u_r = uw[:, :D]
        w_r = uw[:, D:]
        qg_r = qn * eg
        kdec_r = kn * jnp.exp(gl_r - gc_r)

        vnew_rows, o1_rows = [], []
        for pr in range(G // 2):
            s = state_ref[h0 // 2 + pr]
            ws = _dot(jnp.concatenate([lanes_of(w_r, 2 * pr), lanes_of(qg_r, 2 * pr)], axis=0), s)
            vnew_c = lanes_of(u_r, 2 * pr) - ws[:C]
            o1_c = ws[C:]
            upd = _dot_tn(lanes_of(kdec_r, 2 * pr), vnew_c)
            gl_lanes = jnp.where(lane2 < D, jnp.exp(g_last[2 * pr]), jnp.exp(g_last[2 * pr + 1]))
            state_ref[h0 // 2 + pr] = s * gl_lanes + jnp.where(pair_block, upd, 0.0)
            vnew_rows += [vnew_c[:, :D], vnew_c[:, D:]]
            o1_rows += [o1_c[:, :D], o1_c[:, D:]]
        o = jnp.concatenate(o1_rows, axis=0) + _dot(qk, jnp.concatenate(vnew_rows, axis=0))
        on = o * lax.rsqrt(jnp.mean(o * o, axis=-1, keepdims=True) + NORM_EPS) * ng_ref[...]
        z_r = rows_of(z_ref[0])
        out = (on * (z_r * _sigmoid(z_r))).astype(o_ref.dtype)
        for j, i in enumerate(heads):
            o_ref[0, :, i * D:(i + 1) * D] = out[j * C:(j + 1) * C]


def _gdn(proj_a, ba, conv_w, alog_p, dtb_p, norm_g, heads, hg):
    b, s, _ = proj_a.shape
    assert hg % 2 == 0 and heads % hg == 0
    ngroups = heads // hg
    w = hg * HEAD_DIM
    nc = s // CHUNK

    def colspec(off):
        return pl.BlockSpec((1, CHUNK, w), lambda bi, gi, ci, off=off: (bi, ci, off * ngroups + gi))

    def cwspec(off):
        return pl.BlockSpec((4, w), lambda bi, gi, ci, off=off: (0, off * ngroups + gi))

    vec = pl.BlockSpec((1, 1, LANES), lambda bi, gi, ci: (gi, 0, 0))
    return pl.pallas_call(
        functools.partial(_gdn_kernel, hg=hg),
        grid=(b, ngroups, nc),
        in_specs=[colspec(0), colspec(1), colspec(2), colspec(3),
                  pl.BlockSpec((1, CHUNK, LANES), lambda bi, gi, ci: (bi, ci, gi)),
                  cwspec(0), cwspec(1), cwspec(2), vec, vec,
                  pl.BlockSpec((1, HEAD_DIM), lambda bi, gi, ci: (0, 0))],
        out_specs=pl.BlockSpec((1, CHUNK, w), lambda bi, gi, ci: (bi, ci, gi)),
        out_shape=jax.ShapeDtypeStruct((b, s, heads * HEAD_DIM), BF16),
        scratch_shapes=[pltpu.VMEM((hg // 2, 2 * HEAD_DIM, 2 * HEAD_DIM), F32)]
        + [pltpu.VMEM((8, w), F32)] * 3 + [pltpu.VMEM((8 + CHUNK, w), F32)] * 3,
        compiler_params=_params("parallel", "parallel", "arbitrary"),
        name="gated_delta_rule",
    )(proj_a, proj_a, proj_a, proj_a, ba, conv_w, conv_w, conv_w, alog_p, dtb_p,
      norm_g.reshape(1, HEAD_DIM))


def _qknorm_rope_kernel(x_ref, pos_ref, g_ref, freq_ref, o_ref, *, scale):
    pos = pos_ref[0]
    ang = pos * freq_ref[...]
    cos = jnp.cos(ang)
    sin = jnp.sin(ang)
    lane = lax.broadcasted_iota(jnp.int32, ang.shape, 1)
    first_half = lane < ROT_DIM // 2
    sin_signed = jnp.where(first_half, -sin, sin)
    g = g_ref[...]
    ngroups = x_ref.shape[-1] // HEAD_DIM
    for j in range(ngroups):
        sl = slice(j * HEAD_DIM, (j + 1) * HEAD_DIM)
        x = x_ref[0, :, sl]
        xn = x * lax.rsqrt(jnp.mean(x * x, axis=-1, keepdims=True) + NORM_EPS) * g
        partner = jnp.where(first_half, pltpu.roll(xn, HEAD_DIM - ROT_DIM // 2, 1),
                            pltpu.roll(xn, ROT_DIM // 2, 1))
        o_ref[0, :, sl] = ((xn * cos + partner * sin_signed) * scale).astype(o_ref.dtype)


def _qknorm_rope(proj_b, col_block, width, pos_f, g, scale, name):
    b, s, _ = proj_b.shape
    ts = _pick_tile(s, 256, 8)
    half = ROT_DIM // 2
    inv_freq = np.power(np.float32(ROPE_THETA),
                        -np.arange(half, dtype=np.float32) * np.float32(2.0) / np.float32(ROT_DIM))
    freq = np.zeros((1, LANES), np.float32)
    freq[0, :half] = inv_freq
    freq[0, half:ROT_DIM] = inv_freq
    return pl.pallas_call(
        functools.partial(_qknorm_rope_kernel, scale=scale),
        grid=(b, s // ts),
        in_specs=[pl.BlockSpec((1, ts, width), lambda bi, si: (bi, si, col_block)),
                  pl.BlockSpec((1, ts, 1), lambda bi, si: (bi, si, 0)),
                  pl.BlockSpec((1, HEAD_DIM), lambda bi, si: (0, 0)),
                  pl.BlockSpec((1, LANES), lambda bi, si: (0, 0))],
        out_specs=pl.BlockSpec((1, ts, width), lambda bi, si: (bi, si, 0)),
        out_shape=jax.ShapeDtypeStruct((b, s, width), BF16),
        compiler_params=_params("parallel", "parallel"),
        name=name,
    )(proj_b, pos_f, g.reshape(1, HEAD_DIM), jnp.asarray(freq))


def _diff_attn_kernel(q_ref, k_ref, v_ref, lq1_ref, lk1_ref, lq2_ref, lk2_ref, sg_ref, o_ref,
                      acc_ref, m_ref, l_ref, *, tq, lam_init):
    D = HEAD_DIM
    qi = pl.program_id(2)
    q1 = q_ref[0, :, :D]
    q2 = q_ref[0, :, D:]
    acc_ref[...] = jnp.zeros_like(acc_ref)
    m_ref[...] = jnp.full_like(m_ref, NEG_INF)
    l_ref[...] = jnp.zeros_like(l_ref)

    def lanes(x, width):
        return jnp.concatenate([x] * (width // LANES), axis=1)

    def step(start, mask):
        kblk = k_ref[0, pl.ds(start, tq), :]
        vblk = v_ref[0, pl.ds(start, tq), :]
        s = jnp.concatenate([_dot_nt(q1, kblk[:, :D]), _dot_nt(q2, kblk[:, D:])], axis=0)
        if mask is not None:
            s = jnp.where(mask, s, NEG_INF)
        m_prev = m_ref[...]
        m_next = jnp.maximum(m_prev, jnp.max(s, axis=-1, keepdims=True))
        alpha = jnp.exp(m_prev - m_next)
        p = jnp.exp(s - lanes(m_next, tq))
        l_ref[...] = alpha * l_ref[...] + jnp.sum(p, axis=-1, keepdims=True)
        acc_ref[...] = lanes(alpha, 2 * D) * acc_ref[...] + jnp.dot(p.astype(BF16), vblk,
                                                                    preferred_element_type=F32)
        m_ref[...] = m_next

    def body(j, carry):
        step(pl.multiple_of(j * tq, tq), None)
        return carry

    lax.fori_loop(0, qi, body, 0)
    shift_c = CHUNK.bit_length() - 1
    row = lax.broadcasted_iota(jnp.int32, (2 * tq, tq), 0)
    row = jnp.right_shift(jnp.where(row >= tq, row - tq, row), shift_c)
    col = jnp.right_shift(lax.broadcasted_iota(jnp.int32, (2 * tq, tq), 1), shift_c)
    step(pl.multiple_of(qi * tq, tq), row >= col)

    lam = (jnp.exp(jnp.sum(lq1_ref[...] * lk1_ref[...], axis=-1, keepdims=True))
           - jnp.exp(jnp.sum(lq2_ref[...] * lk2_ref[...], axis=-1, keepdims=True)) + lam_init)
    a = acc_ref[...] / lanes(l_ref[...], 2 * D)
    o = a[:tq] - lam * a[tq:]
    on = o * lax.rsqrt(jnp.mean(o * o, axis=-1, keepdims=True) + NORM_EPS)
    o_ref[0] = (on * sg_ref[...] * (1.0 - lam_init)).astype(o_ref.dtype)


def _diff_attn(q, k, v_src, v_col_block, lq1, lk1, lq2, lk2, subln_g, heads, lam_init):
    b, s, _ = q.shape
    hw = 2 * HEAD_DIM
    tq = _pick_tile(s, ATTN_BLOCK, CHUNK)
    vec = pl.BlockSpec((1, HEAD_DIM), lambda bi, hi, qi: (0, 0))
    return pl.pallas_call(
        functools.partial(_diff_attn_kernel, tq=tq, lam_init=lam_init),
        grid=(b, heads, s // tq),
        in_specs=[pl.BlockSpec((1, tq, hw), lambda bi, hi, qi: (bi, qi, hi)),
                  pl.BlockSpec((1, s, hw), lambda bi, hi, qi: (bi, 0, hi)),
                  pl.BlockSpec((1, s, hw), lambda bi, hi, qi: (bi, 0, v_col_block + hi)),
                  vec, vec, vec, vec,
                  pl.BlockSpec((1, hw), lambda bi, hi, qi: (0, 0))],
        out_specs=pl.BlockSpec((1, tq, hw), lambda bi, hi, qi: (bi, qi, hi)),
        out_shape=jax.ShapeDtypeStruct((b, s, heads * hw), BF16),
        scratch_shapes=[pltpu.VMEM((2 * tq, hw), F32)] + [pltpu.VMEM((2 * tq, LANES), F32)] * 2,
        compiler_params=_params("parallel", "parallel", "arbitrary"),
        name="diff_attention",
    )(q, k, v_src, lq1.reshape(1, -1), lk1.reshape(1, -1), lq2.reshape(1, -1), lk2.reshape(1, -1),
      subln_g.reshape(1, hw))


def _merge_kernel(oa_ref, ob_ref, wa_ref, wb_ref, ga_ref, gb_ref, ba_ref, bb_ref, o_ref, wa_s, wb_s):
    @pl.when(pl.program_id(1) == 0)
    def _():
        wa_s[...] = wa_ref[...].astype(BF16)
        wb_s[...] = wb_ref[...].astype(BF16)

    ya = jnp.dot(oa_ref[...], wa_s[...], preferred_element_type=F32)
    yb = jnp.dot(ob_ref[...], wb_s[...], preferred_element_type=F32)
    merged = _sigmoid(ga_ref[...] + ba_ref[...]) * ya + _sigmoid(gb_ref[...] + bb_ref[...]) * yb
    o_ref[...] = merged.astype(o_ref.dtype)


def _merge(oa, ob, w_a, w_b, proj_b, gate_col0, b_gate, d):
    m, ka = oa.shape
    kb = ob.shape[1]
    tn = _pick_tile(d, 512, LANES)
    tm = _pick_tile(m, 512, 8)
    assert gate_col0 % tn == 0
    ga = gate_col0 // tn
    gb = (gate_col0 + d) // tn
    nb = d // tn
    bg = b_gate.reshape(1, 2 * d)
    return pl.pallas_call(
        _merge_kernel,
        grid=(nb, m // tm),
        in_specs=[pl.BlockSpec((tm, ka), lambda j, i: (i, 0)),
                  pl.BlockSpec((tm, kb), lambda j, i: (i, 0)),
                  pl.BlockSpec((ka, tn), lambda j, i: (0, j)),
                  pl.BlockSpec((kb, tn), lambda j, i: (0, j)),
                  pl.BlockSpec((tm, tn), lambda j, i: (i, ga + j)),
                  pl.BlockSpec((tm, tn), lambda j, i: (i, gb + j)),
                  pl.BlockSpec((1, tn), lambda j, i: (0, j)),
                  pl.BlockSpec((1, tn), lambda j, i: (0, nb + j))],
        out_specs=pl.BlockSpec((tm, tn), lambda j, i: (i, j)),
        out_shape=jax.ShapeDtypeStruct((m, d), BF16),
        scratch_shapes=[pltpu.VMEM((ka, tn), BF16), pltpu.VMEM((kb, tn), BF16)],
        compiler_params=_params("parallel", "arbitrary"),
        name="branch_merge",
    )(oa, ob, w_a, w_b, proj_b, proj_b, bg, bg)


def _router_kernel(x_ref, g_ref, wr_ref, br_ref, h_ref, ti_ref, tw_ref, rk_ref, cnt_ref, carry_ref,
                   *, n_experts):
    @pl.when(pl.program_id(0) == 0)
    def _():
        carry_ref[...] = jnp.zeros_like(carry_ref)

    x = x_ref[...]
    h = x * lax.rsqrt(jnp.mean(x * x, axis=-1, keepdims=True) + NORM_EPS) * g_ref[...]
    h_ref[...] = h
    logits = jnp.dot(h, wr_ref[...], preferred_element_type=F32,
                     precision=lax.Precision.HIGHEST) + br_ref[...]
    tm = x.shape[0]
    lane = lax.broadcasted_iota(jnp.int32, (tm, LANES), 1)
    lane_f = lane.astype(F32)
    cur = jnp.where(lane < n_experts, logits, -jnp.inf)
    vals, hots = [], []
    ti = jnp.zeros((tm, LANES), F32)
    for kk in range(TOP_K):
        mx = jnp.max(cur, axis=-1, keepdims=True)
        idx = jnp.min(jnp.where(cur == mx, lane_f, float(LANES)), axis=-1, keepdims=True)
        hot = lane_f == idx
        vals.append(mx)
        hots.append(hot)
        ti = jnp.where(lane == kk, idx, ti)
        cur = jnp.where(hot, -jnp.inf, cur)
    exps = [jnp.exp(v - vals[0]) for v in vals]
    denom = exps[0] + exps[1] + exps[2] + exps[3]
    tw = jnp.zeros((tm, LANES), F32)
    for kk in range(TOP_K):
        tw = jnp.where(lane == kk, exps[kk] / denom, tw)
    multihot = jnp.zeros((tm, LANES), F32)
    for hot in hots:
        multihot = multihot + hot.astype(F32)
    r = lax.broadcasted_iota(jnp.int32, (tm, tm), 0)
    c = lax.broadcasted_iota(jnp.int32, (tm, tm), 1)
    prefix = jnp.dot((r > c).astype(BF16), multihot.astype(BF16), preferred_element_type=F32)
    prefix = prefix + carry_ref[...]
    rk = jnp.zeros((tm, LANES), F32)
    for kk in range(TOP_K):
        rk_k = jnp.sum(jnp.where(hots[kk], prefix, 0.0), axis=-1, keepdims=True)
        rk = jnp.where(lane == kk, rk_k, rk)
    carry = carry_ref[...] + jnp.sum(multihot, axis=0, keepdims=True)
    carry_ref[...] = carry
    cnt_ref[...] = carry.astype(jnp.int32)
    ti_ref[...] = ti.astype(jnp.int32)
    tw_ref[...] = tw
    rk_ref[...] = rk.astype(jnp.int32)


def _router(x1, g, w_router, b_router):
    n, d = x1.shape
    e = w_router.shape[1]
    tm = _pick_tile(n, 256, 8)
    wr = jnp.zeros((d, LANES), F32).at[:, :e].set(w_router)
    br = jnp.zeros((1, LANES), F32).at[0, :e].set(b_router)
    row = pl.BlockSpec((tm, LANES), lambda i: (i, 0))
    return pl.pallas_call(
        functools.partial(_router_kernel, n_experts=e),
        grid=(n // tm,),
        in_specs=[pl.BlockSpec((tm, d), lambda i: (i, 0)),
                  pl.BlockSpec((1, d), lambda i: (0, 0)),
                  pl.BlockSpec((d, LANES), lambda i: (0, 0)),
                  pl.BlockSpec((1, LANES), lambda i: (0, 0))],
        out_specs=[pl.BlockSpec((tm, d), lambda i: (i, 0)), row, row, row,
                   pl.BlockSpec((1, LANES), lambda i: (0, 0))],
        out_shape=[jax.ShapeDtypeStruct((n, d), F32),
                   jax.ShapeDtypeStruct((n, LANES), jnp.int32),
                   jax.ShapeDtypeStruct((n, LANES), F32),
                   jax.ShapeDtypeStruct((n, LANES), jnp.int32),
                   jax.ShapeDtypeStruct((1, LANES), jnp.int32)],
        scratch_shapes=[pltpu.VMEM((1, LANES), F32)],
        compiler_params=_params("arbitrary"),
        name="router_topk",
    )(x1, g.reshape(1, d), wr, br)


def _row_copy(src_ref, src_row, dst_ref, dst_row, sem):
    return pltpu.make_async_copy(src_ref.at[pl.ds(src_row, 1)], dst_ref.at[pl.ds(dst_row, 1)], sem)


def _expert_kernel(be_ref, tok_ref, tokn_ref, h_ref, wg_ref, wu_ref, wd_ref, bg_ref, bu_ref, bd_ref,
                   o_ref, xg_ref, xb_ref, sem, *, rows, nblk):
    b = pl.program_id(0)
    f = pl.program_id(1)
    slot = b % 2

    def gather(t_ref, s):
        def body(r, carry):
            _row_copy(h_ref, t_ref[0, 0, r], xg_ref.at[s], r, sem.at[s]).start()
            return carry
        lax.fori_loop(0, rows, body, 0, unroll=8)

    @pl.when((f == 0) & (b == 0))
    def _():
        gather(tok_ref, 0)

    @pl.when((f == 0) & (b + 1 < nblk))
    def _():
        gather(tokn_ref, 1 - slot)

    @pl.when(f == 0)
    def _():
        pltpu.make_async_copy(h_ref.at[pl.ds(0, rows)], xg_ref.at[slot], sem.at[slot]).wait()
        xb_ref[...] = xg_ref[slot].astype(BF16)
        o_ref[...] = jnp.broadcast_to(bd_ref[0], o_ref.shape)

    xb = xb_ref[...]
    g = jnp.dot(xb, wg_ref[0].astype(BF16), preferred_element_type=F32) + bg_ref[0]
    u = jnp.dot(xb, wu_ref[0].astype(BF16), preferred_element_type=F32) + bu_ref[0]
    gate = jnp.minimum(g, SWIGLU_LIMIT)
    up = jnp.clip(u, -SWIGLU_LIMIT, SWIGLU_LIMIT)
    act = (up + 1.0) * gate * _sigmoid(SWIGLU_ALPHA * gate)
    o_ref[...] += jnp.dot(act.astype(BF16), wd_ref[0].astype(BF16), preferred_element_type=F32)


def _experts(h, slot_token, block_expert, w_gate_up, b_gate_up, w_down, b_down, rows):
    n, d = h.shape
    p = slot_token.shape[0]
    e, _, dff2 = w_gate_up.shape
    dff = dff2 // 2
    tf = _pick_tile(dff, 512, LANES)
    nf = dff // tf
    nblk = p // rows
    tok3 = slot_token.reshape(nblk, 1, rows)
    grid_spec = pltpu.PrefetchScalarGridSpec(
        num_scalar_prefetch=1,
        grid=(nblk, nf),
        in_specs=[pl.BlockSpec((1, 1, rows), lambda b, f, be: (b, 0, 0), memory_space=pltpu.SMEM),
                  pl.BlockSpec((1, 1, rows), lambda b, f, be: (jnp.minimum(b + 1, nblk - 1), 0, 0),
                               memory_space=pltpu.SMEM),
                  pl.BlockSpec(memory_space=pl.ANY),
                  pl.BlockSpec((1, d, tf), lambda b, f, be: (be[b], 0, f)),
                  pl.BlockSpec((1, d, tf), lambda b, f, be: (be[b], 0, nf + f)),
                  pl.BlockSpec((1, tf, d), lambda b, f, be: (be[b], f, 0)),
                  pl.BlockSpec((1, 1, tf), lambda b, f, be: (be[b], 0, f)),
                  pl.BlockSpec((1, 1, tf), lambda b, f, be: (be[b], 0, nf + f)),
                  pl.BlockSpec((1, 1, d), lambda b, f, be: (be[b], 0, 0))],
        out_specs=pl.BlockSpec((rows, d), lambda b, f, be: (b, 0)),
        scratch_shapes=[pltpu.VMEM((2, rows, d), F32), pltpu.VMEM((rows, d), BF16),
                        pltpu.SemaphoreType.DMA((2,))],
    )
    return pl.pallas_call(
        functools.partial(_expert_kernel, rows=rows, nblk=nblk),
        grid_spec=grid_spec,
        out_shape=jax.ShapeDtypeStruct((p, d), F32),
        compiler_params=_params("arbitrary", "arbitrary"),
        name="expert_mlp",
    )(block_expert, tok3, tok3, h, w_gate_up, w_gate_up, w_down, b_gate_up.reshape(e, 1, dff2),
      b_gate_up.reshape(e, 1, dff2), b_down.reshape(e, 1, d))


def _combine_kernel(dest_ref, destn_ref, x_ref, tw_ref, ys_ref, o_ref, buf_ref, sem, *, tm, nt):
    i = pl.program_id(0)
    slot = i % 2

    def gather(d_ref, s):
        def body(t, carry):
            for kk in range(TOP_K):
                _row_copy(ys_ref, d_ref[0, 0, t * TOP_K + kk], buf_ref.at[s, kk], t, sem.at[s]).start()
            return carry
        lax.fori_loop(0, tm, body, 0, unroll=2)

    @pl.when(i == 0)
    def _():
        gather(dest_ref, 0)

    @pl.when(i + 1 < nt)
    def _():
        gather(destn_ref, 1 - slot)

    for kk in range(TOP_K):
        pltpu.make_async_copy(ys_ref.at[pl.ds(0, tm)], buf_ref.at[slot, kk], sem.at[slot]).wait()
    tw = tw_ref[...]
    y = buf_ref[slot, 0] * tw[:, 0:1]
    for kk in range(1, TOP_K):
        y = y + buf_ref[slot, kk] * tw[:, kk:kk + 1]
    o_ref[...] = x_ref[...] + y


def _combine(x1, topw, dest, ys):
    n, d = x1.shape
    tm = _pick_tile(n, 128, 8)
    nt = n // tm
    dest3 = dest.reshape(nt, 1, tm * TOP_K)
    return pl.pallas_call(
        functools.partial(_combine_kernel, tm=tm, nt=nt),
        grid=(nt,),
        in_specs=[pl.BlockSpec((1, 1, tm * TOP_K), lambda i: (i, 0, 0), memory_space=pltpu.SMEM),
                  pl.BlockSpec((1, 1, tm * TOP_K), lambda i: (jnp.minimum(i + 1, nt - 1), 0, 0),
                               memory_space=pltpu.SMEM),
                  pl.BlockSpec((tm, d), lambda i: (i, 0)),
                  pl.BlockSpec((tm, LANES), lambda i: (i, 0)),
                  pl.BlockSpec(memory_space=pl.ANY)],
        out_specs=pl.BlockSpec((tm, d), lambda i: (i, 0)),
        out_shape=jax.ShapeDtypeStruct((n, d), F32),
        scratch_shapes=[pltpu.VMEM((2, TOP_K, tm, d), F32), pltpu.SemaphoreType.DMA((2,))],
        compiler_params=_params("arbitrary"),
        name="moe_combine",
    )(dest3, dest3, x1, topw, ys)


ATTN_BLOCK = 512
GDN_HEADS_PER_STEP = 8
GDN_GROUP = 4
EXPERT_ROWS = 512


def _layer(x, positions, layer_idx, norm1_g, w_in, b_gate, conv_w, a_log, dt_bias, gdn_norm_g,
           q_norm_g, k_norm_g, lambda_q1, lambda_k1, lambda_q2, lambda_k2, subln_g,
           w_branch_a, w_branch_b, w_out, norm2_g, w_router, b_router,
           w_gate_up, b_gate_up, w_down, b_down):
    b, s, d = x.shape
    n = b * s
    gdn_heads = a_log.shape[0]
    gdn_dim = gdn_heads * HEAD_DIM
    diff_v = w_branch_b.shape[0]
    diff_heads = diff_v // (2 * HEAD_DIM)
    diff_qk = 2 * diff_heads * HEAD_DIM
    off_beta = 4 * gdn_dim
    off_alpha = off_beta + gdn_heads
    off_qb = off_alpha + gdn_heads
    off_gate = off_qb + 2 * diff_qk + diff_v
    assert w_in.shape[1] == off_gate + 2 * d
    n_experts = w_router.shape[1]

    x2 = x.reshape(n, d)
    h = _rmsnorm(x2, norm1_g, BF16)

    proj_a = _matmul(h, w_in, 0, off_beta, F32, name="proj_gdn")
    hg = min(GDN_HEADS_PER_STEP, gdn_heads)
    ngroups = gdn_heads // hg
    cols = []
    for gi in range(ngroups):
        cols += [off_beta + gi * hg + i for i in range(hg)] + [off_alpha + gi * hg + i for i in range(hg)]
        cols += [off_beta] * (LANES - 2 * hg)
    lane_valid = np.tile(np.arange(LANES) < 2 * hg, ngroups)
    w_ba = jnp.where(lane_valid[None, :], w_in[:, np.asarray(cols)], 0.0)
    ba = _matmul(h, w_ba, 0, ngroups * LANES, F32, name="proj_beta_alpha")
    w_b = w_in[:, off_qb:]
    proj_b = _matmul(h, w_b, 0, w_b.shape[1], F32, name="proj_attn_gate")

    alog_p = jnp.zeros((ngroups, 1, LANES), F32).at[:, 0, hg:2 * hg].set(a_log.reshape(ngroups, hg))
    dtb_p = jnp.zeros((ngroups, 1, LANES), F32).at[:, 0, hg:2 * hg].set(dt_bias.reshape(ngroups, hg))
    oa = _gdn(proj_a.reshape(b, s, off_beta), ba.reshape(b, s, ngroups * LANES), conv_w,
              alog_p, dtb_p, gdn_norm_g, gdn_heads, hg)

    pb3 = proj_b.reshape(b, s, proj_b.shape[1])
    pos_f = positions.astype(F32).reshape(b, s, 1)
    qr = _qknorm_rope(pb3, 0, diff_qk, pos_f, q_norm_g, HEAD_DIM ** -0.5, "q_norm_rope")
    kr = _qknorm_rope(pb3, 1, diff_qk, pos_f, k_norm_g, 1.0, "k_norm_rope")
    vb = pb3[:, :, 2 * diff_qk:2 * diff_qk + diff_v].astype(BF16)
    lam_init = 0.8 - 0.6 * math.exp(-0.3 * layer_idx)
    ob = _diff_attn(qr, kr, vb, 0, lambda_q1, lambda_k1, lambda_q2, lambda_k2, subln_g,
                    diff_heads, lam_init)

    merged = _merge(oa.reshape(n, gdn_dim), ob.reshape(n, diff_v), w_branch_a, w_branch_b,
                    proj_b, 2 * diff_qk + diff_v, b_gate, d)
    x1 = _matmul(merged, w_out, 0, d, F32, res=x2, name="out_proj")

    h2, topi, topw, rank, counts = _router(x1, norm2_g, w_router, b_router)
    rows = EXPERT_ROWS
    top_e = topi[:, :TOP_K]
    cnt = counts[0, :n_experts]
    padded = (cnt + rows - 1) // rows * rows
    pad_end = jnp.cumsum(padded)
    pad_start = pad_end - padded
    dest = (pad_start[top_e] + rank[:, :TOP_K]).reshape(n * TOP_K)
    p_rows = n * TOP_K + n_experts * rows
    nblk = p_rows // rows
    slot_token = jnp.zeros((p_rows,), jnp.int32).at[dest].set(
        jnp.repeat(jnp.arange(n, dtype=jnp.int32), TOP_K))
    block_expert = jnp.minimum(
        jnp.searchsorted(pad_end, jnp.arange(nblk, dtype=jnp.int32) * rows, side="right"),
        n_experts - 1).astype(jnp.int32)
    ys = _experts(h2, slot_token, block_expert, w_gate_up, b_gate_up, w_down, b_down, rows)
    out = _combine(x1, topw, dest, ys)
    return out.reshape(b, s, d)


def kernel(x, positions, norm1_g, w_in, b_gate, conv_w, a_log, dt_bias, gdn_norm_g, q_norm_g, k_norm_g, lambda_q1, lambda_k1, lambda_q2, lambda_k2, subln_g, w_branch_a, w_branch_b, w_out, norm2_g, w_router, b_router, w_gate_up, b_gate_up, w_down, b_down):
    depth = norm1_g.shape[0]
    for l in range(depth):
        x = _layer(x, positions, l, norm1_g[l], w_in[l], b_gate[l], conv_w[l], a_log[l], dt_bias[l],
                   gdn_norm_g[l], q_norm_g[l], k_norm_g[l], lambda_q1[l], lambda_k1[l], lambda_q2[l],
                   lambda_k2[l], subln_g[l], w_branch_a[l], w_branch_b[l], w_out[l], norm2_g[l],
                   w_router[l], b_router[l], w_gate_up[l], b_gate_up[l], w_down[l], b_down[l])
    return x
```

```python
import functools
import math

import numpy as np
import jax
import jax.numpy as jnp
from jax import lax
from jax.experimental import pallas as pl
from jax.experimental.pallas import tpu as pltpu

F32 = jnp.float32
BF16 = jnp.bfloat16

NORM_EPS = 1e-6
NEG_INF = -1e30
CHUNK = 64
HEAD_DIM = 128
ROT_DIM = HEAD_DIM // 4
ROPE_THETA = 500000.0
TOP_K = 4
SWIGLU_ALPHA = 1.702
SWIGLU_LIMIT = 7.0
LANES = 128
VMEM_LIMIT_BYTES = 60 * 1024 * 1024


def _pick_tile(n, target, quantum):
    if n <= target:
        return n
    t = (target // quantum) * quantum
    while t > quantum and n % t:
        t -= quantum
    assert n % t == 0, (n, target, quantum)
    return t


def _params(*sem):
    return pltpu.CompilerParams(dimension_semantics=sem, vmem_limit_bytes=VMEM_LIMIT_BYTES)


def _sigmoid(x):
    return 1.0 / (1.0 + jnp.exp(-x))


def _softplus(x):
    return jnp.maximum(x, 0.0) + jnp.log(1.0 + jnp.exp(-jnp.abs(x)))


def _dot(a, b):
    return jnp.dot(a.astype(BF16), b.astype(BF16), preferred_element_type=F32)


def _dot_nt(a, b):
    return lax.dot_general(a.astype(BF16), b.astype(BF16), (((1,), (1,)), ((), ())),
                           preferred_element_type=F32)


def _dot_tn(a, b):
    return lax.dot_general(a.astype(BF16), b.astype(BF16), (((0,), (0,)), ((), ())),
                           preferred_element_type=F32)


def _rmsnorm_kernel(x_ref, g_ref, o_ref):
    x = x_ref[...]
    y = x * lax.rsqrt(jnp.mean(x * x, axis=-1, keepdims=True) + NORM_EPS)
    o_ref[...] = (y * g_ref[...]).astype(o_ref.dtype)


def _rmsnorm(x2d, g, out_dtype):
    n, d = x2d.shape
    tm = _pick_tile(n, 512, 8)
    return pl.pallas_call(
        _rmsnorm_kernel,
        grid=(n // tm,),
        in_specs=[pl.BlockSpec((tm, d), lambda i: (i, 0)), pl.BlockSpec((1, d), lambda i: (0, 0))],
        out_specs=pl.BlockSpec((tm, d), lambda i: (i, 0)),
        out_shape=jax.ShapeDtypeStruct((n, d), out_dtype),
        compiler_params=_params("parallel"),
        name="rmsnorm",
    )(x2d, g.reshape(1, d))


def _matmul_kernel(*refs, has_res):
    if has_res:
        x_ref, w_ref, r_ref, o_ref, wb_ref = refs
    else:
        x_ref, w_ref, o_ref, wb_ref = refs

    @pl.when(pl.program_id(1) == 0)
    def _():
        wb_ref[...] = w_ref[...].astype(BF16)

    acc = jnp.dot(x_ref[...], wb_ref[...], preferred_element_type=F32)
    if has_res:
        acc = acc + r_ref[...]
    o_ref[...] = acc.astype(o_ref.dtype)


def _matmul(x, w, col0, ncols, out_dtype, res=None, name="matmul"):
    m, k = x.shape
    tn = _pick_tile(ncols, 1024, LANES)
    tm = _pick_tile(m, 512, 8)
    assert col0 % tn == 0
    cb = col0 // tn
    in_specs = [pl.BlockSpec((tm, k), lambda j, i: (i, 0)),
                pl.BlockSpec((k, tn), lambda j, i: (0, cb + j))]
    args = [x, w]
    if res is not None:
        in_specs.append(pl.BlockSpec((tm, tn), lambda j, i: (i, j)))
        args.append(res)
    return pl.pallas_call(
        functools.partial(_matmul_kernel, has_res=res is not None),
        grid=(ncols // tn, m // tm),
        in_specs=in_specs,
        out_specs=pl.BlockSpec((tm, tn), lambda j, i: (i, j)),
        out_shape=jax.ShapeDtypeStruct((m, ncols), out_dtype),
        scratch_shapes=[pltpu.VMEM((k, tn), BF16)],
        compiler_params=_params("parallel", "arbitrary"),
        name=name,
    )(*args)


def _gdn_kernel(q_ref, k_ref, v_ref, z_ref, ba_ref, cwq_ref, cwk_ref, cwv_ref, alog_ref, dtb_ref,
                ng_ref, o_ref, state_ref, tq_ref, tk_ref, tv_ref, eq_ref, ek_ref, ev_ref, *, hg):
    C = CHUNK
    D = HEAD_DIM

    @pl.when(pl.program_id(2) == 0)
    def _():
        state_ref[...] = jnp.zeros_like(state_ref)
        tq_ref[...] = jnp.zeros_like(tq_ref)
        tk_ref[...] = jnp.zeros_like(tk_ref)
        tv_ref[...] = jnp.zeros_like(tv_ref)

    def conv_silu(u_ref, tail_ref, ext_ref, w_ref):
        u = u_ref[0]
        ext_ref[0:8, :] = tail_ref[...]
        ext_ref[8:8 + C, :] = u
        w = w_ref[...]
        y = ext_ref[5:5 + C, :] * w[0:1, :]
        y = y + ext_ref[6:6 + C, :] * w[1:2, :]
        y = y + ext_ref[7:7 + C, :] * w[2:3, :]
        y = y + u * w[3:4, :]
        tail_ref[...] = u[C - 8:C, :]
        return y * _sigmoid(y)

    def l2norm_heads(x, scale):
        parts = []
        for i in range(hg):
            xi = x[:, i * D:(i + 1) * D]
            parts.append(xi * (lax.rsqrt(jnp.sum(xi * xi, axis=-1, keepdims=True) + NORM_EPS) * scale))
        return jnp.concatenate(parts, axis=1)

    qc = l2norm_heads(conv_silu(q_ref, tq_ref, eq_ref, cwq_ref), D ** -0.5)
    kc = l2norm_heads(conv_silu(k_ref, tk_ref, ek_ref, cwk_ref), 1.0)
    vc = conv_silu(v_ref, tv_ref, ev_ref, cwv_ref)

    G = GDN_GROUP
    R = G * C
    shift_c = C.bit_length() - 1

    ba = ba_ref[0]
    beta_full = _sigmoid(ba)
    g_full = -jnp.exp(alog_ref[0]) * _softplus(ba + dtb_ref[0])
    r64 = lax.broadcasted_iota(jnp.int32, (C, C), 0)
    c64 = lax.broadcasted_iota(jnp.int32, (C, C), 1)
    gcum_full = jnp.dot((r64 >= c64).astype(F32), g_full, preferred_element_type=F32,
                        precision=lax.Precision.HIGHEST)
    gcum_t = gcum_full.T
    row = lax.broadcasted_iota(jnp.int32, (R, R), 0)
    col = lax.broadcasted_iota(jnp.int32, (R, R), 1)
    same_head = jnp.right_shift(row, shift_c) == jnp.right_shift(col, shift_c)
    tril = same_head & (row >= col)
    strict = same_head & (row > col)
    eye = (row == col).astype(F32)
    lane2 = lax.broadcasted_iota(jnp.int32, (1, 2 * D), 1)
    r2 = lax.broadcasted_iota(jnp.int32, (2 * D, 2 * D), 0)
    c2 = lax.broadcasted_iota(jnp.int32, (2 * D, 2 * D), 1)
    pair_block = (r2 < D) == (c2 < D)

    def lanes_of(x, i0):
        return jnp.concatenate([x[i0 * C:(i0 + 1) * C], x[(i0 + 1) * C:(i0 + 2) * C]], axis=1)

    for h0 in range(0, hg, G):
        heads = range(h0, h0 + G)

        def rows_of(x):
            return jnp.concatenate([x[:, i * D:(i + 1) * D] for i in heads], axis=0)

        beta_r = jnp.concatenate([beta_full[:, i:i + 1] for i in heads], axis=0)
        gc_r = jnp.concatenate([gcum_full[:, hg + i:hg + i + 1] for i in heads], axis=0)
        gc_l = jnp.concatenate([gcum_t[hg + i:hg + i + 1, :] for i in heads], axis=1)
        g_last = [gcum_full[C - 1:C, hg + i:hg + i + 1] for i in heads]
        gl_r = jnp.concatenate([jnp.broadcast_to(g, (C, 1)) for g in g_last], axis=0)

        qn, kn, v_r = rows_of(qc), rows_of(kc), rows_of(vc)
        decay = jnp.where(tril, jnp.exp(jnp.where(tril, gc_r - gc_l, 0.0)), 0.0)
        eg = jnp.exp(gc_r)
        kb = kn * beta_r
        gram = _dot_nt(jnp.concatenate([kb, qn], axis=0), kn)
        nmat = jnp.where(strict, gram[:R] * decay, 0.0)
        qk = gram[R:] * decay
        p = eye - nmat
        mpow = _dot(nmat, nmat)
        for _ in range(4):
            mb = mpow.astype(BF16)
            x = jnp.dot(jnp.concatenate([p.astype(BF16), mb], axis=0), mb, preferred_element_type=F32)
            p = p + x[:R]
            mpow = x[R:]
        tmat = p + _dot(p, mpow)

        kbg_r = kb * eg
        qg_r = qn * eg
        kdec_r = kn * jnp.exp(gl_r - gc_r)
        states, ks_rows, o1_rows = [], [], []
        for pr in range(G // 2):
            s = state_ref[h0 // 2 + pr]
            xs = _dot(jnp.concatenate([lanes_of(kbg_r, 2 * pr), lanes_of(qg_r, 2 * pr)], axis=0), s)
            states.append(s)
            ks_rows += [xs[:C, :D], xs[:C, D:]]
            o1_rows += [xs[C:, :D], xs[C:, D:]]
        vnew_r = _dot(tmat, v_r * beta_r - jnp.concatenate(ks_rows, axis=0))
        for pr in range(G // 2):
            upd = _dot_tn(lanes_of(kdec_r, 2 * pr), lanes_of(vnew_r, 2 * pr))
            gl_lanes = jnp.where(lane2 < D, jnp.exp(g_last[2 * pr]), jnp.exp(g_last[2 * pr + 1]))
            state_ref[h0 // 2 + pr] = states[pr] * gl_lanes + jnp.where(pair_block, upd, 0.0)
        o = jnp.concatenate(o1_rows, axis=0) + _dot(qk, vnew_r)
        on = o * lax.rsqrt(jnp.mean(o * o, axis=-1, keepdims=True) + NORM_EPS) * ng_ref[...]
        z_r = rows_of(z_ref[0])
        out = (on * (z_r * _sigmoid(z_r))).astype(o_ref.dtype)
        for j, i in enumerate(heads):
            o_ref[0, :, i * D:(i + 1) * D] = out[j * C:(j + 1) * C]


def _gdn(proj_a, ba, conv_w, alog_p, dtb_p, norm_g, heads, hg):
    b, s, _ = proj_a.shape
    assert hg % 2 == 0 and heads % hg == 0
    ngroups = heads // hg
    w = hg * HEAD_DIM
    nc = s // CHUNK

    def colspec(off):
        return pl.BlockSpec((1, CHUNK, w), lambda bi, gi, ci, off=off: (bi, ci, off * ngroups + gi))

    def cwspec(off):
        return pl.BlockSpec((4, w), lambda bi, gi, ci, off=off: (0, off * ngroups + gi))

    vec = pl.BlockSpec((1, 1, LANES), lambda bi, gi, ci: (gi, 0, 0))
    return pl.pallas_call(
        functools.partial(_gdn_kernel, hg=hg),
        grid=(b, ngroups, nc),
        in_specs=[colspec(0), colspec(1), colspec(2), colspec(3),
                  pl.BlockSpec((1, CHUNK, LANES), lambda bi, gi, ci: (bi, ci, gi)),
                  cwspec(0), cwspec(1), cwspec(2), vec, vec,
                  pl.BlockSpec((1, HEAD_DIM), lambda bi, gi, ci: (0, 0))],
        out_specs=pl.BlockSpec((1, CHUNK, w), lambda bi, gi, ci: (bi, ci, gi)),
        out_shape=jax.ShapeDtypeStruct((b, s, heads * HEAD_DIM), BF16),
        scratch_shapes=[pltpu.VMEM((hg // 2, 2 * HEAD_DIM, 2 * HEAD_DIM), F32)]
        + [pltpu.VMEM((8, w), F32)] * 3 + [pltpu.VMEM((8 + CHUNK, w), F32)] * 3,
        compiler_params=_params("parallel", "parallel", "arbitrary"),
        name="gated_delta_rule",
    )(proj_a, proj_a, proj_a, proj_a, ba, conv_w, conv_w, conv_w, alog_p, dtb_p,
      norm_g.reshape(1, HEAD_DIM))


def _qknorm_rope_kernel(x_ref, pos_ref, g_ref, freq_ref, o_ref, *, scale):
    pos = pos_ref[0]
    ang = pos * freq_ref[...]
    cos = jnp.cos(ang)
    sin = jnp.sin(ang)
    lane = lax.broadcasted_iota(jnp.int32, ang.shape, 1)
    first_half = lane < ROT_DIM // 2
    sin_signed = jnp.where(first_half, -sin, sin)
    g = g_ref[...]
    ngroups = x_ref.shape[-1] // HEAD_DIM
    for j in range(ngroups):
        sl = slice(j * HEAD_DIM, (j + 1) * HEAD_DIM)
        x = x_ref[0, :, sl]
        xn = x * lax.rsqrt(jnp.mean(x * x, axis=-1, keepdims=True) + NORM_EPS) * g
        partner = jnp.where(first_half, pltpu.roll(xn, HEAD_DIM - ROT_DIM // 2, 1),
                            pltpu.roll(xn, ROT_DIM // 2, 1))
        o_ref[0, :, sl] = ((xn * cos + partner * sin_signed) * scale).astype(o_ref.dtype)


def _qknorm_rope(proj_b, col_block, width, pos_f, g, scale, name):
    b, s, _ = proj_b.shape
    ts = _pick_tile(s, 256, 8)
    half = ROT_DIM // 2
    inv_freq = np.power(np.float32(ROPE_THETA),
                        -np.arange(half, dtype=np.float32) * np.float32(2.0) / np.float32(ROT_DIM))
    freq = np.zeros((1, LANES), np.float32)
    freq[0, :half] = inv_freq
    freq[0, half:ROT_DIM] = inv_freq
    return pl.pallas_call(
        functools.partial(_qknorm_rope_kernel, scale=scale),
        grid=(b, s // ts),
        in_specs=[pl.BlockSpec((1, ts, width), lambda bi, si: (bi, si, col_block)),
                  pl.BlockSpec((1, ts, 1), lambda bi, si: (bi, si, 0)),
                  pl.BlockSpec((1, HEAD_DIM), lambda bi, si: (0, 0)),
                  pl.BlockSpec((1, LANES), lambda bi, si: (0, 0))],
        out_specs=pl.BlockSpec((1, ts, width), lambda bi, si: (bi, si, 0)),
        out_shape=jax.ShapeDtypeStruct((b, s, width), BF16),
        compiler_params=_params("parallel", "parallel"),
        name=name,
    )(proj_b, pos_f, g.reshape(1, HEAD_DIM), jnp.asarray(freq))


def _diff_attn_kernel(q_ref, k_ref, v_ref, lq1_ref, lk1_ref, lq2_ref, lk2_ref, sg_ref, o_ref,
                      acc_ref, m_ref, l_ref, *, tq, lam_init):
    D = HEAD_DIM
    qi = pl.program_id(2)
    q1 = q_ref[0, :, :D]
    q2 = q_ref[0, :, D:]
    acc_ref[...] = jnp.zeros_like(acc_ref)
    m_ref[...] = jnp.full_like(m_ref, NEG_INF)
    l_ref[...] = jnp.zeros_like(l_ref)

    def lanes(x, width):
        return jnp.concatenate([x] * (width // LANES), axis=1)

    def step(start, mask):
        kblk = k_ref[0, pl.ds(start, tq), :]
        vblk = v_ref[0, pl.ds(start, tq), :].astype(BF16)
        s = jnp.concatenate([_dot_nt(q1, kblk[:, :D]), _dot_nt(q2, kblk[:, D:])], axis=0)
        if mask is not None:
            s = jnp.where(mask, s, NEG_INF)
        m_prev = m_ref[...]
        m_next = jnp.maximum(m_prev, jnp.max(s, axis=-1, keepdims=True))
        alpha = jnp.exp(m_prev - m_next)
        p = jnp.exp(s - lanes(m_next, tq))
        l_ref[...] = alpha * l_ref[...] + jnp.sum(p, axis=-1, keepdims=True)
        acc_ref[...] = lanes(alpha, 2 * D) * acc_ref[...] + jnp.dot(p.astype(BF16), vblk,
                                                                    preferred_element_type=F32)
        m_ref[...] = m_next

    def body(j, carry):
        step(pl.multiple_of(j * tq, tq), None)
        return carry

    lax.fori_loop(0, qi, body, 0)
    shift_c = CHUNK.bit_length() - 1
    row = lax.broadcasted_iota(jnp.int32, (2 * tq, tq), 0)
    row = jnp.right_shift(jnp.where(row >= tq, row - tq, row), shift_c)
    col = jnp.right_shift(lax.broadcasted_iota(jnp.int32, (2 * tq, tq), 1), shift_c)
    step(pl.multiple_of(qi * tq, tq), row >= col)

    lam = (jnp.exp(jnp.sum(lq1_ref[...] * lk1_ref[...], axis=-1, keepdims=True))
           - jnp.exp(jnp.sum(lq2_ref[...] * lk2_ref[...], axis=-1, keepdims=True)) + lam_init)
    a = acc_ref[...] / lanes(l_ref[...], 2 * D)
    o = a[:tq] - lam * a[tq:]
    on = o * lax.rsqrt(jnp.mean(o * o, axis=-1, keepdims=True) + NORM_EPS)
    o_ref[0] = (on * sg_ref[...] * (1.0 - lam_init)).astype(o_ref.dtype)


def _diff_attn(q, k, v_src, v_col_block, lq1, lk1, lq2, lk2, subln_g, heads, lam_init):
    b, s, _ = q.shape
    hw = 2 * HEAD_DIM
    tq = _pick_tile(s, ATTN_BLOCK, CHUNK)
    vec = pl.BlockSpec((1, HEAD_DIM), lambda bi, hi, qi: (0, 0))
    return pl.pallas_call(
        functools.partial(_diff_attn_kernel, tq=tq, lam_init=lam_init),
        grid=(b, heads, s // tq),
        in_specs=[pl.BlockSpec((1, tq, hw), lambda bi, hi, qi: (bi, qi, hi)),
                  pl.BlockSpec((1, s, hw), lambda bi, hi, qi: (bi, 0, hi)),
                  pl.BlockSpec((1, s, hw), lambda bi, hi, qi: (bi, 0, v_col_block + hi)),
                  vec, vec, vec, vec,
                  pl.BlockSpec((1, hw), lambda bi, hi, qi: (0, 0))],
        out_specs=pl.BlockSpec((1, tq, hw), lambda bi, hi, qi: (bi, qi, hi)),
        out_shape=jax.ShapeDtypeStruct((b, s, heads * hw), BF16),
        scratch_shapes=[pltpu.VMEM((2 * tq, hw), F32)] + [pltpu.VMEM((2 * tq, LANES), F32)] * 2,
        compiler_params=_params("parallel", "parallel", "arbitrary"),
        name="diff_attention",
    )(q, k, v_src, lq1.reshape(1, -1), lk1.reshape(1, -1), lq2.reshape(1, -1), lk2.reshape(1, -1),
      subln_g.reshape(1, hw))


def _merge_kernel(oa_ref, ob_ref, wa_ref, wb_ref, ga_ref, gb_ref, ba_ref, bb_ref, o_ref, wa_s, wb_s):
    @pl.when(pl.program_id(1) == 0)
    def _():
        wa_s[...] = wa_ref[...].astype(BF16)
        wb_s[...] = wb_ref[...].astype(BF16)

    ya = jnp.dot(oa_ref[...], wa_s[...], preferred_element_type=F32)
    yb = jnp.dot(ob_ref[...], wb_s[...], preferred_element_type=F32)
    merged = _sigmoid(ga_ref[...] + ba_ref[...]) * ya + _sigmoid(gb_ref[...] + bb_ref[...]) * yb
    o_ref[...] = merged.astype(o_ref.dtype)


def _merge(oa, ob, w_a, w_b, proj_b, gate_col0, b_gate, d):
    m, ka = oa.shape
    kb = ob.shape[1]
    tn = _pick_tile(d, 512, LANES)
    tm = _pick_tile(m, 512, 8)
    assert gate_col0 % tn == 0
    ga = gate_col0 // tn
    gb = (gate_col0 + d) // tn
    nb = d // tn
    bg = b_gate.reshape(1, 2 * d)
    return pl.pallas_call(
        _merge_kernel,
        grid=(nb, m // tm),
        in_specs=[pl.BlockSpec((tm, ka), lambda j, i: (i, 0)),
                  pl.BlockSpec((tm, kb), lambda j, i: (i, 0)),
                  pl.BlockSpec((ka, tn), lambda j, i: (0, j)),
                  pl.BlockSpec((kb, tn), lambda j, i: (0, j)),
                  pl.BlockSpec((tm, tn), lambda j, i: (i, ga + j)),
                  pl.BlockSpec((tm, tn), lambda j, i: (i, gb + j)),
                  pl.BlockSpec((1, tn), lambda j, i: (0, j)),
                  pl.BlockSpec((1, tn), lambda j, i: (0, nb + j))],
        out_specs=pl.BlockSpec((tm, tn), lambda j, i: (i, j)),
        out_shape=jax.ShapeDtypeStruct((m, d), BF16),
        scratch_shapes=[pltpu.VMEM((ka, tn), BF16), pltpu.VMEM((kb, tn), BF16)],
        compiler_params=_params("parallel", "arbitrary"),
        name="branch_merge",
    )(oa, ob, w_a, w_b, proj_b, proj_b, bg, bg)


def _router_kernel(x_ref, g_ref, wr_ref, br_ref, h_ref, ti_ref, tw_ref, rk_ref, cnt_ref, carry_ref,
                   *, n_experts):
    @pl.when(pl.program_id(0) == 0)
    def _():
        carry_ref[...] = jnp.zeros_like(carry_ref)

    x = x_ref[...]
    h = x * lax.rsqrt(jnp.mean(x * x, axis=-1, keepdims=True) + NORM_EPS) * g_ref[...]
    h_ref[...] = h
    logits = jnp.dot(h, wr_ref[...], preferred_element_type=F32,
                     precision=lax.Precision.HIGHEST) + br_ref[...]
    tm = x.shape[0]
    lane = lax.broadcasted_iota(jnp.int32, (tm, LANES), 1)
    lane_f = lane.astype(F32)
    cur = jnp.where(lane < n_experts, logits, -jnp.inf)
    vals, hots = [], []
    ti = jnp.zeros((tm, LANES), F32)
    for kk in range(TOP_K):
        mx = jnp.max(cur, axis=-1, keepdims=True)
        idx = jnp.min(jnp.where(cur == mx, lane_f, float(LANES)), axis=-1, keepdims=True)
        hot = lane_f == idx
        vals.append(mx)
        hots.append(hot)
        ti = jnp.where(lane == kk, idx, ti)
        cur = jnp.where(hot, -jnp.inf, cur)
    exps = [jnp.exp(v - vals[0]) for v in vals]
    denom = exps[0] + exps[1] + exps[2] + exps[3]
    tw = jnp.zeros((tm, LANES), F32)
    for kk in range(TOP_K):
        tw = jnp.where(lane == kk, exps[kk] / denom, tw)
    multihot = jnp.zeros((tm, LANES), F32)
    for hot in hots:
        multihot = multihot + hot.astype(F32)
    r = lax.broadcasted_iota(jnp.int32, (tm, tm), 0)
    c = lax.broadcasted_iota(jnp.int32, (tm, tm), 1)
    prefix = jnp.dot((r > c).astype(BF16), multihot.astype(BF16), preferred_element_type=F32)
    prefix = prefix + carry_ref[...]
    rk = jnp.zeros((tm, LANES), F32)
    for kk in range(TOP_K):
        rk_k = jnp.sum(jnp.where(hots[kk], prefix, 0.0), axis=-1, keepdims=True)
        rk = jnp.where(lane == kk, rk_k, rk)
    carry = carry_ref[...] + jnp.sum(multihot, axis=0, keepdims=True)
    carry_ref[...] = carry
    cnt_ref[...] = carry.astype(jnp.int32)
    ti_ref[...] = ti.astype(jnp.int32)
    tw_ref[...] = tw
    rk_ref[...] = rk.astype(jnp.int32)


def _router(x1, g, w_router, b_router):
    n, d = x1.shape
    e = w_router.shape[1]
    tm = _pick_tile(n, 256, 8)
    wr = jnp.zeros((d, LANES), F32).at[:, :e].set(w_router)
    br = jnp.zeros((1, LANES), F32).at[0, :e].set(b_router)
    row = pl.BlockSpec((tm, LANES), lambda i: (i, 0))
    return pl.pallas_call(
        functools.partial(_router_kernel, n_experts=e),
        grid=(n // tm,),
        in_specs=[pl.BlockSpec((tm, d), lambda i: (i, 0)),
                  pl.BlockSpec((1, d), lambda i: (0, 0)),
                  pl.BlockSpec((d, LANES), lambda i: (0, 0)),
                  pl.BlockSpec((1, LANES), lambda i: (0, 0))],
        out_specs=[pl.BlockSpec((tm, d), lambda i: (i, 0)), row, row, row,
                   pl.BlockSpec((1, LANES), lambda i: (0, 0))],
        out_shape=[jax.ShapeDtypeStruct((n, d), F32),
                   jax.ShapeDtypeStruct((n, LANES), jnp.int32),
                   jax.ShapeDtypeStruct((n, LANES), F32),
                   jax.ShapeDtypeStruct((n, LANES), jnp.int32),
                   jax.ShapeDtypeStruct((1, LANES), jnp.int32)],
        scratch_shapes=[pltpu.VMEM((1, LANES), F32)],
        compiler_params=_params("arbitrary"),
        name="router_topk",
    )(x1, g.reshape(1, d), wr, br)


def _row_copy(src_ref, src_row, dst_ref, dst_row, sem):
    return pltpu.make_async_copy(src_ref.at[pl.ds(src_row, 1)], dst_ref.at[pl.ds(dst_row, 1)], sem)


def _expert_kernel(se_ref, sb0_ref, sn_ref, na_ref, *refs, rows, sbk, nf, nsuper, nblk):
    tok_refs = refs[:sbk]
    tokn_refs = refs[sbk:2 * sbk]
    (h_ref, wg_ref, wu_ref, wd_ref, bg_ref, bu_ref, bd_ref, ys_ref,
     xg_ref, xb_ref, acc_ref, wgb_ref, wub_ref, wdb_ref, gsem, osem) = refs[2 * sbk:]
    s = pl.program_id(0)
    f = pl.program_id(1)
    nsub = sn_ref[s]

    def gather(t_refs, count):
        for j in range(sbk):
            @pl.when(j < count)
            def _(j=j):
                def body(r, carry):
                    _row_copy(h_ref, t_refs[j][0, 0, r], xg_ref, j * rows + r, gsem).start()
                    return carry
                lax.fori_loop(0, rows, body, 0, unroll=8)

    def out_copy(j, blk0):
        return pltpu.make_async_copy(acc_ref.at[pl.ds(j * rows, rows)],
                                     ys_ref.at[pl.ds((blk0 + j) * rows, rows)], osem)

    n_act = na_ref[0]
    blocks_used = na_ref[1]

    def tail_block(j):
        return blocks_used + (s - n_act) * sbk + j

    def zero_copy(j):
        return pltpu.make_async_copy(acc_ref.at[pl.ds(0, rows)],
                                     ys_ref.at[pl.ds(tail_block(j) * rows, rows)], osem)

    @pl.when((f == 0) & (s == 0))
    def _():
        gather(tok_refs, nsub)

    @pl.when(f == 0)
    def _():
        for j in range(sbk):
            @pl.when(j < nsub)
            def _(j=j):
                pltpu.make_async_copy(h_ref.at[pl.ds(0, rows)], xg_ref.at[pl.ds(j * rows, rows)], gsem).wait()
        for j in range(sbk):
            @pl.when(j < nsub)
            def _(j=j):
                xb_ref[pl.ds(j * rows, rows), :] = xg_ref[pl.ds(j * rows, rows), :].astype(BF16)

        @pl.when(s + 1 < nsuper)
        def _():
            gather(tokn_refs, sn_ref[s + 1])

        @pl.when(s > 0)
        def _():
            for j in range(sbk):
                @pl.when(j < sn_ref[s - 1])
                def _(j=j):
                    out_copy(j, sb0_ref[s - 1]).wait()

        for j in range(sbk):
            @pl.when(j < nsub)
            def _(j=j):
                acc_ref[pl.ds(j * rows, rows), :] = jnp.broadcast_to(bd_ref[0], (rows, acc_ref.shape[1]))

        @pl.when(s == n_act)
        def _():
            acc_ref[pl.ds(0, rows), :] = jnp.zeros((rows, acc_ref.shape[1]), F32)

        for j in range(sbk):
            @pl.when((s >= n_act) & (tail_block(j) < nblk))
            def _(j=j):
                zero_copy(j).start()

    wgb_ref[...] = wg_ref[0].astype(BF16)
    wub_ref[...] = wu_ref[0].astype(BF16)
    wdb_ref[...] = wd_ref[0].astype(BF16)
    for j in range(sbk):
        @pl.when(j < nsub)
        def _(j=j):
            xb = xb_ref[pl.ds(j * rows, rows), :]
            g = jnp.dot(xb, wgb_ref[...], preferred_element_type=F32) + bg_ref[0]
            u = jnp.dot(xb, wub_ref[...], preferred_element_type=F32) + bu_ref[0]
            gate = jnp.minimum(g, SWIGLU_LIMIT)
            up = jnp.clip(u, -SWIGLU_LIMIT, SWIGLU_LIMIT)
            act = (up + 1.0) * gate * _sigmoid(SWIGLU_ALPHA * gate)
            acc_ref[pl.ds(j * rows, rows), :] += jnp.dot(act.astype(BF16), wdb_ref[...],
                                                         preferred_element_type=F32)

    @pl.when(f == nf - 1)
    def _():
        for j in range(sbk):
            @pl.when((s >= n_act) & (tail_block(j) < nblk))
            def _(j=j):
                zero_copy(j).wait()

        for j in range(sbk):
            @pl.when(j < nsub)
            def _(j=j):
                out_copy(j, sb0_ref[s]).start()

        @pl.when(s == nsuper - 1)
        def _():
            for j in range(sbk):
                @pl.when(j < nsub)
                def _(j=j):
                    out_copy(j, sb0_ref[s]).wait()


def _experts(h, slot_token, sb_expert, sb_blk0, sb_nsub, n_active, w_gate_up, b_gate_up, w_down, b_down,
             rows, sbk):
    n, d = h.shape
    p = slot_token.shape[0]
    e, _, dff2 = w_gate_up.shape
    dff = dff2 // 2
    tf = _pick_tile(dff, 512, LANES)
    nf = dff // tf
    nblk = p // rows
    nsuper = sb_expert.shape[0]
    tok3 = slot_token.reshape(nblk, 1, rows)

    def frozen_f(s, f, na):
        return jnp.where(s < na[0], f, nf - 1)

    def tok_spec(j, ahead):
        def index(s, f, se, sb0, sn, na):
            si = jnp.minimum(s + ahead, nsuper - 1)
            return (jnp.minimum(sb0[si] + j, nblk - 1), 0, 0)
        return pl.BlockSpec((1, 1, rows), index, memory_space=pltpu.SMEM)

    grid_spec = pltpu.PrefetchScalarGridSpec(
        num_scalar_prefetch=4,
        grid=(nsuper, nf),
        in_specs=[tok_spec(j, 0) for j in range(sbk)] + [tok_spec(j, 1) for j in range(sbk)] + [
            pl.BlockSpec(memory_space=pl.ANY),
            pl.BlockSpec((1, d, tf), lambda s, f, se, sb0, sn, na: (se[s], 0, frozen_f(s, f, na))),
            pl.BlockSpec((1, d, tf), lambda s, f, se, sb0, sn, na: (se[s], 0, nf + frozen_f(s, f, na))),
            pl.BlockSpec((1, tf, d), lambda s, f, se, sb0, sn, na: (se[s], frozen_f(s, f, na), 0)),
            pl.BlockSpec((1, 1, tf), lambda s, f, se, sb0, sn, na: (se[s], 0, frozen_f(s, f, na))),
            pl.BlockSpec((1, 1, tf), lambda s, f, se, sb0, sn, na: (se[s], 0, nf + frozen_f(s, f, na))),
            pl.BlockSpec((1, 1, d), lambda s, f, se, sb0, sn, na: (se[s], 0, 0))],
        out_specs=pl.BlockSpec(memory_space=pl.ANY),
        scratch_shapes=[pltpu.VMEM((sbk * rows, d), F32), pltpu.VMEM((sbk * rows, d), BF16),
                        pltpu.VMEM((sbk * rows, d), F32),
                        pltpu.VMEM((d, tf), BF16), pltpu.VMEM((d, tf), BF16), pltpu.VMEM((tf, d), BF16),
                        pltpu.SemaphoreType.DMA(()), pltpu.SemaphoreType.DMA(())],
    )
    return pl.pallas_call(
        functools.partial(_expert_kernel, rows=rows, sbk=sbk, nf=nf, nsuper=nsuper, nblk=nblk),
        grid_spec=grid_spec,
        out_shape=jax.ShapeDtypeStruct((p, d), F32),
        compiler_params=_params("arbitrary", "arbitrary"),
        name="expert_mlp",
    )(sb_expert, sb_blk0, sb_nsub, n_active, *([tok3] * (2 * sbk)), h, w_gate_up, w_gate_up, w_down,
      b_gate_up.reshape(e, 1, dff2), b_gate_up.reshape(e, 1, dff2), b_down.reshape(e, 1, d))


def _combine_kernel(dest_ref, destn_ref, x_ref, tw_ref, ys_ref, o_ref, buf_ref, sem, *, tm, nt):
    i = pl.program_id(0)
    slot = i % 2

    def gather(d_ref, s):
        def body(t, carry):
            for kk in range(TOP_K):
                _row_copy(ys_ref, d_ref[0, 0, t * TOP_K + kk], buf_ref.at[s, kk], t, sem.at[s]).start()
            return carry
        lax.fori_loop(0, tm, body, 0, unroll=2)

    @pl.when(i == 0)
    def _():
        gather(dest_ref, 0)

    @pl.when(i + 1 < nt)
    def _():
        gather(destn_ref, 1 - slot)

    for kk in range(TOP_K):
        pltpu.make_async_copy(ys_ref.at[pl.ds(0, tm)], buf_ref.at[slot, kk], sem.at[slot]).wait()
    tw = tw_ref[...]
    y = buf_ref[slot, 0] * tw[:, 0:1]
    for kk in range(1, TOP_K):
        y = y + buf_ref[slot, kk] * tw[:, kk:kk + 1]
    o_ref[...] = x_ref[...] + y


def _combine(x1, topw, dest, ys):
    n, d = x1.shape
    tm = _pick_tile(n, 128, 8)
    nt = n // tm
    dest3 = dest.reshape(nt, 1, tm * TOP_K)
    return pl.pallas_call(
        functools.partial(_combine_kernel, tm=tm, nt=nt),
        grid=(nt,),
        in_specs=[pl.BlockSpec((1, 1, tm * TOP_K), lambda i: (i, 0, 0), memory_space=pltpu.SMEM),
                  pl.BlockSpec((1, 1, tm * TOP_K), lambda i: (jnp.minimum(i + 1, nt - 1), 0, 0),
                               memory_space=pltpu.SMEM),
                  pl.BlockSpec((tm, d), lambda i: (i, 0)),
                  pl.BlockSpec((tm, LANES), lambda i: (i, 0)),
                  pl.BlockSpec(memory_space=pl.ANY)],
        out_specs=pl.BlockSpec((tm, d), lambda i: (i, 0)),
        out_shape=jax.ShapeDtypeStruct((n, d), F32),
        scratch_shapes=[pltpu.VMEM((2, TOP_K, tm, d), F32), pltpu.SemaphoreType.DMA((2,))],
        compiler_params=_params("arbitrary"),
        name="moe_combine",
    )(dest3, dest3, x1, topw, ys)


ATTN_BLOCK = 512
GDN_HEADS_PER_STEP = 8
GDN_GROUP = 4
EXPERT_ROWS = 512
EXPERT_SUBBLOCKS = 2


def _layer(x, positions, layer_idx, norm1_g, w_in, b_gate, conv_w, a_log, dt_bias, gdn_norm_g,
           q_norm_g, k_norm_g, lambda_q1, lambda_k1, lambda_q2, lambda_k2, subln_g,
           w_branch_a, w_branch_b, w_out, norm2_g, w_router, b_router,
           w_gate_up, b_gate_up, w_down, b_down):
    b, s, d = x.shape
    n = b * s
    gdn_heads = a_log.shape[0]
    gdn_dim = gdn_heads * HEAD_DIM
    diff_v = w_branch_b.shape[0]
    diff_heads = diff_v // (2 * HEAD_DIM)
    diff_qk = 2 * diff_heads * HEAD_DIM
    off_beta = 4 * gdn_dim
    off_alpha = off_beta + gdn_heads
    off_qb = off_alpha + gdn_heads
    off_gate = off_qb + 2 * diff_qk + diff_v
    assert w_in.shape[1] == off_gate + 2 * d
    n_experts = w_router.shape[1]

    x2 = x.reshape(n, d)
    h = _rmsnorm(x2, norm1_g, BF16)

    proj_a = _matmul(h, w_in, 0, off_beta, F32, name="proj_gdn")
    hg = min(GDN_HEADS_PER_STEP, gdn_heads)
    ngroups = gdn_heads // hg
    cols = []
    for gi in range(ngroups):
        cols += [off_beta + gi * hg + i for i in range(hg)] + [off_alpha + gi * hg + i for i in range(hg)]
        cols += [off_beta] * (LANES - 2 * hg)
    lane_valid = np.tile(np.arange(LANES) < 2 * hg, ngroups)
    w_ba = jnp.where(lane_valid[None, :], w_in[:, np.asarray(cols)], 0.0)
    ba = _matmul(h, w_ba, 0, ngroups * LANES, F32, name="proj_beta_alpha")
    w_b = w_in[:, off_qb:]
    proj_b = _matmul(h, w_b, 0, w_b.shape[1], F32, name="proj_attn_gate")

    alog_p = jnp.zeros((ngroups, 1, LANES), F32).at[:, 0, hg:2 * hg].set(a_log.reshape(ngroups, hg))
    dtb_p = jnp.zeros((ngroups, 1, LANES), F32).at[:, 0, hg:2 * hg].set(dt_bias.reshape(ngroups, hg))
    oa = _gdn(proj_a.reshape(b, s, off_beta), ba.reshape(b, s, ngroups * LANES), conv_w,
              alog_p, dtb_p, gdn_norm_g, gdn_heads, hg)

    pb3 = proj_b.reshape(b, s, proj_b.shape[1])
    pos_f = positions.astype(F32).reshape(b, s, 1)
    qr = _qknorm_rope(pb3, 0, diff_qk, pos_f, q_norm_g, HEAD_DIM ** -0.5, "q_norm_rope")
    kr = _qknorm_rope(pb3, 1, diff_qk, pos_f, k_norm_g, 1.0, "k_norm_rope")
    lam_init = 0.8 - 0.6 * math.exp(-0.3 * layer_idx)
    ob = _diff_attn(qr, kr, pb3, 2 * diff_qk // (2 * HEAD_DIM), lambda_q1, lambda_k1, lambda_q2, lambda_k2, subln_g,
                    diff_heads, lam_init)

    merged = _merge(oa.reshape(n, gdn_dim), ob.reshape(n, diff_v), w_branch_a, w_branch_b,
                    proj_b, 2 * diff_qk + diff_v, b_gate, d)
    x1 = _matmul(merged, w_out, 0, d, F32, res=x2, name="out_proj")

    h2, topi, topw, rank, counts = _router(x1, norm2_g, w_router, b_router)
    rows, sbk = EXPERT_ROWS, EXPERT_SUBBLOCKS
    top_e = topi[:, :TOP_K]
    cnt = counts[0, :n_experts]
    nb = (cnt + rows - 1) // rows
    blk_end = jnp.cumsum(nb)
    blk_start = blk_end - nb
    dest = (blk_start[top_e] * rows + rank[:, :TOP_K]).reshape(n * TOP_K)
    p_rows = n * TOP_K + n_experts * rows
    nblk = p_rows // rows
    slot_token = jnp.zeros((p_rows,), jnp.int32).at[dest].set(
        jnp.repeat(jnp.arange(n, dtype=jnp.int32), TOP_K))
    ns = (nb + sbk - 1) // sbk
    ns_end = jnp.cumsum(ns)
    ns_start = ns_end - ns
    n_active = ns_end[-1]
    nsuper = nblk // sbk + n_experts
    s_idx = jnp.arange(nsuper, dtype=jnp.int32)
    active = s_idx < n_active
    s_src = jnp.where(active, s_idx, jnp.maximum(n_active - 1, 0))
    sb_expert = jnp.minimum(jnp.sum(s_src[:, None] >= ns_end[None, :], axis=1), n_experts - 1)
    sb_local = s_src - ns_start[sb_expert]
    sb_blk0 = blk_start[sb_expert] + sb_local * sbk
    sb_nsub = jnp.where(active, jnp.clip(nb[sb_expert] - sb_local * sbk, 0, sbk), 0)
    ys = _experts(h2, slot_token, sb_expert.astype(jnp.int32), sb_blk0.astype(jnp.int32),
                  sb_nsub.astype(jnp.int32), jnp.stack([n_active, blk_end[-1]]).astype(jnp.int32),
                  w_gate_up, b_gate_up, w_down, b_down, rows, sbk)
    out = _combine(x1, topw, dest, ys)
    return out.reshape(b, s, d)


def kernel(x, positions, norm1_g, w_in, b_gate, conv_w, a_log, dt_bias, gdn_norm_g, q_norm_g, k_norm_g, lambda_q1, lambda_k1, lambda_q2, lambda_k2, subln_g, w_branch_a, w_branch_b, w_out, norm2_g, w_router, b_router, w_gate_up, b_gate_up, w_down, b_down):
    depth = norm1_g.shape[0]
    for l in range(depth):
        x = _layer(x, positions, l, norm1_g[l], w_in[l], b_gate[l], conv_w[l], a_log[l], dt_bias[l],
                   gdn_norm_g[l], q_norm_g[l], k_norm_g[l], lambda_q1[l], lambda_k1[l], lambda_q2[l],
                   lambda_k2[l], subln_g[l], w_branch_a[l], w_branch_b[l], w_out[l], norm2_g[l],
                   w_router[l], b_router[l], w_gate_up[l], b_gate_up[l], w_down[l], b_down[l])
    return x
```

```python
import functools
import math

import numpy as np
import jax
import jax.numpy as jnp
from jax import lax
from jax.experimental import pallas as pl
from jax.experimental.pallas import tpu as pltpu

F32 = jnp.float32
BF16 = jnp.bfloat16

NORM_EPS = 1e-6
NEG_INF = -1e30
CHUNK = 64
HEAD_DIM = 128
ROT_DIM = HEAD_DIM // 4
ROPE_THETA = 500000.0
TOP_K = 4
SWIGLU_ALPHA = 1.702
SWIGLU_LIMIT = 7.0
LANES = 128
VMEM_LIMIT_BYTES = 60 * 1024 * 1024


def _pick_tile(n, target, quantum):
    if n <= target:
        return n
    t = (target // quantum) * quantum
    while t > quantum and n % t:
        t -= quantum
    assert n % t == 0, (n, target, quantum)
    return t


def _params(*sem):
    return pltpu.CompilerParams(dimension_semantics=sem, vmem_limit_bytes=VMEM_LIMIT_BYTES)


def _sigmoid(x):
    return 1.0 / (1.0 + jnp.exp(-x))


def _softplus(x):
    return jnp.maximum(x, 0.0) + jnp.log(1.0 + jnp.exp(-jnp.abs(x)))


def _dot(a, b):
    return jnp.dot(a.astype(BF16), b.astype(BF16), preferred_element_type=F32)


def _dot_nt(a, b):
    return lax.dot_general(a.astype(BF16), b.astype(BF16), (((1,), (1,)), ((), ())),
                           preferred_element_type=F32)


def _dot_tn(a, b):
    return lax.dot_general(a.astype(BF16), b.astype(BF16), (((0,), (0,)), ((), ())),
                           preferred_element_type=F32)


def _rmsnorm_kernel(x_ref, g_ref, o_ref):
    x = x_ref[...]
    y = x * lax.rsqrt(jnp.mean(x * x, axis=-1, keepdims=True) + NORM_EPS)
    o_ref[...] = (y * g_ref[...]).astype(o_ref.dtype)


def _rmsnorm(x2d, g, out_dtype):
    n, d = x2d.shape
    tm = _pick_tile(n, 512, 8)
    return pl.pallas_call(
        _rmsnorm_kernel,
        grid=(n // tm,),
        in_specs=[pl.BlockSpec((tm, d), lambda i: (i, 0)), pl.BlockSpec((1, d), lambda i: (0, 0))],
        out_specs=pl.BlockSpec((tm, d), lambda i: (i, 0)),
        out_shape=jax.ShapeDtypeStruct((n, d), out_dtype),
        compiler_params=_params("parallel"),
        name="rmsnorm",
    )(x2d, g.reshape(1, d))


def _matmul_kernel(*refs, has_res):
    if has_res:
        x_ref, w_ref, r_ref, o_ref, wb_ref = refs
    else:
        x_ref, w_ref, o_ref, wb_ref = refs

    @pl.when(pl.program_id(1) == 0)
    def _():
        wb_ref[...] = w_ref[...].astype(BF16)

    acc = jnp.dot(x_ref[...], wb_ref[...], preferred_element_type=F32)
    if has_res:
        acc = acc + r_ref[...]
    o_ref[...] = acc.astype(o_ref.dtype)


def _matmul(x, w, col0, ncols, out_dtype, res=None, name="matmul"):
    m, k = x.shape
    tn = _pick_tile(ncols, 1024, LANES)
    tm = _pick_tile(m, 512, 8)
    assert col0 % tn == 0
    cb = col0 // tn
    in_specs = [pl.BlockSpec((tm, k), lambda j, i: (i, 0)),
                pl.BlockSpec((k, tn), lambda j, i: (0, cb + j))]
    args = [x, w]
    if res is not None:
        in_specs.append(pl.BlockSpec((tm, tn), lambda j, i: (i, j)))
        args.append(res)
    return pl.pallas_call(
        functools.partial(_matmul_kernel, has_res=res is not None),
        grid=(ncols // tn, m // tm),
        in_specs=in_specs,
        out_specs=pl.BlockSpec((tm, tn), lambda j, i: (i, j)),
        out_shape=jax.ShapeDtypeStruct((m, ncols), out_dtype),
        scratch_shapes=[pltpu.VMEM((k, tn), BF16)],
        compiler_params=_params("parallel", "arbitrary"),
        name=name,
    )(*args)


def _gdn_kernel(q_ref, k_ref, v_ref, z_ref, ba_ref, cwq_ref, cwk_ref, cwv_ref, alog_ref, dtb_ref,
                ng_ref, o_ref, state_ref, tq_ref, tk_ref, tv_ref, eq_ref, ek_ref, ev_ref, *, hg):
    C = CHUNK
    D = HEAD_DIM

    @pl.when(pl.program_id(2) == 0)
    def _():
        state_ref[...] = jnp.zeros_like(state_ref)
        tq_ref[...] = jnp.zeros_like(tq_ref)
        tk_ref[...] = jnp.zeros_like(tk_ref)
        tv_ref[...] = jnp.zeros_like(tv_ref)

    def conv_silu(u_ref, tail_ref, ext_ref, w_ref):
        u = u_ref[0]
        ext_ref[0:8, :] = tail_ref[...]
        ext_ref[8:8 + C, :] = u
        w = w_ref[...]
        y = ext_ref[5:5 + C, :] * w[0:1, :]
        y = y + ext_ref[6:6 + C, :] * w[1:2, :]
        y = y + ext_ref[7:7 + C, :] * w[2:3, :]
        y = y + u * w[3:4, :]
        tail_ref[...] = u[C - 8:C, :]
        return y * _sigmoid(y)

    def l2norm_heads(x, scale):
        parts = []
        for i in range(hg):
            xi = x[:, i * D:(i + 1) * D]
            parts.append(xi * (lax.rsqrt(jnp.sum(xi * xi, axis=-1, keepdims=True) + NORM_EPS) * scale))
        return jnp.concatenate(parts, axis=1)

    qc = l2norm_heads(conv_silu(q_ref, tq_ref, eq_ref, cwq_ref), D ** -0.5)
    kc = l2norm_heads(conv_silu(k_ref, tk_ref, ek_ref, cwk_ref), 1.0)
    vc = conv_silu(v_ref, tv_ref, ev_ref, cwv_ref)

    G = GDN_GROUP
    R = G * C
    shift_c = C.bit_length() - 1

    ba = ba_ref[0]
    beta_full = _sigmoid(ba)
    g_full = -jnp.exp(alog_ref[0]) * _softplus(ba + dtb_ref[0])
    r64 = lax.broadcasted_iota(jnp.int32, (C, C), 0)
    c64 = lax.broadcasted_iota(jnp.int32, (C, C), 1)
    gcum_full = jnp.dot((r64 >= c64).astype(F32), g_full, preferred_element_type=F32,
                        precision=lax.Precision.HIGHEST)
    gcum_t = gcum_full.T
    row = lax.broadcasted_iota(jnp.int32, (R, R), 0)
    col = lax.broadcasted_iota(jnp.int32, (R, R), 1)
    same_head = jnp.right_shift(row, shift_c) == jnp.right_shift(col, shift_c)
    tril = same_head & (row >= col)
    strict = same_head & (row > col)
    lane2 = lax.broadcasted_iota(jnp.int32, (1, 2 * D), 1)
    r2 = lax.broadcasted_iota(jnp.int32, (2 * D, 2 * D), 0)
    c2 = lax.broadcasted_iota(jnp.int32, (2 * D, 2 * D), 1)
    pair_block = (r2 < D) == (c2 < D)

    def lanes_of(x, i0):
        return jnp.concatenate([x[i0 * C:(i0 + 1) * C], x[(i0 + 1) * C:(i0 + 2) * C]], axis=1)

    for h0 in range(0, hg, G):
        heads = range(h0, h0 + G)

        def rows_of(x):
            return jnp.concatenate([x[:, i * D:(i + 1) * D] for i in heads], axis=0)

        beta_r = jnp.concatenate([beta_full[:, i:i + 1] for i in heads], axis=0)
        gc_r = jnp.concatenate([gcum_full[:, hg + i:hg + i + 1] for i in heads], axis=0)
        gc_l = jnp.concatenate([gcum_t[hg + i:hg + i + 1, :] for i in heads], axis=1)
        g_last = [gcum_full[C - 1:C, hg + i:hg + i + 1] for i in heads]
        gl_r = jnp.concatenate([jnp.broadcast_to(g, (C, 1)) for g in g_last], axis=0)

        qn, kn, v_r = rows_of(qc), rows_of(kc), rows_of(vc)
        decay = jnp.where(tril, jnp.exp(jnp.where(tril, gc_r - gc_l, 0.0)), 0.0)
        eg = jnp.exp(gc_r)
        kb = kn * beta_r
        gram = _dot_nt(jnp.concatenate([kb, qn], axis=0), kn)
        nmat = jnp.where(strict, gram[:R] * decay, 0.0)
        qk = gram[R:] * decay
        kbg_r = kb * eg
        qg_r = qn * eg
        kdec_r = kn * jnp.exp(gl_r - gc_r)
        states, ks_rows, o1_rows = [], [], []
        for pr in range(G // 2):
            s = state_ref[h0 // 2 + pr]
            xs = _dot(jnp.concatenate([lanes_of(kbg_r, 2 * pr), lanes_of(qg_r, 2 * pr)], axis=0), s)
            states.append(s)
            ks_rows += [xs[:C, :D], xs[:C, D:]]
            o1_rows += [xs[C:, :D], xs[C:, D:]]
        y = v_r * beta_r - jnp.concatenate(ks_rows, axis=0)
        m = -nmat
        for _ in range(shift_c - 1):
            mb = m.astype(BF16)
            x = jnp.dot(mb, jnp.concatenate([mb, y.astype(BF16)], axis=1), preferred_element_type=F32)
            m = x[:, :R]
            y = y + x[:, R:]
        vnew_r = y + _dot(m, y)
        for pr in range(G // 2):
            upd = _dot_tn(lanes_of(kdec_r, 2 * pr), lanes_of(vnew_r, 2 * pr))
            gl_lanes = jnp.where(lane2 < D, jnp.exp(g_last[2 * pr]), jnp.exp(g_last[2 * pr + 1]))
            state_ref[h0 // 2 + pr] = states[pr] * gl_lanes + jnp.where(pair_block, upd, 0.0)
        o = jnp.concatenate(o1_rows, axis=0) + _dot(qk, vnew_r)
        on = o * lax.rsqrt(jnp.mean(o * o, axis=-1, keepdims=True) + NORM_EPS) * ng_ref[...]
        z_r = rows_of(z_ref[0])
        out = (on * (z_r * _sigmoid(z_r))).astype(o_ref.dtype)
        for j, i in enumerate(heads):
            o_ref[0, :, i * D:(i + 1) * D] = out[j * C:(j + 1) * C]


def _gdn(proj_a, ba, conv_w, alog_p, dtb_p, norm_g, heads, hg):
    b, s, _ = proj_a.shape
    assert hg % 2 == 0 and heads % hg == 0
    ngroups = heads // hg
    w = hg * HEAD_DIM
    nc = s // CHUNK

    def colspec(off):
        return pl.BlockSpec((1, CHUNK, w), lambda bi, gi, ci, off=off: (bi, ci, off * ngroups + gi))

    def cwspec(off):
        return pl.BlockSpec((4, w), lambda bi, gi, ci, off=off: (0, off * ngroups + gi))

    vec = pl.BlockSpec((1, 1, LANES), lambda bi, gi, ci: (gi, 0, 0))
    return pl.pallas_call(
        functools.partial(_gdn_kernel, hg=hg),
        grid=(b, ngroups, nc),
        in_specs=[colspec(0), colspec(1), colspec(2), colspec(3),
                  pl.BlockSpec((1, CHUNK, LANES), lambda bi, gi, ci: (bi, ci, gi)),
                  cwspec(0), cwspec(1), cwspec(2), vec, vec,
                  pl.BlockSpec((1, HEAD_DIM), lambda bi, gi, ci: (0, 0))],
        out_specs=pl.BlockSpec((1, CHUNK, w), lambda bi, gi, ci: (bi, ci, gi)),
        out_shape=jax.ShapeDtypeStruct((b, s, heads * HEAD_DIM), BF16),
        scratch_shapes=[pltpu.VMEM((hg // 2, 2 * HEAD_DIM, 2 * HEAD_DIM), F32)]
        + [pltpu.VMEM((8, w), F32)] * 3 + [pltpu.VMEM((8 + CHUNK, w), F32)] * 3,
        compiler_params=_params("parallel", "parallel", "arbitrary"),
        name="gated_delta_rule",
    )(proj_a, proj_a, proj_a, proj_a, ba, conv_w, conv_w, conv_w, alog_p, dtb_p,
      norm_g.reshape(1, HEAD_DIM))


def _qknorm_rope_kernel(x_ref, pos_ref, g_ref, freq_ref, o_ref, *, scales):
    pos = pos_ref[0]
    ang = pos * freq_ref[...]
    cos = jnp.cos(ang)
    sin = jnp.sin(ang)
    lane = lax.broadcasted_iota(jnp.int32, ang.shape, 1)
    first_half = lane < ROT_DIM // 2
    sin_signed = jnp.where(first_half, -sin, sin)
    ngroups = x_ref.shape[-1] // HEAD_DIM
    per_range = ngroups // len(scales)
    for j in range(ngroups):
        which = j // per_range
        sl = slice(j * HEAD_DIM, (j + 1) * HEAD_DIM)
        x = x_ref[0, :, sl]
        xn = x * lax.rsqrt(jnp.mean(x * x, axis=-1, keepdims=True) + NORM_EPS) * g_ref[which:which + 1, :]
        partner = jnp.where(first_half, pltpu.roll(xn, HEAD_DIM - ROT_DIM // 2, 1),
                            pltpu.roll(xn, ROT_DIM // 2, 1))
        o_ref[0, :, sl] = ((xn * cos + partner * sin_signed) * scales[which]).astype(o_ref.dtype)


def _qknorm_rope(proj_b, width, pos_f, gains, scales):
    b, s, _ = proj_b.shape
    ts = _pick_tile(s, 256, 8)
    half = ROT_DIM // 2
    inv_freq = np.power(np.float32(ROPE_THETA),
                        -np.arange(half, dtype=np.float32) * np.float32(2.0) / np.float32(ROT_DIM))
    freq = np.zeros((1, LANES), np.float32)
    freq[0, :half] = inv_freq
    freq[0, half:ROT_DIM] = inv_freq
    g = jnp.stack([x.reshape(HEAD_DIM) for x in gains])
    return pl.pallas_call(
        functools.partial(_qknorm_rope_kernel, scales=tuple(scales)),
        grid=(b, s // ts),
        in_specs=[pl.BlockSpec((1, ts, width), lambda bi, si: (bi, si, 0)),
                  pl.BlockSpec((1, ts, 1), lambda bi, si: (bi, si, 0)),
                  pl.BlockSpec((len(gains), HEAD_DIM), lambda bi, si: (0, 0)),
                  pl.BlockSpec((1, LANES), lambda bi, si: (0, 0))],
        out_specs=pl.BlockSpec((1, ts, width), lambda bi, si: (bi, si, 0)),
        out_shape=jax.ShapeDtypeStruct((b, s, width), BF16),
        compiler_params=_params("parallel", "parallel"),
        name="qk_norm_rope",
    )(proj_b, pos_f, g, jnp.asarray(freq))


def _diff_attn_kernel(q_ref, k_ref, v_ref, lq1_ref, lk1_ref, lq2_ref, lk2_ref, sg_ref, o_ref,
                      acc_ref, m_ref, l_ref, *, tq, lam_init):
    D = HEAD_DIM
    qi = pl.program_id(2)
    q1 = q_ref[0, :, :D]
    q2 = q_ref[0, :, D:]
    acc_ref[...] = jnp.zeros_like(acc_ref)
    m_ref[...] = jnp.full_like(m_ref, NEG_INF)
    l_ref[...] = jnp.zeros_like(l_ref)

    def lanes(x, width):
        return jnp.concatenate([x] * (width // LANES), axis=1)

    def step(start, mask):
        kblk = k_ref[0, pl.ds(start, tq), :]
        vblk = v_ref[0, pl.ds(start, tq), :].astype(BF16)
        s = jnp.concatenate([_dot_nt(q1, kblk[:, :D]), _dot_nt(q2, kblk[:, D:])], axis=0)
        if mask is not None:
            s = jnp.where(mask, s, NEG_INF)
        m_prev = m_ref[...]
        m_next = jnp.maximum(m_prev, jnp.max(s, axis=-1, keepdims=True))
        alpha = jnp.exp2(m_prev - m_next)
        p = jnp.exp2(s - lanes(m_next, tq))
        l_ref[...] = alpha * l_ref[...] + jnp.sum(p, axis=-1, keepdims=True)
        acc_ref[...] = lanes(alpha, 2 * D) * acc_ref[...] + jnp.dot(p.astype(BF16), vblk,
                                                                    preferred_element_type=F32)
        m_ref[...] = m_next

    def body(j, carry):
        step(pl.multiple_of(j * tq, tq), None)
        return carry

    lax.fori_loop(0, qi, body, 0)
    shift_c = CHUNK.bit_length() - 1
    row = lax.broadcasted_iota(jnp.int32, (2 * tq, tq), 0)
    row = jnp.right_shift(jnp.where(row >= tq, row - tq, row), shift_c)
    col = jnp.right_shift(lax.broadcasted_iota(jnp.int32, (2 * tq, tq), 1), shift_c)
    step(pl.multiple_of(qi * tq, tq), row >= col)

    lam = (jnp.exp(jnp.sum(lq1_ref[...] * lk1_ref[...], axis=-1, keepdims=True))
           - jnp.exp(jnp.sum(lq2_ref[...] * lk2_ref[...], axis=-1, keepdims=True)) + lam_init)
    a = acc_ref[...] / lanes(l_ref[...], 2 * D)
    o = a[:tq] - lam * a[tq:]
    on = o * lax.rsqrt(jnp.mean(o * o, axis=-1, keepdims=True) + NORM_EPS)
    o_ref[0] = (on * sg_ref[...] * (1.0 - lam_init)).astype(o_ref.dtype)


def _diff_attn(qk, v_src, v_col_block, lq1, lk1, lq2, lk2, subln_g, heads, lam_init):
    b, s, _ = qk.shape
    hw = 2 * HEAD_DIM
    tq = _pick_tile(s, ATTN_BLOCK, CHUNK)
    vec = pl.BlockSpec((1, HEAD_DIM), lambda bi, hi, qi: (0, 0))
    return pl.pallas_call(
        functools.partial(_diff_attn_kernel, tq=tq, lam_init=lam_init),
        grid=(b, heads, s // tq),
        in_specs=[pl.BlockSpec((1, tq, hw), lambda bi, hi, qi: (bi, qi, hi)),
                  pl.BlockSpec((1, s, hw), lambda bi, hi, qi: (bi, 0, heads + hi)),
                  pl.BlockSpec((1, s, hw), lambda bi, hi, qi: (bi, 0, v_col_block + hi)),
                  vec, vec, vec, vec,
                  pl.BlockSpec((1, hw), lambda bi, hi, qi: (0, 0))],
        out_specs=pl.BlockSpec((1, tq, hw), lambda bi, hi, qi: (bi, qi, hi)),
        out_shape=jax.ShapeDtypeStruct((b, s, heads * hw), BF16),
        scratch_shapes=[pltpu.VMEM((2 * tq, hw), F32)] + [pltpu.VMEM((2 * tq, LANES), F32)] * 2,
        compiler_params=_params("parallel", "parallel", "arbitrary"),
        name="diff_attention",
    )(qk, qk, v_src, lq1.reshape(1, -1), lk1.reshape(1, -1), lq2.reshape(1, -1), lk2.reshape(1, -1),
      subln_g.reshape(1, hw))


def _merge_kernel(oa_ref, ob_ref, wa_ref, wb_ref, ga_ref, gb_ref, ba_ref, bb_ref, o_ref, wa_s, wb_s):
    @pl.when(pl.program_id(1) == 0)
    def _():
        wa_s[...] = wa_ref[...].astype(BF16)
        wb_s[...] = wb_ref[...].astype(BF16)

    ya = jnp.dot(oa_ref[...], wa_s[...], preferred_element_type=F32)
    yb = jnp.dot(ob_ref[...], wb_s[...], preferred_element_type=F32)
    merged = _sigmoid(ga_ref[...] + ba_ref[...]) * ya + _sigmoid(gb_ref[...] + bb_ref[...]) * yb
    o_ref[...] = merged.astype(o_ref.dtype)


def _merge(oa, ob, w_a, w_b, proj_b, gate_col0, b_gate, d):
    m, ka = oa.shape
    kb = ob.shape[1]
    tn = _pick_tile(d, 512, LANES)
    tm = _pick_tile(m, 512, 8)
    assert gate_col0 % tn == 0
    ga = gate_col0 // tn
    gb = (gate_col0 + d) // tn
    nb = d // tn
    bg = b_gate.reshape(1, 2 * d)
    return pl.pallas_call(
        _merge_kernel,
        grid=(nb, m // tm),
        in_specs=[pl.BlockSpec((tm, ka), lambda j, i: (i, 0)),
                  pl.BlockSpec((tm, kb), lambda j, i: (i, 0)),
                  pl.BlockSpec((ka, tn), lambda j, i: (0, j)),
                  pl.BlockSpec((kb, tn), lambda j, i: (0, j)),
                  pl.BlockSpec((tm, tn), lambda j, i: (i, ga + j)),
                  pl.BlockSpec((tm, tn), lambda j, i: (i, gb + j)),
                  pl.BlockSpec((1, tn), lambda j, i: (0, j)),
                  pl.BlockSpec((1, tn), lambda j, i: (0, nb + j))],
        out_specs=pl.BlockSpec((tm, tn), lambda j, i: (i, j)),
        out_shape=jax.ShapeDtypeStruct((m, d), BF16),
        scratch_shapes=[pltpu.VMEM((ka, tn), BF16), pltpu.VMEM((kb, tn), BF16)],
        compiler_params=_params("parallel", "arbitrary"),
        name="branch_merge",
    )(oa, ob, w_a, w_b, proj_b, proj_b, bg, bg)


def _router_kernel(x_ref, g_ref, wr_ref, br_ref, h_ref, ti_ref, tw_ref, rk_ref, cnt_ref, carry_ref,
                   *, n_experts):
    @pl.when(pl.program_id(0) == 0)
    def _():
        carry_ref[...] = jnp.zeros_like(carry_ref)

    x = x_ref[...]
    h = x * lax.rsqrt(jnp.mean(x * x, axis=-1, keepdims=True) + NORM_EPS) * g_ref[...]
    h_ref[...] = h
    logits = jnp.dot(h, wr_ref[...], preferred_element_type=F32,
                     precision=lax.Precision.HIGHEST) + br_ref[...]
    tm = x.shape[0]
    lane = lax.broadcasted_iota(jnp.int32, (tm, LANES), 1)
    lane_f = lane.astype(F32)
    cur = jnp.where(lane < n_experts, logits, -jnp.inf)
    vals, hots = [], []
    ti = jnp.zeros((tm, LANES), F32)
    for kk in range(TOP_K):
        mx = jnp.max(cur, axis=-1, keepdims=True)
        idx = jnp.min(jnp.where(cur == mx, lane_f, float(LANES)), axis=-1, keepdims=True)
        hot = lane_f == idx
        vals.append(mx)
        hots.append(hot)
        ti = jnp.where(lane == kk, idx, ti)
        cur = jnp.where(hot, -jnp.inf, cur)
    exps = [jnp.exp(v - vals[0]) for v in vals]
    denom = exps[0] + exps[1] + exps[2] + exps[3]
    tw = jnp.zeros((tm, LANES), F32)
    for kk in range(TOP_K):
        tw = jnp.where(lane == kk, exps[kk] / denom, tw)
    multihot = jnp.zeros((tm, LANES), F32)
    for hot in hots:
        multihot = multihot + hot.astype(F32)
    r = lax.broadcasted_iota(jnp.int32, (tm, tm), 0)
    c = lax.broadcasted_iota(jnp.int32, (tm, tm), 1)
    prefix = jnp.dot((r > c).astype(BF16), multihot.astype(BF16), preferred_element_type=F32)
    prefix = prefix + carry_ref[...]
    rk = jnp.zeros((tm, LANES), F32)
    for kk in range(TOP_K):
        rk_k = jnp.sum(jnp.where(hots[kk], prefix, 0.0), axis=-1, keepdims=True)
        rk = jnp.where(lane == kk, rk_k, rk)
    carry = carry_ref[...] + jnp.sum(multihot, axis=0, keepdims=True)
    carry_ref[...] = carry
    cnt_ref[...] = carry.astype(jnp.int32)
    ti_ref[...] = ti.astype(jnp.int32)
    tw_ref[...] = tw
    rk_ref[...] = rk.astype(jnp.int32)


def _router(x1, g, w_router, b_router):
    n, d = x1.shape
    e = w_router.shape[1]
    tm = _pick_tile(n, 256, 8)
    wr = jnp.zeros((d, LANES), F32).at[:, :e].set(w_router)
    br = jnp.zeros((1, LANES), F32).at[0, :e].set(b_router)
    row = pl.BlockSpec((tm, LANES), lambda i: (i, 0))
    return pl.pallas_call(
        functools.partial(_router_kernel, n_experts=e),
        grid=(n // tm,),
        in_specs=[pl.BlockSpec((tm, d), lambda i: (i, 0)),
                  pl.BlockSpec((1, d), lambda i: (0, 0)),
                  pl.BlockSpec((d, LANES), lambda i: (0, 0)),
                  pl.BlockSpec((1, LANES), lambda i: (0, 0))],
        out_specs=[pl.BlockSpec((tm, d), lambda i: (i, 0)), row, row, row,
                   pl.BlockSpec((1, LANES), lambda i: (0, 0))],
        out_shape=[jax.ShapeDtypeStruct((n, d), F32),
                   jax.ShapeDtypeStruct((n, LANES), jnp.int32),
                   jax.ShapeDtypeStruct((n, LANES), F32),
                   jax.ShapeDtypeStruct((n, LANES), jnp.int32),
                   jax.ShapeDtypeStruct((1, LANES), jnp.int32)],
        scratch_shapes=[pltpu.VMEM((1, LANES), F32)],
        compiler_params=_params("arbitrary"),
        name="router_topk",
    )(x1, g.reshape(1, d), wr, br)


def _row_copy(src_ref, src_row, dst_ref, dst_row, sem):
    return pltpu.make_async_copy(src_ref.at[pl.ds(src_row, 1)], dst_ref.at[pl.ds(dst_row, 1)], sem)


def _expert_kernel(se_ref, sb0_ref, sn_ref, na_ref, *refs, rows, sbk, nf, nsuper, nblk):
    tok_refs = refs[:sbk]
    tokn_refs = refs[sbk:2 * sbk]
    (h_ref, wg_ref, wu_ref, wd_ref, bg_ref, bu_ref, bd_ref, ys_ref,
     xg_ref, xb_ref, acc_ref, wgb_ref, wub_ref, wdb_ref, gsem, osem) = refs[2 * sbk:]
    s = pl.program_id(0)
    f = pl.program_id(1)
    nsub = sn_ref[s]

    def gather(t_refs, count):
        for j in range(sbk):
            @pl.when(j < count)
            def _(j=j):
                def body(r, carry):
                    _row_copy(h_ref, t_refs[j][0, 0, r], xg_ref, j * rows + r, gsem).start()
                    return carry
                lax.fori_loop(0, rows, body, 0, unroll=8)

    def out_copy(j, blk0):
        return pltpu.make_async_copy(acc_ref.at[pl.ds(j * rows, rows)],
                                     ys_ref.at[pl.ds((blk0 + j) * rows, rows)], osem)

    n_act = na_ref[0]
    blocks_used = na_ref[1]

    def tail_block(j):
        return blocks_used + (s - n_act) * sbk + j

    def zero_copy(j):
        return pltpu.make_async_copy(acc_ref.at[pl.ds(0, rows)],
                                     ys_ref.at[pl.ds(tail_block(j) * rows, rows)], osem)

    @pl.when((f == 0) & (s == 0))
    def _():
        gather(tok_refs, nsub)

    @pl.when(f == 0)
    def _():
        for j in range(sbk):
            @pl.when(j < nsub)
            def _(j=j):
                pltpu.make_async_copy(h_ref.at[pl.ds(0, rows)], xg_ref.at[pl.ds(j * rows, rows)], gsem).wait()
        for j in range(sbk):
            @pl.when(j < nsub)
            def _(j=j):
                xb_ref[pl.ds(j * rows, rows), :] = xg_ref[pl.ds(j * rows, rows), :].astype(BF16)

        @pl.when(s + 1 < nsuper)
        def _():
            gather(tokn_refs, sn_ref[s + 1])

        @pl.when(s > 0)
        def _():
            for j in range(sbk):
                @pl.when(j < sn_ref[s - 1])
                def _(j=j):
                    out_copy(j, sb0_ref[s - 1]).wait()

        for j in range(sbk):
            @pl.when(j < nsub)
            def _(j=j):
                acc_ref[pl.ds(j * rows, rows), :] = jnp.broadcast_to(bd_ref[0], (rows, acc_ref.shape[1]))

        @pl.when(s == n_act)
        def _():
            acc_ref[pl.ds(0, rows), :] = jnp.zeros((rows, acc_ref.shape[1]), F32)

        for j in range(sbk):
            @pl.when((s >= n_act) & (tail_block(j) < nblk))
            def _(j=j):
                zero_copy(j).start()

    wgb_ref[...] = wg_ref[0].astype(BF16)
    wub_ref[...] = wu_ref[0].astype(BF16)
    wdb_ref[...] = wd_ref[0].astype(BF16)
    for j in range(sbk):
        @pl.when(j < nsub)
        def _(j=j):
            xb = xb_ref[pl.ds(j * rows, rows), :]
            g = jnp.dot(xb, wgb_ref[...], preferred_element_type=F32) + bg_ref[0]
            u = jnp.dot(xb, wub_ref[...], preferred_element_type=F32) + bu_ref[0]
            gate = jnp.minimum(g, SWIGLU_LIMIT)
            up = jnp.clip(u, -SWIGLU_LIMIT, SWIGLU_LIMIT)
            act = (up + 1.0) * gate * _sigmoid(SWIGLU_ALPHA * gate)
            acc_ref[pl.ds(j * rows, rows), :] += jnp.dot(act.astype(BF16), wdb_ref[...],
                                                         preferred_element_type=F32)

    @pl.when(f == nf - 1)
    def _():
        for j in range(sbk):
            @pl.when((s >= n_act) & (tail_block(j) < nblk))
            def _(j=j):
                zero_copy(j).wait()

        for j in range(sbk):
            @pl.when(j < nsub)
            def _(j=j):
                out_copy(j, sb0_ref[s]).start()

        @pl.when(s == nsuper - 1)
        def _():
            for j in range(sbk):
                @pl.when(j < nsub)
                def _(j=j):
                    out_copy(j, sb0_ref[s]).wait()


def _experts(h, slot_token, sb_expert, sb_blk0, sb_nsub, n_active, w_gate_up, b_gate_up, w_down, b_down,
             rows, sbk):
    n, d = h.shape
    p = slot_token.shape[0]
    e, _, dff2 = w_gate_up.shape
    dff = dff2 // 2
    tf = _pick_tile(dff, 512, LANES)
    nf = dff // tf
    nblk = p // rows
    nsuper = sb_expert.shape[0]
    tok3 = slot_token.reshape(nblk, 1, rows)

    def frozen_f(s, f, na):
        return jnp.where(s < na[0], f, nf - 1)

    def tok_spec(j, ahead):
        def index(s, f, se, sb0, sn, na):
            si = jnp.minimum(s + ahead, nsuper - 1)
            return (jnp.minimum(sb0[si] + j, nblk - 1), 0, 0)
        return pl.BlockSpec((1, 1, rows), index, memory_space=pltpu.SMEM)

    grid_spec = pltpu.PrefetchScalarGridSpec(
        num_scalar_prefetch=4,
        grid=(nsuper, nf),
        in_specs=[tok_spec(j, 0) for j in range(sbk)] + [tok_spec(j, 1) for j in range(sbk)] + [
            pl.BlockSpec(memory_space=pl.ANY),
            pl.BlockSpec((1, d, tf), lambda s, f, se, sb0, sn, na: (se[s], 0, frozen_f(s, f, na))),
            pl.BlockSpec((1, d, tf), lambda s, f, se, sb0, sn, na: (se[s], 0, nf + frozen_f(s, f, na))),
            pl.BlockSpec((1, tf, d), lambda s, f, se, sb0, sn, na: (se[s], frozen_f(s, f, na), 0)),
            pl.BlockSpec((1, 1, tf), lambda s, f, se, sb0, sn, na: (se[s], 0, frozen_f(s, f, na))),
            pl.BlockSpec((1, 1, tf), lambda s, f, se, sb0, sn, na: (se[s], 0, nf + frozen_f(s, f, na))),
            pl.BlockSpec((1, 1, d), lambda s, f, se, sb0, sn, na: (se[s], 0, 0))],
        out_specs=pl.BlockSpec(memory_space=pl.ANY),
        scratch_shapes=[pltpu.VMEM((sbk * rows, d), F32), pltpu.VMEM((sbk * rows, d), BF16),
                        pltpu.VMEM((sbk * rows, d), F32),
                        pltpu.VMEM((d, tf), BF16), pltpu.VMEM((d, tf), BF16), pltpu.VMEM((tf, d), BF16),
                        pltpu.SemaphoreType.DMA(()), pltpu.SemaphoreType.DMA(())],
    )
    return pl.pallas_call(
        functools.partial(_expert_kernel, rows=rows, sbk=sbk, nf=nf, nsuper=nsuper, nblk=nblk),
        grid_spec=grid_spec,
        out_shape=jax.ShapeDtypeStruct((p, d), F32),
        compiler_params=_params("arbitrary", "arbitrary"),
        name="expert_mlp",
    )(sb_expert, sb_blk0, sb_nsub, n_active, *([tok3] * (2 * sbk)), h, w_gate_up, w_gate_up, w_down,
      b_gate_up.reshape(e, 1, dff2), b_gate_up.reshape(e, 1, dff2), b_down.reshape(e, 1, d))


def _combine_kernel(dest_ref, destn_ref, x_ref, tw_ref, ys_ref, o_ref, buf_ref, sem, *, tm, nt):
    i = pl.program_id(0)
    slot = i % 2

    def gather(d_ref, s):
        def body(t, carry):
            for kk in range(TOP_K):
                _row_copy(ys_ref, d_ref[0, 0, t * TOP_K + kk], buf_ref.at[s, kk], t, sem.at[s]).start()
            return carry
        lax.fori_loop(0, tm, body, 0, unroll=2)

    @pl.when(i == 0)
    def _():
        gather(dest_ref, 0)

    @pl.when(i + 1 < nt)
    def _():
        gather(destn_ref, 1 - slot)

    for kk in range(TOP_K):
        pltpu.make_async_copy(ys_ref.at[pl.ds(0, tm)], buf_ref.at[slot, kk], sem.at[slot]).wait()
    tw = tw_ref[...]
    y = buf_ref[slot, 0] * tw[:, 0:1]
    for kk in range(1, TOP_K):
        y = y + buf_ref[slot, kk] * tw[:, kk:kk + 1]
    o_ref[...] = x_ref[...] + y


def _combine(x1, topw, dest, ys):
    n, d = x1.shape
    tm = _pick_tile(n, 128, 8)
    nt = n // tm
    dest3 = dest.reshape(nt, 1, tm * TOP_K)
    return pl.pallas_call(
        functools.partial(_combine_kernel, tm=tm, nt=nt),
        grid=(nt,),
        in_specs=[pl.BlockSpec((1, 1, tm * TOP_K), lambda i: (i, 0, 0), memory_space=pltpu.SMEM),
                  pl.BlockSpec((1, 1, tm * TOP_K), lambda i: (jnp.minimum(i + 1, nt - 1), 0, 0),
                               memory_space=pltpu.SMEM),
                  pl.BlockSpec((tm, d), lambda i: (i, 0)),
                  pl.BlockSpec((tm, LANES), lambda i: (i, 0)),
                  pl.BlockSpec(memory_space=pl.ANY)],
        out_specs=pl.BlockSpec((tm, d), lambda i: (i, 0)),
        out_shape=jax.ShapeDtypeStruct((n, d), F32),
        scratch_shapes=[pltpu.VMEM((2, TOP_K, tm, d), F32), pltpu.SemaphoreType.DMA((2,))],
        compiler_params=_params("arbitrary"),
        name="moe_combine",
    )(dest3, dest3, x1, topw, ys)


ATTN_BLOCK = 512
GDN_HEADS_PER_STEP = 16
GDN_GROUP = 4
EXPERT_ROWS = 512
EXPERT_SUBBLOCKS = 2


def _layer(x, positions, layer_idx, norm1_g, w_in, b_gate, conv_w, a_log, dt_bias, gdn_norm_g,
           q_norm_g, k_norm_g, lambda_q1, lambda_k1, lambda_q2, lambda_k2, subln_g,
           w_branch_a, w_branch_b, w_out, norm2_g, w_router, b_router,
           w_gate_up, b_gate_up, w_down, b_down):
    b, s, d = x.shape
    n = b * s
    gdn_heads = a_log.shape[0]
    gdn_dim = gdn_heads * HEAD_DIM
    diff_v = w_branch_b.shape[0]
    diff_heads = diff_v // (2 * HEAD_DIM)
    diff_qk = 2 * diff_heads * HEAD_DIM
    off_beta = 4 * gdn_dim
    off_alpha = off_beta + gdn_heads
    off_qb = off_alpha + gdn_heads
    off_gate = off_qb + 2 * diff_qk + diff_v
    assert w_in.shape[1] == off_gate + 2 * d
    n_experts = w_router.shape[1]

    x2 = x.reshape(n, d)
    h = _rmsnorm(x2, norm1_g, BF16)

    proj_a = _matmul(h, w_in, 0, off_beta, F32, name="proj_gdn")
    hg = min(GDN_HEADS_PER_STEP, gdn_heads)
    ngroups = gdn_heads // hg
    cols = []
    for gi in range(ngroups):
        cols += [off_beta + gi * hg + i for i in range(hg)] + [off_alpha + gi * hg + i for i in range(hg)]
        cols += [off_beta] * (LANES - 2 * hg)
    lane_valid = np.tile(np.arange(LANES) < 2 * hg, ngroups)
    w_ba = jnp.where(lane_valid[None, :], w_in[:, np.asarray(cols)], 0.0)
    ba = _matmul(h, w_ba, 0, ngroups * LANES, F32, name="proj_beta_alpha")
    w_b = w_in[:, off_qb:]
    proj_b = _matmul(h, w_b, 0, w_b.shape[1], F32, name="proj_attn_gate")

    alog_p = jnp.zeros((ngroups, 1, LANES), F32).at[:, 0, hg:2 * hg].set(a_log.reshape(ngroups, hg))
    dtb_p = jnp.zeros((ngroups, 1, LANES), F32).at[:, 0, hg:2 * hg].set(dt_bias.reshape(ngroups, hg))
    oa = _gdn(proj_a.reshape(b, s, off_beta), ba.reshape(b, s, ngroups * LANES), conv_w,
              alog_p, dtb_p, gdn_norm_g, gdn_heads, hg)

    pb3 = proj_b.reshape(b, s, proj_b.shape[1])
    pos_f = positions.astype(F32).reshape(b, s, 1)
    qk = _qknorm_rope(pb3, 2 * diff_qk, pos_f, (q_norm_g, k_norm_g),
                      (HEAD_DIM ** -0.5 * math.log2(math.e), 1.0))
    lam_init = 0.8 - 0.6 * math.exp(-0.3 * layer_idx)
    ob = _diff_attn(qk, pb3, 2 * diff_qk // (2 * HEAD_DIM), lambda_q1, lambda_k1, lambda_q2, lambda_k2, subln_g,
                    diff_heads, lam_init)

    merged = _merge(oa.reshape(n, gdn_dim), ob.reshape(n, diff_v), w_branch_a, w_branch_b,
                    proj_b, 2 * diff_qk + diff_v, b_gate, d)
    x1 = _matmul(merged, w_out, 0, d, F32, res=x2, name="out_proj")

    h2, topi, topw, rank, counts = _router(x1, norm2_g, w_router, b_router)
    rows, sbk = EXPERT_ROWS, EXPERT_SUBBLOCKS
    top_e = topi[:, :TOP_K]
    cnt = counts[0, :n_experts]
    nb = (cnt + rows - 1) // rows
    blk_end = jnp.cumsum(nb)
    blk_start = blk_end - nb
    dest = (blk_start[top_e] * rows + rank[:, :TOP_K]).reshape(n * TOP_K)
    p_rows = n * TOP_K + n_experts * rows
    nblk = p_rows // rows
    slot_token = jnp.zeros((p_rows,), jnp.int32).at[dest].set(
        jnp.repeat(jnp.arange(n, dtype=jnp.int32), TOP_K))
    ns = (nb + sbk - 1) // sbk
    ns_end = jnp.cumsum(ns)
    ns_start = ns_end - ns
    n_active = ns_end[-1]
    nsuper = nblk // sbk + n_experts
    s_idx = jnp.arange(nsuper, dtype=jnp.int32)
    active = s_idx < n_active
    s_src = jnp.where(active, s_idx, jnp.maximum(n_active - 1, 0))
    sb_expert = jnp.minimum(jnp.sum(s_src[:, None] >= ns_end[None, :], axis=1), n_experts - 1)
    sb_local = s_src - ns_start[sb_expert]
    sb_blk0 = blk_start[sb_expert] + sb_local * sbk
    sb_nsub = jnp.where(active, jnp.clip(nb[sb_expert] - sb_local * sbk, 0, sbk), 0)
    ys = _experts(h2, slot_token, sb_expert.astype(jnp.int32), sb_blk0.astype(jnp.int32),
                  sb_nsub.astype(jnp.int32), jnp.stack([n_active, blk_end[-1]]).astype(jnp.int32),
                  w_gate_up, b_gate_up, w_down, b_down, rows, sbk)
    out = _combine(x1, topw, dest, ys)
    return out.reshape(b, s, d)


def kernel(x, positions, norm1_g, w_in, b_gate, conv_w, a_log, dt_bias, gdn_norm_g, q_norm_g, k_norm_g, lambda_q1, lambda_k1, lambda_q2, lambda_k2, subln_g, w_branch_a, w_branch_b, w_out, norm2_g, w_router, b_router, w_gate_up, b_gate_up, w_down, b_down):
    depth = norm1_g.shape[0]
    for l in range(depth):
        x = _layer(x, positions, l, norm1_g[l], w_in[l], b_gate[l], conv_w[l], a_log[l], dt_bias[l],
                   gdn_norm_g[l], q_norm_g[l], k_norm_g[l], lambda_q1[l], lambda_k1[l], lambda_q2[l],
                   lambda_k2[l], subln_g[l], w_branch_a[l], w_branch_b[l], w_out[l], norm2_g[l],
                   w_router[l], b_router[l], w_gate_up[l], b_gate_up[l], w_down[l], b_down[l])
    return x
```

```python
import functools
import math

import numpy as np
import jax
import jax.numpy as jnp
from jax import lax
from jax.experimental import pallas as pl
from jax.experimental.pallas import tpu as pltpu

F32 = jnp.float32
BF16 = jnp.bfloat16

NORM_EPS = 1e-6
NEG_INF = -1e30
CHUNK = 64
HEAD_DIM = 128
ROT_DIM = HEAD_DIM // 4
ROPE_THETA = 500000.0
TOP_K = 4
SWIGLU_ALPHA = 1.702
SWIGLU_LIMIT = 7.0
LANES = 128
VMEM_LIMIT_BYTES = 60 * 1024 * 1024


def _pick_tile(n, target, quantum):
    if n <= target:
        return n
    t = (target // quantum) * quantum
    while t > quantum and n % t:
        t -= quantum
    assert n % t == 0, (n, target, quantum)
    return t


def _params(*sem):
    return pltpu.CompilerParams(dimension_semantics=sem, vmem_limit_bytes=VMEM_LIMIT_BYTES)


def _sigmoid(x):
    return 1.0 / (1.0 + jnp.exp(-x))


def _softplus(x):
    return jnp.maximum(x, 0.0) + jnp.log(1.0 + jnp.exp(-jnp.abs(x)))


def _dot(a, b):
    return jnp.dot(a.astype(BF16), b.astype(BF16), preferred_element_type=F32)


def _dot_nt(a, b):
    return lax.dot_general(a.astype(BF16), b.astype(BF16), (((1,), (1,)), ((), ())),
                           preferred_element_type=F32)


def _dot_tn(a, b):
    return lax.dot_general(a.astype(BF16), b.astype(BF16), (((0,), (0,)), ((), ())),
                           preferred_element_type=F32)


def _rmsnorm_kernel(x_ref, g_ref, o_ref):
    x = x_ref[...]
    y = x * lax.rsqrt(jnp.mean(x * x, axis=-1, keepdims=True) + NORM_EPS)
    o_ref[...] = (y * g_ref[...]).astype(o_ref.dtype)


def _rmsnorm(x2d, g, out_dtype):
    n, d = x2d.shape
    tm = _pick_tile(n, 512, 8)
    return pl.pallas_call(
        _rmsnorm_kernel,
        grid=(n // tm,),
        in_specs=[pl.BlockSpec((tm, d), lambda i: (i, 0)), pl.BlockSpec((1, d), lambda i: (0, 0))],
        out_specs=pl.BlockSpec((tm, d), lambda i: (i, 0)),
        out_shape=jax.ShapeDtypeStruct((n, d), out_dtype),
        compiler_params=_params("parallel"),
        name="rmsnorm",
    )(x2d, g.reshape(1, d))


def _matmul_kernel(*refs, has_res):
    if has_res:
        x_ref, w_ref, r_ref, o_ref, wb_ref = refs
    else:
        x_ref, w_ref, o_ref, wb_ref = refs

    @pl.when(pl.program_id(1) == 0)
    def _():
        wb_ref[...] = w_ref[...].astype(BF16)

    acc = jnp.dot(x_ref[...], wb_ref[...], preferred_element_type=F32)
    if has_res:
        acc = acc + r_ref[...]
    o_ref[...] = acc.astype(o_ref.dtype)


def _matmul(x, w, col0, ncols, out_dtype, res=None, name="matmul"):
    m, k = x.shape
    tn = _pick_tile(ncols, 1024, LANES)
    tm = _pick_tile(m, 512, 8)
    assert col0 % tn == 0
    cb = col0 // tn
    in_specs = [pl.BlockSpec((tm, k), lambda j, i: (i, 0)),
                pl.BlockSpec((k, tn), lambda j, i: (0, cb + j))]
    args = [x, w]
    if res is not None:
        in_specs.append(pl.BlockSpec((tm, tn), lambda j, i: (i, j)))
        args.append(res)
    return pl.pallas_call(
        functools.partial(_matmul_kernel, has_res=res is not None),
        grid=(ncols // tn, m // tm),
        in_specs=in_specs,
        out_specs=pl.BlockSpec((tm, tn), lambda j, i: (i, j)),
        out_shape=jax.ShapeDtypeStruct((m, ncols), out_dtype),
        scratch_shapes=[pltpu.VMEM((k, tn), BF16)],
        compiler_params=_params("parallel", "arbitrary"),
        name=name,
    )(*args)


def _gdn_kernel(q_ref, k_ref, v_ref, z_ref, ba_ref, cwq_ref, cwk_ref, cwv_ref, alog_ref, dtb_ref,
                ng_ref, o_ref, state_ref, tq_ref, tk_ref, tv_ref, eq_ref, ek_ref, ev_ref, *, hg):
    C = CHUNK
    D = HEAD_DIM

    @pl.when(pl.program_id(2) == 0)
    def _():
        state_ref[...] = jnp.zeros_like(state_ref)
        tq_ref[...] = jnp.zeros_like(tq_ref)
        tk_ref[...] = jnp.zeros_like(tk_ref)
        tv_ref[...] = jnp.zeros_like(tv_ref)

    def conv_silu(u_ref, tail_ref, ext_ref, w_ref):
        u = u_ref[0]
        ext_ref[0:8, :] = tail_ref[...]
        ext_ref[8:8 + C, :] = u
        w = w_ref[...]
        y = ext_ref[5:5 + C, :] * w[0:1, :]
        y = y + ext_ref[6:6 + C, :] * w[1:2, :]
        y = y + ext_ref[7:7 + C, :] * w[2:3, :]
        y = y + u * w[3:4, :]
        tail_ref[...] = u[C - 8:C, :]
        return y * _sigmoid(y)

    def l2norm_heads(x, scale):
        parts = []
        for i in range(hg):
            xi = x[:, i * D:(i + 1) * D]
            parts.append(xi * (lax.rsqrt(jnp.sum(xi * xi, axis=-1, keepdims=True) + NORM_EPS) * scale))
        return jnp.concatenate(parts, axis=1)

    qc = l2norm_heads(conv_silu(q_ref, tq_ref, eq_ref, cwq_ref), D ** -0.5)
    kc = l2norm_heads(conv_silu(k_ref, tk_ref, ek_ref, cwk_ref), 1.0)
    vc = conv_silu(v_ref, tv_ref, ev_ref, cwv_ref)

    G = GDN_GROUP
    R = G * C
    shift_c = C.bit_length() - 1

    ba = ba_ref[0]
    beta_full = _sigmoid(ba)
    g_full = -jnp.exp(alog_ref[0]) * _softplus(ba + dtb_ref[0])
    r64 = lax.broadcasted_iota(jnp.int32, (C, C), 0)
    c64 = lax.broadcasted_iota(jnp.int32, (C, C), 1)
    gcum_full = jnp.dot((r64 >= c64).astype(F32), g_full, preferred_element_type=F32,
                        precision=lax.Precision.HIGHEST)
    gcum_t = gcum_full.T
    row = lax.broadcasted_iota(jnp.int32, (R, R), 0)
    col = lax.broadcasted_iota(jnp.int32, (R, R), 1)
    same_head = jnp.right_shift(row, shift_c) == jnp.right_shift(col, shift_c)
    tril = same_head & (row >= col)
    strict = same_head & (row > col)
    lane2 = lax.broadcasted_iota(jnp.int32, (1, 2 * D), 1)
    r2 = lax.broadcasted_iota(jnp.int32, (2 * D, 2 * D), 0)
    c2 = lax.broadcasted_iota(jnp.int32, (2 * D, 2 * D), 1)
    pair_block = (r2 < D) == (c2 < D)

    def lanes_of(x, i0):
        return jnp.concatenate([x[i0 * C:(i0 + 1) * C], x[(i0 + 1) * C:(i0 + 2) * C]], axis=1)

    for h0 in range(0, hg, G):
        heads = range(h0, h0 + G)

        def rows_of(x):
            return jnp.concatenate([x[:, i * D:(i + 1) * D] for i in heads], axis=0)

        beta_r = jnp.concatenate([beta_full[:, i:i + 1] for i in heads], axis=0)
        gc_r = jnp.concatenate([gcum_full[:, hg + i:hg + i + 1] for i in heads], axis=0)
        gc_l = jnp.concatenate([gcum_t[hg + i:hg + i + 1, :] for i in heads], axis=1)
        g_last = [gcum_full[C - 1:C, hg + i:hg + i + 1] for i in heads]
        gl_r = jnp.concatenate([jnp.broadcast_to(g, (C, 1)) for g in g_last], axis=0)

        qn, kn, v_r = rows_of(qc), rows_of(kc), rows_of(vc)
        decay = jnp.where(tril, jnp.exp(jnp.where(tril, gc_r - gc_l, 0.0)), 0.0)
        eg = jnp.exp(gc_r)
        kb = kn * beta_r
        gram = _dot_nt(jnp.concatenate([kb, qn], axis=0), kn)
        nmat = jnp.where(strict, gram[:R] * decay, 0.0)
        qk = gram[R:] * decay
        kbg_r = kb * eg
        qg_r = qn * eg
        kdec_r = kn * jnp.exp(gl_r - gc_r)
        states, ks_rows, o1_rows = [], [], []
        for pr in range(G // 2):
            s = state_ref[h0 // 2 + pr]
            xs = _dot(jnp.concatenate([lanes_of(kbg_r, 2 * pr), lanes_of(qg_r, 2 * pr)], axis=0), s)
            states.append(s)
            ks_rows += [xs[:C, :D], xs[:C, D:]]
            o1_rows += [xs[C:, :D], xs[C:, D:]]
        y = v_r * beta_r - jnp.concatenate(ks_rows, axis=0)
        m = -nmat
        for _ in range(shift_c - 1):
            mb = m.astype(BF16)
            x = jnp.dot(mb, jnp.concatenate([mb, y.astype(BF16)], axis=1), preferred_element_type=F32)
            m = x[:, :R]
            y = y + x[:, R:]
        vnew_r = y + _dot(m, y)
        for pr in range(G // 2):
            upd = _dot_tn(lanes_of(kdec_r, 2 * pr), lanes_of(vnew_r, 2 * pr))
            gl_lanes = jnp.where(lane2 < D, jnp.exp(g_last[2 * pr]), jnp.exp(g_last[2 * pr + 1]))
            state_ref[h0 // 2 + pr] = states[pr] * gl_lanes + jnp.where(pair_block, upd, 0.0)
        o = jnp.concatenate(o1_rows, axis=0) + _dot(qk, vnew_r)
        on = o * lax.rsqrt(jnp.mean(o * o, axis=-1, keepdims=True) + NORM_EPS) * ng_ref[...]
        z_r = rows_of(z_ref[0])
        out = (on * (z_r * _sigmoid(z_r))).astype(o_ref.dtype)
        for j, i in enumerate(heads):
            o_ref[0, :, i * D:(i + 1) * D] = out[j * C:(j + 1) * C]


def _gdn(proj_a, ba, conv_w, alog_p, dtb_p, norm_g, heads, hg):
    b, s, _ = proj_a.shape
    assert hg % 2 == 0 and heads % hg == 0
    ngroups = heads // hg
    w = hg * HEAD_DIM
    nc = s // CHUNK

    def colspec(off):
        return pl.BlockSpec((1, CHUNK, w), lambda bi, gi, ci, off=off: (bi, ci, off * ngroups + gi))

    def cwspec(off):
        return pl.BlockSpec((4, w), lambda bi, gi, ci, off=off: (0, off * ngroups + gi))

    vec = pl.BlockSpec((1, 1, LANES), lambda bi, gi, ci: (gi, 0, 0))
    return pl.pallas_call(
        functools.partial(_gdn_kernel, hg=hg),
        grid=(b, ngroups, nc),
        in_specs=[colspec(0), colspec(1), colspec(2), colspec(3),
                  pl.BlockSpec((1, CHUNK, LANES), lambda bi, gi, ci: (bi, ci, gi)),
                  cwspec(0), cwspec(1), cwspec(2), vec, vec,
                  pl.BlockSpec((1, HEAD_DIM), lambda bi, gi, ci: (0, 0))],
        out_specs=pl.BlockSpec((1, CHUNK, w), lambda bi, gi, ci: (bi, ci, gi)),
        out_shape=jax.ShapeDtypeStruct((b, s, heads * HEAD_DIM), BF16),
        scratch_shapes=[pltpu.VMEM((hg // 2, 2 * HEAD_DIM, 2 * HEAD_DIM), F32)]
        + [pltpu.VMEM((8, w), F32)] * 3 + [pltpu.VMEM((8 + CHUNK, w), F32)] * 3,
        compiler_params=_params("parallel", "parallel", "arbitrary"),
        name="gated_delta_rule",
    )(proj_a, proj_a, proj_a, proj_a, ba, conv_w, conv_w, conv_w, alog_p, dtb_p,
      norm_g.reshape(1, HEAD_DIM))


def _rope_table_kernel(pos_ref, freq_ref, cos_ref, sin_ref):
    ang = pos_ref[...] * freq_ref[...]
    lane = lax.broadcasted_iota(jnp.int32, ang.shape, 1)
    sin = jnp.sin(ang)
    cos_ref[...] = jnp.cos(ang)
    sin_ref[...] = jnp.where(lane < ROT_DIM // 2, -sin, sin)


def _rope_table(pos_f):
    n = pos_f.shape[0]
    ts = _pick_tile(n, 1024, 8)
    half = ROT_DIM // 2
    inv_freq = np.power(np.float32(ROPE_THETA),
                        -np.arange(half, dtype=np.float32) * np.float32(2.0) / np.float32(ROT_DIM))
    freq = np.zeros((1, LANES), np.float32)
    freq[0, :half] = inv_freq
    freq[0, half:ROT_DIM] = inv_freq
    row = pl.BlockSpec((ts, LANES), lambda i: (i, 0))
    return pl.pallas_call(
        _rope_table_kernel,
        grid=(n // ts,),
        in_specs=[pl.BlockSpec((ts, 1), lambda i: (i, 0)), pl.BlockSpec((1, LANES), lambda i: (0, 0))],
        out_specs=[row, row],
        out_shape=[jax.ShapeDtypeStruct((n, LANES), F32)] * 2,
        compiler_params=_params("parallel"),
        name="rope_table",
    )(pos_f, jnp.asarray(freq))


def _proj_shift_kernel(*refs, shift, rope, q_tiles, scales):
    if rope:
        x_ref, w_ref, w2_ref, cos_ref, sin_ref, g_ref, o_ref, wb_ref = refs
    else:
        x_ref, w_ref, w2_ref, o_ref, wb_ref = refs
    tn = wb_ref.shape[1]

    @pl.when(pl.program_id(1) == 0)
    def _():
        if shift:
            wb_ref[:, :tn - shift] = w_ref[:, shift:].astype(BF16)
            wb_ref[:, tn - shift:] = w2_ref[:, :shift].astype(BF16)
        else:
            wb_ref[...] = w_ref[...].astype(BF16)

    acc = jnp.dot(x_ref[...], wb_ref[...], preferred_element_type=F32)
    if not rope:
        o_ref[...] = acc.astype(o_ref.dtype)
        return
    is_q = pl.program_id(0) < q_tiles
    g = jnp.where(is_q, g_ref[0:1, :], g_ref[1:2, :])
    scale = jnp.where(is_q, scales[0], scales[1])
    cos = cos_ref[...]
    sin_signed = sin_ref[...]
    lane = lax.broadcasted_iota(jnp.int32, cos.shape, 1)
    first_half = lane < ROT_DIM // 2
    for j in range(tn // HEAD_DIM):
        sl = slice(j * HEAD_DIM, (j + 1) * HEAD_DIM)
        x = acc[:, sl]
        xn = x * (lax.rsqrt(jnp.mean(x * x, axis=-1, keepdims=True) + NORM_EPS) * scale) * g
        partner = jnp.where(first_half, pltpu.roll(xn, HEAD_DIM - ROT_DIM // 2, 1),
                            pltpu.roll(xn, ROT_DIM // 2, 1))
        o_ref[:, sl] = (xn * cos + partner * sin_signed).astype(o_ref.dtype)


def _proj_shift(x, w, col0, ncols, out_dtype, rope=None, name="proj_shift"):
    m, k = x.shape
    shift = col0 % LANES
    tn = _pick_tile(rope[4] if rope else ncols, 1024, LANES)
    tm = _pick_tile(m, 512, 8)
    assert (col0 - shift) % tn == 0 and ncols % tn == 0
    cb = (col0 - shift) // tn
    per = tn // LANES
    in_specs = [pl.BlockSpec((tm, k), lambda j, i: (i, 0)),
                pl.BlockSpec((k, tn), lambda j, i: (0, cb + j)),
                pl.BlockSpec((k, LANES), lambda j, i: (0, (cb + j + 1) * per))]
    args = [x, w, w]
    q_tiles, scales = 0, (1.0, 1.0)
    if rope:
        cos, sin_signed, gains, scales, q_cols = rope
        q_tiles = q_cols // tn
        in_specs += [pl.BlockSpec((tm, LANES), lambda j, i: (i, 0)),
                     pl.BlockSpec((tm, LANES), lambda j, i: (i, 0)),
                     pl.BlockSpec((2, HEAD_DIM), lambda j, i: (0, 0))]
        args += [cos, sin_signed, gains]
    return pl.pallas_call(
        functools.partial(_proj_shift_kernel, shift=shift, rope=rope is not None, q_tiles=q_tiles,
                          scales=tuple(scales)),
        grid=(ncols // tn, m // tm),
        in_specs=in_specs,
        out_specs=pl.BlockSpec((tm, tn), lambda j, i: (i, j)),
        out_shape=jax.ShapeDtypeStruct((m, ncols), out_dtype),
        scratch_shapes=[pltpu.VMEM((k, tn), BF16)],
        compiler_params=_params("parallel", "arbitrary"),
        name=name,
    )(*args)


def _diff_attn_kernel(q_ref, k_ref, v_ref, lq1_ref, lk1_ref, lq2_ref, lk2_ref, sg_ref, o_ref,
                      acc_ref, m_ref, l_ref, *, tq, lam_init):
    D = HEAD_DIM
    qi = pl.program_id(2)
    q1 = q_ref[0, :, :D]
    q2 = q_ref[0, :, D:]
    acc_ref[...] = jnp.zeros_like(acc_ref)
    m_ref[...] = jnp.full_like(m_ref, NEG_INF)
    l_ref[...] = jnp.zeros_like(l_ref)

    def lanes(x, width):
        return jnp.concatenate([x] * (width // LANES), axis=1)

    def step(start, mask):
        kblk = k_ref[0, pl.ds(start, tq), :]
        vblk = v_ref[0, pl.ds(start, tq), :].astype(BF16)
        s = jnp.concatenate([_dot_nt(q1, kblk[:, :D]), _dot_nt(q2, kblk[:, D:])], axis=0)
        if mask is not None:
            s = jnp.where(mask, s, NEG_INF)
        m_prev = m_ref[...]
        m_next = jnp.maximum(m_prev, jnp.max(s, axis=-1, keepdims=True))
        alpha = jnp.exp2(m_prev - m_next)
        p = jnp.exp2(s - lanes(m_next, tq))
        l_ref[...] = alpha * l_ref[...] + jnp.sum(p, axis=-1, keepdims=True)
        acc_ref[...] = lanes(alpha, 2 * D) * acc_ref[...] + jnp.dot(p.astype(BF16), vblk,
                                                                    preferred_element_type=F32)
        m_ref[...] = m_next

    def body(j, carry):
        step(pl.multiple_of(j * tq, tq), None)
        return carry

    lax.fori_loop(0, qi, body, 0)
    shift_c = CHUNK.bit_length() - 1
    row = lax.broadcasted_iota(jnp.int32, (2 * tq, tq), 0)
    row = jnp.right_shift(jnp.where(row >= tq, row - tq, row), shift_c)
    col = jnp.right_shift(lax.broadcasted_iota(jnp.int32, (2 * tq, tq), 1), shift_c)
    step(pl.multiple_of(qi * tq, tq), row >= col)

    lam = (jnp.exp(jnp.sum(lq1_ref[...] * lk1_ref[...], axis=-1, keepdims=True))
           - jnp.exp(jnp.sum(lq2_ref[...] * lk2_ref[...], axis=-1, keepdims=True)) + lam_init)
    a = acc_ref[...] / lanes(l_ref[...], 2 * D)
    o = a[:tq] - lam * a[tq:]
    on = o * lax.rsqrt(jnp.mean(o * o, axis=-1, keepdims=True) + NORM_EPS)
    o_ref[0] = (on * sg_ref[...] * (1.0 - lam_init)).astype(o_ref.dtype)


def _diff_attn(qk, v_src, v_col_block, lq1, lk1, lq2, lk2, subln_g, heads, lam_init):
    b, s, _ = qk.shape
    hw = 2 * HEAD_DIM
    tq = _pick_tile(s, ATTN_BLOCK, CHUNK)
    vec = pl.BlockSpec((1, HEAD_DIM), lambda bi, hi, qi: (0, 0))
    return pl.pallas_call(
        functools.partial(_diff_attn_kernel, tq=tq, lam_init=lam_init),
        grid=(b, heads, s // tq),
        in_specs=[pl.BlockSpec((1, tq, hw), lambda bi, hi, qi: (bi, qi, hi)),
                  pl.BlockSpec((1, s, hw), lambda bi, hi, qi: (bi, 0, heads + hi)),
                  pl.BlockSpec((1, s, hw), lambda bi, hi, qi: (bi, 0, v_col_block + hi)),
                  vec, vec, vec, vec,
                  pl.BlockSpec((1, hw), lambda bi, hi, qi: (0, 0))],
        out_specs=pl.BlockSpec((1, tq, hw), lambda bi, hi, qi: (bi, qi, hi)),
        out_shape=jax.ShapeDtypeStruct((b, s, heads * hw), BF16),
        scratch_shapes=[pltpu.VMEM((2 * tq, hw), F32)] + [pltpu.VMEM((2 * tq, LANES), F32)] * 2,
        compiler_params=_params("parallel", "parallel", "arbitrary"),
        name="diff_attention",
    )(qk, qk, v_src, lq1.reshape(1, -1), lk1.reshape(1, -1), lq2.reshape(1, -1), lk2.reshape(1, -1),
      subln_g.reshape(1, hw))


def _merge_kernel(oa_ref, ob_ref, wa_ref, wb_ref, ga_ref, gb_ref, ba_ref, bb_ref, o_ref, wa_s, wb_s):
    @pl.when(pl.program_id(1) == 0)
    def _():
        wa_s[...] = wa_ref[...].astype(BF16)
        wb_s[...] = wb_ref[...].astype(BF16)

    ya = jnp.dot(oa_ref[...], wa_s[...], preferred_element_type=F32)
    yb = jnp.dot(ob_ref[...], wb_s[...], preferred_element_type=F32)
    merged = _sigmoid(ga_ref[...] + ba_ref[...]) * ya + _sigmoid(gb_ref[...] + bb_ref[...]) * yb
    o_ref[...] = merged.astype(o_ref.dtype)


def _merge(oa, ob, w_a, w_b, proj_b, gate_col0, b_gate, d):
    m, ka = oa.shape
    kb = ob.shape[1]
    tn = _pick_tile(d, 512, LANES)
    tm = _pick_tile(m, 512, 8)
    assert gate_col0 % tn == 0
    ga = gate_col0 // tn
    gb = (gate_col0 + d) // tn
    nb = d // tn
    bg = b_gate.reshape(1, 2 * d)
    return pl.pallas_call(
        _merge_kernel,
        grid=(nb, m // tm),
        in_specs=[pl.BlockSpec((tm, ka), lambda j, i: (i, 0)),
                  pl.BlockSpec((tm, kb), lambda j, i: (i, 0)),
                  pl.BlockSpec((ka, tn), lambda j, i: (0, j)),
                  pl.BlockSpec((kb, tn), lambda j, i: (0, j)),
                  pl.BlockSpec((tm, tn), lambda j, i: (i, ga + j)),
                  pl.BlockSpec((tm, tn), lambda j, i: (i, gb + j)),
                  pl.BlockSpec((1, tn), lambda j, i: (0, j)),
                  pl.BlockSpec((1, tn), lambda j, i: (0, nb + j))],
        out_specs=pl.BlockSpec((tm, tn), lambda j, i: (i, j)),
        out_shape=jax.ShapeDtypeStruct((m, d), BF16),
        scratch_shapes=[pltpu.VMEM((ka, tn), BF16), pltpu.VMEM((kb, tn), BF16)],
        compiler_params=_params("parallel", "arbitrary"),
        name="branch_merge",
    )(oa, ob, w_a, w_b, proj_b, proj_b, bg, bg)


def _router_kernel(x_ref, g_ref, wr_ref, br_ref, h_ref, ti_ref, tw_ref, rk_ref, cnt_ref, carry_ref,
                   *, n_experts):
    @pl.when(pl.program_id(0) == 0)
    def _():
        carry_ref[...] = jnp.zeros_like(carry_ref)

    x = x_ref[...]
    h = x * lax.rsqrt(jnp.mean(x * x, axis=-1, keepdims=True) + NORM_EPS) * g_ref[...]
    h_ref[...] = h
    logits = jnp.dot(h, wr_ref[...], preferred_element_type=F32,
                     precision=lax.Precision.HIGHEST) + br_ref[...]
    tm = x.shape[0]
    lane = lax.broadcasted_iota(jnp.int32, (tm, LANES), 1)
    lane_f = lane.astype(F32)
    cur = jnp.where(lane < n_experts, logits, -jnp.inf)
    vals, hots = [], []
    ti = jnp.zeros((tm, LANES), F32)
    for kk in range(TOP_K):
        mx = jnp.max(cur, axis=-1, keepdims=True)
        idx = jnp.min(jnp.where(cur == mx, lane_f, float(LANES)), axis=-1, keepdims=True)
        hot = lane_f == idx
        vals.append(mx)
        hots.append(hot)
        ti = jnp.where(lane == kk, idx, ti)
        cur = jnp.where(hot, -jnp.inf, cur)
    exps = [jnp.exp(v - vals[0]) for v in vals]
    denom = exps[0] + exps[1] + exps[2] + exps[3]
    tw = jnp.zeros((tm, LANES), F32)
    for kk in range(TOP_K):
        tw = jnp.where(lane == kk, exps[kk] / denom, tw)
    multihot = jnp.zeros((tm, LANES), F32)
    for hot in hots:
        multihot = multihot + hot.astype(F32)
    r = lax.broadcasted_iota(jnp.int32, (tm, tm), 0)
    c = lax.broadcasted_iota(jnp.int32, (tm, tm), 1)
    prefix = jnp.dot((r > c).astype(BF16), multihot.astype(BF16), preferred_element_type=F32)
    prefix = prefix + carry_ref[...]
    rk = jnp.zeros((tm, LANES), F32)
    for kk in range(TOP_K):
        rk_k = jnp.sum(jnp.where(hots[kk], prefix, 0.0), axis=-1, keepdims=True)
        rk = jnp.where(lane == kk, rk_k, rk)
    carry = carry_ref[...] + jnp.sum(multihot, axis=0, keepdims=True)
    carry_ref[...] = carry
    cnt_ref[...] = carry.astype(jnp.int32)
    ti_ref[...] = ti.astype(jnp.int32)
    tw_ref[...] = tw
    rk_ref[...] = rk.astype(jnp.int32)


def _router(x1, g, w_router, b_router):
    n, d = x1.shape
    e = w_router.shape[1]
    tm = _pick_tile(n, 256, 8)
    wr = jnp.zeros((d, LANES), F32).at[:, :e].set(w_router)
    br = jnp.zeros((1, LANES), F32).at[0, :e].set(b_router)
    row = pl.BlockSpec((tm, LANES), lambda i: (i, 0))
    return pl.pallas_call(
        functools.partial(_router_kernel, n_experts=e),
        grid=(n // tm,),
        in_specs=[pl.BlockSpec((tm, d), lambda i: (i, 0)),
                  pl.BlockSpec((1, d), lambda i: (0, 0)),
                  pl.BlockSpec((d, LANES), lambda i: (0, 0)),
                  pl.BlockSpec((1, LANES), lambda i: (0, 0))],
        out_specs=[pl.BlockSpec((tm, d), lambda i: (i, 0)), row, row, row,
                   pl.BlockSpec((1, LANES), lambda i: (0, 0))],
        out_shape=[jax.ShapeDtypeStruct((n, d), F32),
                   jax.ShapeDtypeStruct((n, LANES), jnp.int32),
                   jax.ShapeDtypeStruct((n, LANES), F32),
                   jax.ShapeDtypeStruct((n, LANES), jnp.int32),
                   jax.ShapeDtypeStruct((1, LANES), jnp.int32)],
        scratch_shapes=[pltpu.VMEM((1, LANES), F32)],
        compiler_params=_params("arbitrary"),
        name="router_topk",
    )(x1, g.reshape(1, d), wr, br)


def _row_copy(src_ref, src_row, dst_ref, dst_row, sem):
    return pltpu.make_async_copy(src_ref.at[pl.ds(src_row, 1)], dst_ref.at[pl.ds(dst_row, 1)], sem)


def _expert_kernel(se_ref, sb0_ref, sn_ref, na_ref, *refs, rows, sbk, nf, nsuper, nblk):
    tok_refs = refs[:sbk]
    tokn_refs = refs[sbk:2 * sbk]
    (h_ref, wg_ref, wu_ref, wd_ref, bg_ref, bu_ref, bd_ref, ys_ref,
     xg_ref, xb_ref, acc_ref, wgb_ref, wub_ref, wdb_ref, gsem, osem) = refs[2 * sbk:]
    s = pl.program_id(0)
    f = pl.program_id(1)
    nsub = sn_ref[s]

    def gather(t_refs, count):
        for j in range(sbk):
            @pl.when(j < count)
            def _(j=j):
                def body(r, carry):
                    _row_copy(h_ref, t_refs[j][0, 0, r], xg_ref, j * rows + r, gsem).start()
                    return carry
                lax.fori_loop(0, rows, body, 0, unroll=8)

    def out_copy(j, blk0):
        return pltpu.make_async_copy(acc_ref.at[pl.ds(j * rows, rows)],
                                     ys_ref.at[pl.ds((blk0 + j) * rows, rows)], osem)

    n_act = na_ref[0]
    blocks_used = na_ref[1]

    def tail_block(j):
        return blocks_used + (s - n_act) * sbk + j

    def zero_copy(j):
        return pltpu.make_async_copy(acc_ref.at[pl.ds(0, rows)],
                                     ys_ref.at[pl.ds(tail_block(j) * rows, rows)], osem)

    @pl.when((f == 0) & (s == 0))
    def _():
        gather(tok_refs, nsub)

    @pl.when(f == 0)
    def _():
        for j in range(sbk):
            @pl.when(j < nsub)
            def _(j=j):
                pltpu.make_async_copy(h_ref.at[pl.ds(0, rows)], xg_ref.at[pl.ds(j * rows, rows)], gsem).wait()
        for j in range(sbk):
            @pl.when(j < nsub)
            def _(j=j):
                xb_ref[pl.ds(j * rows, rows), :] = xg_ref[pl.ds(j * rows, rows), :].astype(BF16)

        @pl.when(s + 1 < nsuper)
        def _():
            gather(tokn_refs, sn_ref[s + 1])

        @pl.when(s > 0)
        def _():
            for j in range(sbk):
                @pl.when(j < sn_ref[s - 1])
                def _(j=j):
                    out_copy(j, sb0_ref[s - 1]).wait()

        for j in range(sbk):
            @pl.when(j < nsub)
            def _(j=j):
                acc_ref[pl.ds(j * rows, rows), :] = jnp.broadcast_to(bd_ref[0], (rows, acc_ref.shape[1]))

        @pl.when(s == n_act)
        def _():
            acc_ref[pl.ds(0, rows), :] = jnp.zeros((rows, acc_ref.shape[1]), F32)

        for j in range(sbk):
            @pl.when((s >= n_act) & (tail_block(j) < nblk))
            def _(j=j):
                zero_copy(j).start()

    @pl.when(nsub > 0)
    def _():
        wgb_ref[...] = wg_ref[0].astype(BF16)
        wub_ref[...] = wu_ref[0].astype(BF16)
        wdb_ref[...] = wd_ref[0].astype(BF16)

    for j in range(sbk):
        @pl.when(j < nsub)
        def _(j=j):
            xb = xb_ref[pl.ds(j * rows, rows), :]
            g = jnp.dot(xb, wgb_ref[...], preferred_element_type=F32) + bg_ref[0]
            u = jnp.dot(xb, wub_ref[...], preferred_element_type=F32) + bu_ref[0]
            gate = jnp.minimum(g, SWIGLU_LIMIT)
            up = jnp.clip(u, -SWIGLU_LIMIT, SWIGLU_LIMIT)
            act = (up + 1.0) * gate * _sigmoid(SWIGLU_ALPHA * gate)
            acc_ref[pl.ds(j * rows, rows), :] += jnp.dot(act.astype(BF16), wdb_ref[...],
                                                         preferred_element_type=F32)

    @pl.when(f == nf - 1)
    def _():
        for j in range(sbk):
            @pl.when((s >= n_act) & (tail_block(j) < nblk))
            def _(j=j):
                zero_copy(j).wait()

        for j in range(sbk):
            @pl.when(j < nsub)
            def _(j=j):
                out_copy(j, sb0_ref[s]).start()

        @pl.when(s == nsuper - 1)
        def _():
            for j in range(sbk):
                @pl.when(j < nsub)
                def _(j=j):
                    out_copy(j, sb0_ref[s]).wait()


def _experts(h, slot_token, sb_expert, sb_blk0, sb_nsub, n_active, w_gate_up, b_gate_up, w_down, b_down,
             rows, sbk):
    n, d = h.shape
    p = slot_token.shape[0]
    e, _, dff2 = w_gate_up.shape
    dff = dff2 // 2
    tf = _pick_tile(dff, 512, LANES)
    nf = dff // tf
    nblk = p // rows
    nsuper = sb_expert.shape[0]
    tok3 = slot_token.reshape(nblk, 1, rows)

    def frozen_f(s, f, na):
        return jnp.where(s < na[0], f, nf - 1)

    def tok_spec(j, ahead):
        def index(s, f, se, sb0, sn, na):
            si = jnp.minimum(s + ahead, nsuper - 1)
            return (jnp.minimum(sb0[si] + j, nblk - 1), 0, 0)
        return pl.BlockSpec((1, 1, rows), index, memory_space=pltpu.SMEM)

    grid_spec = pltpu.PrefetchScalarGridSpec(
        num_scalar_prefetch=4,
        grid=(nsuper, nf),
        in_specs=[tok_spec(j, 0) for j in range(sbk)] + [tok_spec(j, 1) for j in range(sbk)] + [
            pl.BlockSpec(memory_space=pl.ANY),
            pl.BlockSpec((1, d, tf), lambda s, f, se, sb0, sn, na: (se[s], 0, frozen_f(s, f, na))),
            pl.BlockSpec((1, d, tf), lambda s, f, se, sb0, sn, na: (se[s], 0, nf + frozen_f(s, f, na))),
            pl.BlockSpec((1, tf, d), lambda s, f, se, sb0, sn, na: (se[s], frozen_f(s, f, na), 0)),
            pl.BlockSpec((1, 1, tf), lambda s, f, se, sb0, sn, na: (se[s], 0, frozen_f(s, f, na))),
            pl.BlockSpec((1, 1, tf), lambda s, f, se, sb0, sn, na: (se[s], 0, nf + frozen_f(s, f, na))),
            pl.BlockSpec((1, 1, d), lambda s, f, se, sb0, sn, na: (se[s], 0, 0))],
        out_specs=pl.BlockSpec(memory_space=pl.ANY),
        scratch_shapes=[pltpu.VMEM((sbk * rows, d), F32), pltpu.VMEM((sbk * rows, d), BF16),
                        pltpu.VMEM((sbk * rows, d), F32),
                        pltpu.VMEM((d, tf), BF16), pltpu.VMEM((d, tf), BF16), pltpu.VMEM((tf, d), BF16),
                        pltpu.SemaphoreType.DMA(()), pltpu.SemaphoreType.DMA(())],
    )
    return pl.pallas_call(
        functools.partial(_expert_kernel, rows=rows, sbk=sbk, nf=nf, nsuper=nsuper, nblk=nblk),
        grid_spec=grid_spec,
        out_shape=jax.ShapeDtypeStruct((p, d), F32),
        compiler_params=_params("arbitrary", "arbitrary"),
        name="expert_mlp",
    )(sb_expert, sb_blk0, sb_nsub, n_active, *([tok3] * (2 * sbk)), h, w_gate_up, w_gate_up, w_down,
      b_gate_up.reshape(e, 1, dff2), b_gate_up.reshape(e, 1, dff2), b_down.reshape(e, 1, d))


def _combine_kernel(dest_ref, destn_ref, x_ref, tw_ref, ys_ref, o_ref, buf_ref, sem, *, tm, nt):
    i = pl.program_id(0)
    slot = i % 2

    def gather(d_ref, s):
        def body(t, carry):
            for kk in range(TOP_K):
                _row_copy(ys_ref, d_ref[0, 0, t * TOP_K + kk], buf_ref.at[s, kk], t, sem.at[s]).start()
            return carry
        lax.fori_loop(0, tm, body, 0, unroll=2)

    @pl.when(i == 0)
    def _():
        gather(dest_ref, 0)

    @pl.when(i + 1 < nt)
    def _():
        gather(destn_ref, 1 - slot)

    for kk in range(TOP_K):
        pltpu.make_async_copy(ys_ref.at[pl.ds(0, tm)], buf_ref.at[slot, kk], sem.at[slot]).wait()
    tw = tw_ref[...]
    y = buf_ref[slot, 0] * tw[:, 0:1]
    for kk in range(1, TOP_K):
        y = y + buf_ref[slot, kk] * tw[:, kk:kk + 1]
    o_ref[...] = x_ref[...] + y


def _combine(x1, topw, dest, ys):
    n, d = x1.shape
    tm = _pick_tile(n, 128, 8)
    nt = n // tm
    dest3 = dest.reshape(nt, 1, tm * TOP_K)
    return pl.pallas_call(
        functools.partial(_combine_kernel, tm=tm, nt=nt),
        grid=(nt,),
        in_specs=[pl.BlockSpec((1, 1, tm * TOP_K), lambda i: (i, 0, 0), memory_space=pltpu.SMEM),
                  pl.BlockSpec((1, 1, tm * TOP_K), lambda i: (jnp.minimum(i + 1, nt - 1), 0, 0),
                               memory_space=pltpu.SMEM),
                  pl.BlockSpec((tm, d), lambda i: (i, 0)),
                  pl.BlockSpec((tm, LANES), lambda i: (i, 0)),
                  pl.BlockSpec(memory_space=pl.ANY)],
        out_specs=pl.BlockSpec((tm, d), lambda i: (i, 0)),
        out_shape=jax.ShapeDtypeStruct((n, d), F32),
        scratch_shapes=[pltpu.VMEM((2, TOP_K, tm, d), F32), pltpu.SemaphoreType.DMA((2,))],
        compiler_params=_params("arbitrary"),
        name="moe_combine",
    )(dest3, dest3, x1, topw, ys)


ATTN_BLOCK = 512
GDN_HEADS_PER_STEP = 16
GDN_GROUP = 4
EXPERT_ROWS = 512
EXPERT_SUBBLOCKS = 2


def _layer(x, positions, layer_idx, norm1_g, w_in, b_gate, conv_w, a_log, dt_bias, gdn_norm_g,
           q_norm_g, k_norm_g, lambda_q1, lambda_k1, lambda_q2, lambda_k2, subln_g,
           w_branch_a, w_branch_b, w_out, norm2_g, w_router, b_router,
           w_gate_up, b_gate_up, w_down, b_down):
    b, s, d = x.shape
    n = b * s
    gdn_heads = a_log.shape[0]
    gdn_dim = gdn_heads * HEAD_DIM
    diff_v = w_branch_b.shape[0]
    diff_heads = diff_v // (2 * HEAD_DIM)
    diff_qk = 2 * diff_heads * HEAD_DIM
    off_beta = 4 * gdn_dim
    off_alpha = off_beta + gdn_heads
    off_qb = off_alpha + gdn_heads
    off_gate = off_qb + 2 * diff_qk + diff_v
    assert w_in.shape[1] == off_gate + 2 * d
    n_experts = w_router.shape[1]

    x2 = x.reshape(n, d)
    h = _rmsnorm(x2, norm1_g, BF16)

    proj_a = _matmul(h, w_in, 0, off_beta, F32, name="proj_gdn")
    hg = min(GDN_HEADS_PER_STEP, gdn_heads)
    ngroups = gdn_heads // hg
    cols = []
    for gi in range(ngroups):
        cols += [off_beta + gi * hg + i for i in range(hg)] + [off_alpha + gi * hg + i for i in range(hg)]
        cols += [off_beta] * (LANES - 2 * hg)
    lane_valid = np.tile(np.arange(LANES) < 2 * hg, ngroups)
    w_ba = jnp.where(lane_valid[None, :], w_in[:, np.asarray(cols)], 0.0)
    ba = _matmul(h, w_ba, 0, ngroups * LANES, F32, name="proj_beta_alpha")
    cos_t, sin_t = _rope_table(positions.astype(F32).reshape(n, 1))
    qk = _proj_shift(h, w_in, off_qb, 2 * diff_qk, BF16, name="proj_qk_rope",
                     rope=(cos_t, sin_t, jnp.stack([q_norm_g, k_norm_g]),
                           (HEAD_DIM ** -0.5 * math.log2(math.e), 1.0), diff_qk))
    vg = _proj_shift(h, w_in, off_qb + 2 * diff_qk, diff_v + 2 * d, F32, name="proj_v_gate")

    alog_p = jnp.zeros((ngroups, 1, LANES), F32).at[:, 0, hg:2 * hg].set(a_log.reshape(ngroups, hg))
    dtb_p = jnp.zeros((ngroups, 1, LANES), F32).at[:, 0, hg:2 * hg].set(dt_bias.reshape(ngroups, hg))
    oa = _gdn(proj_a.reshape(b, s, off_beta), ba.reshape(b, s, ngroups * LANES), conv_w,
              alog_p, dtb_p, gdn_norm_g, gdn_heads, hg)

    lam_init = 0.8 - 0.6 * math.exp(-0.3 * layer_idx)
    ob = _diff_attn(qk.reshape(b, s, 2 * diff_qk), vg.reshape(b, s, diff_v + 2 * d), 0,
                    lambda_q1, lambda_k1, lambda_q2, lambda_k2, subln_g, diff_heads, lam_init)

    merged = _merge(oa.reshape(n, gdn_dim), ob.reshape(n, diff_v), w_branch_a, w_branch_b,
                    vg, diff_v, b_gate, d)
    x1 = _matmul(merged, w_out, 0, d, F32, res=x2, name="out_proj")

    h2, topi, topw, rank, counts = _router(x1, norm2_g, w_router, b_router)
    rows, sbk = EXPERT_ROWS, EXPERT_SUBBLOCKS
    top_e = topi[:, :TOP_K]
    cnt = counts[0, :n_experts]
    nb = (cnt + rows - 1) // rows
    blk_end = jnp.cumsum(nb)
    blk_start = blk_end - nb
    dest = (blk_start[top_e] * rows + rank[:, :TOP_K]).reshape(n * TOP_K)
    p_rows = n * TOP_K + n_experts * rows
    nblk = p_rows // rows
    slot_token = jnp.zeros((p_rows,), jnp.int32).at[dest].set(
        jnp.repeat(jnp.arange(n, dtype=jnp.int32), TOP_K))
    ns = (nb + sbk - 1) // sbk
    ns_end = jnp.cumsum(ns)
    ns_start = ns_end - ns
    n_active = ns_end[-1]
    nsuper = nblk // sbk + n_experts
    s_idx = jnp.arange(nsuper, dtype=jnp.int32)
    active = s_idx < n_active
    s_src = jnp.where(active, s_idx, jnp.maximum(n_active - 1, 0))
    sb_expert = jnp.minimum(jnp.sum(s_src[:, None] >= ns_end[None, :], axis=1), n_experts - 1)
    sb_local = s_src - ns_start[sb_expert]
    sb_blk0 = blk_start[sb_expert] + sb_local * sbk
    sb_nsub = jnp.where(active, jnp.clip(nb[sb_expert] - sb_local * sbk, 0, sbk), 0)
    ys = _experts(h2, slot_token, sb_expert.astype(jnp.int32), sb_blk0.astype(jnp.int32),
                  sb_nsub.astype(jnp.int32), jnp.stack([n_active, blk_end[-1]]).astype(jnp.int32),
                  w_gate_up, b_gate_up, w_down, b_down, rows, sbk)
    out = _combine(x1, topw, dest, ys)
    return out.reshape(b, s, d)


def kernel(x, positions, norm1_g, w_in, b_gate, conv_w, a_log, dt_bias, gdn_norm_g, q_norm_g, k_norm_g, lambda_q1, lambda_k1, lambda_q2, lambda_k2, subln_g, w_branch_a, w_branch_b, w_out, norm2_g, w_router, b_router, w_gate_up, b_gate_up, w_down, b_down):
    depth = norm1_g.shape[0]
    for l in range(depth):
        x = _layer(x, positions, l, norm1_g[l], w_in[l], b_gate[l], conv_w[l], a_log[l], dt_bias[l],
                   gdn_norm_g[l], q_norm_g[l], k_norm_g[l], lambda_q1[l], lambda_k1[l], lambda_q2[l],
                   lambda_k2[l], subln_g[l], w_branch_a[l], w_branch_b[l], w_out[l], norm2_g[l],
                   w_router[l], b_router[l], w_gate_up[l], b_gate_up[l], w_down[l], b_down[l])
    return x
```

```python
import functools
import math

import numpy as np
import jax
import jax.numpy as jnp
from jax import lax
from jax.experimental import pallas as pl
from jax.experimental.pallas import tpu as pltpu

F32 = jnp.float32
BF16 = jnp.bfloat16

NORM_EPS = 1e-6
NEG_INF = -1e30
CHUNK = 64
HEAD_DIM = 128
ROT_DIM = HEAD_DIM // 4
ROPE_THETA = 500000.0
TOP_K = 4
SWIGLU_ALPHA = 1.702
SWIGLU_LIMIT = 7.0
LANES = 128
VMEM_LIMIT_BYTES = 60 * 1024 * 1024


def _pick_tile(n, target, quantum):
    if n <= target:
        return n
    t = (target // quantum) * quantum
    while t > quantum and n % t:
        t -= quantum
    assert n % t == 0, (n, target, quantum)
    return t


def _params(*sem):
    return pltpu.CompilerParams(dimension_semantics=sem, vmem_limit_bytes=VMEM_LIMIT_BYTES)


def _sigmoid(x):
    return 1.0 / (1.0 + jnp.exp(-x))


def _softplus(x):
    return jnp.maximum(x, 0.0) + jnp.log(1.0 + jnp.exp(-jnp.abs(x)))


def _dot(a, b):
    return jnp.dot(a.astype(BF16), b.astype(BF16), preferred_element_type=F32)


def _dot_nt(a, b):
    return lax.dot_general(a.astype(BF16), b.astype(BF16), (((1,), (1,)), ((), ())),
                           preferred_element_type=F32)


def _dot_tn(a, b):
    return lax.dot_general(a.astype(BF16), b.astype(BF16), (((0,), (0,)), ((), ())),
                           preferred_element_type=F32)


def _rmsnorm_kernel(x_ref, g_ref, o_ref):
    x = x_ref[...]
    y = x * lax.rsqrt(jnp.mean(x * x, axis=-1, keepdims=True) + NORM_EPS)
    o_ref[...] = (y * g_ref[...]).astype(o_ref.dtype)


def _rmsnorm(x2d, g, out_dtype):
    n, d = x2d.shape
    tm = _pick_tile(n, 512, 8)
    return pl.pallas_call(
        _rmsnorm_kernel,
        grid=(n // tm,),
        in_specs=[pl.BlockSpec((tm, d), lambda i: (i, 0)), pl.BlockSpec((1, d), lambda i: (0, 0))],
        out_specs=pl.BlockSpec((tm, d), lambda i: (i, 0)),
        out_shape=jax.ShapeDtypeStruct((n, d), out_dtype),
        compiler_params=_params("parallel"),
        name="rmsnorm",
    )(x2d, g.reshape(1, d))


def _matmul_kernel(*refs, has_res):
    if has_res:
        x_ref, w_ref, r_ref, o_ref, wb_ref = refs
    else:
        x_ref, w_ref, o_ref, wb_ref = refs

    @pl.when(pl.program_id(1) == 0)
    def _():
        wb_ref[...] = w_ref[...].astype(BF16)

    acc = jnp.dot(x_ref[...], wb_ref[...], preferred_element_type=F32)
    if has_res:
        acc = acc + r_ref[...]
    o_ref[...] = acc.astype(o_ref.dtype)


def _matmul(x, w, col0, ncols, out_dtype, res=None, name="matmul"):
    m, k = x.shape
    tn = _pick_tile(ncols, 1024, LANES)
    tm = _pick_tile(m, 512, 8)
    assert col0 % tn == 0
    cb = col0 // tn
    in_specs = [pl.BlockSpec((tm, k), lambda j, i: (i, 0)),
                pl.BlockSpec((k, tn), lambda j, i: (0, cb + j))]
    args = [x, w]
    if res is not None:
        in_specs.append(pl.BlockSpec((tm, tn), lambda j, i: (i, j)))
        args.append(res)
    return pl.pallas_call(
        functools.partial(_matmul_kernel, has_res=res is not None),
        grid=(ncols // tn, m // tm),
        in_specs=in_specs,
        out_specs=pl.BlockSpec((tm, tn), lambda j, i: (i, j)),
        out_shape=jax.ShapeDtypeStruct((m, ncols), out_dtype),
        scratch_shapes=[pltpu.VMEM((k, tn), BF16)],
        compiler_params=_params("parallel", "arbitrary"),
        name=name,
    )(*args)


def _gdn_kernel(q_ref, k_ref, v_ref, z_ref, ba_ref, cwq_ref, cwk_ref, cwv_ref, alog_ref, dtb_ref,
                ng_ref, o_ref, state_ref, tq_ref, tk_ref, tv_ref, eq_ref, ek_ref, ev_ref, *, hg):
    C = CHUNK
    D = HEAD_DIM

    @pl.when(pl.program_id(2) == 0)
    def _():
        state_ref[...] = jnp.zeros_like(state_ref)
        tq_ref[...] = jnp.zeros_like(tq_ref)
        tk_ref[...] = jnp.zeros_like(tk_ref)
        tv_ref[...] = jnp.zeros_like(tv_ref)

    def conv_silu(u_ref, tail_ref, ext_ref, w_ref):
        u = u_ref[0]
        ext_ref[0:8, :] = tail_ref[...]
        ext_ref[8:8 + C, :] = u
        w = w_ref[...]
        y = ext_ref[5:5 + C, :] * w[0:1, :]
        y = y + ext_ref[6:6 + C, :] * w[1:2, :]
        y = y + ext_ref[7:7 + C, :] * w[2:3, :]
        y = y + u * w[3:4, :]
        tail_ref[...] = u[C - 8:C, :]
        return y * _sigmoid(y)

    def l2norm_heads(x, scale):
        parts = []
        for i in range(hg):
            xi = x[:, i * D:(i + 1) * D]
            parts.append(xi * (lax.rsqrt(jnp.sum(xi * xi, axis=-1, keepdims=True) + NORM_EPS) * scale))
        return jnp.concatenate(parts, axis=1)

    qc = l2norm_heads(conv_silu(q_ref, tq_ref, eq_ref, cwq_ref), D ** -0.5)
    kc = l2norm_heads(conv_silu(k_ref, tk_ref, ek_ref, cwk_ref), 1.0)
    vc = conv_silu(v_ref, tv_ref, ev_ref, cwv_ref)

    G = GDN_GROUP
    R = G * C
    shift_c = C.bit_length() - 1

    ba = ba_ref[0]
    beta_full = _sigmoid(ba)
    g_full = -jnp.exp(alog_ref[0]) * _softplus(ba + dtb_ref[0])
    r64 = lax.broadcasted_iota(jnp.int32, (C, C), 0)
    c64 = lax.broadcasted_iota(jnp.int32, (C, C), 1)
    gcum_full = jnp.dot((r64 >= c64).astype(F32), g_full, preferred_element_type=F32,
                        precision=lax.Precision.HIGHEST)
    gcum_t = gcum_full.T
    row = lax.broadcasted_iota(jnp.int32, (R, R), 0)
    col = lax.broadcasted_iota(jnp.int32, (R, R), 1)
    same_head = jnp.right_shift(row, shift_c) == jnp.right_shift(col, shift_c)
    tril = same_head & (row >= col)
    strict = same_head & (row > col)
    lane2 = lax.broadcasted_iota(jnp.int32, (1, 2 * D), 1)
    r2 = lax.broadcasted_iota(jnp.int32, (2 * D, 2 * D), 0)
    c2 = lax.broadcasted_iota(jnp.int32, (2 * D, 2 * D), 1)
    pair_block = (r2 < D) == (c2 < D)

    def lanes_of(x, i0):
        return jnp.concatenate([x[i0 * C:(i0 + 1) * C], x[(i0 + 1) * C:(i0 + 2) * C]], axis=1)

    for h0 in range(0, hg, G):
        heads = range(h0, h0 + G)

        def rows_of(x):
            return jnp.concatenate([x[:, i * D:(i + 1) * D] for i in heads], axis=0)

        beta_r = jnp.concatenate([beta_full[:, i:i + 1] for i in heads], axis=0)
        gc_r = jnp.concatenate([gcum_full[:, hg + i:hg + i + 1] for i in heads], axis=0)
        gc_l = jnp.concatenate([gcum_t[hg + i:hg + i + 1, :] for i in heads], axis=1)
        g_last = [gcum_full[C - 1:C, hg + i:hg + i + 1] for i in heads]
        gl_r = jnp.concatenate([jnp.broadcast_to(g, (C, 1)) for g in g_last], axis=0)

        qn, kn, v_r = rows_of(qc), rows_of(kc), rows_of(vc)
        decay = jnp.where(tril, jnp.exp(jnp.where(tril, gc_r - gc_l, 0.0)), 0.0)
        eg = jnp.exp(gc_r)
        kb = kn * beta_r
        gram = _dot_nt(jnp.concatenate([kb, qn], axis=0), kn)
        nmat = jnp.where(strict, gram[:R] * decay, 0.0)
        qk = gram[R:] * decay
        kbg_r = kb * eg
        qg_r = qn * eg
        kdec_r = kn * jnp.exp(gl_r - gc_r)
        states, ks_rows, o1_rows = [], [], []
        for pr in range(G // 2):
            s = state_ref[h0 // 2 + pr]
            xs = _dot(jnp.concatenate([lanes_of(kbg_r, 2 * pr), lanes_of(qg_r, 2 * pr)], axis=0), s)
            states.append(s)
            ks_rows += [xs[:C, :D], xs[:C, D:]]
            o1_rows += [xs[C:, :D], xs[C:, D:]]
        y = v_r * beta_r - jnp.concatenate(ks_rows, axis=0)
        m = -nmat
        for _ in range(shift_c - 1):
            mb = m.astype(BF16)
            x = jnp.dot(mb, jnp.concatenate([mb, y.astype(BF16)], axis=1), preferred_element_type=F32)
            m = x[:, :R]
            y = y + x[:, R:]
        vnew_r = y + _dot(m, y)
        for pr in range(G // 2):
            upd = _dot_tn(lanes_of(kdec_r, 2 * pr), lanes_of(vnew_r, 2 * pr))
            gl_lanes = jnp.where(lane2 < D, jnp.exp(g_last[2 * pr]), jnp.exp(g_last[2 * pr + 1]))
            state_ref[h0 // 2 + pr] = states[pr] * gl_lanes + jnp.where(pair_block, upd, 0.0)
        o = jnp.concatenate(o1_rows, axis=0) + _dot(qk, vnew_r)
        on = o * lax.rsqrt(jnp.mean(o * o, axis=-1, keepdims=True) + NORM_EPS) * ng_ref[...]
        z_r = rows_of(z_ref[0])
        out = (on * (z_r * _sigmoid(z_r))).astype(o_ref.dtype)
        for j, i in enumerate(heads):
            o_ref[0, :, i * D:(i + 1) * D] = out[j * C:(j + 1) * C]


def _gdn(proj_a, ba, conv_w, alog_p, dtb_p, norm_g, heads, hg):
    b, s, _ = proj_a.shape
    assert hg % 2 == 0 and heads % hg == 0
    ngroups = heads // hg
    w = hg * HEAD_DIM
    nc = s // CHUNK

    def colspec(off):
        return pl.BlockSpec((1, CHUNK, w), lambda bi, gi, ci, off=off: (bi, ci, off * ngroups + gi))

    def cwspec(off):
        return pl.BlockSpec((4, w), lambda bi, gi, ci, off=off: (0, off * ngroups + gi))

    vec = pl.BlockSpec((1, 1, LANES), lambda bi, gi, ci: (gi, 0, 0))
    return pl.pallas_call(
        functools.partial(_gdn_kernel, hg=hg),
        grid=(b, ngroups, nc),
        in_specs=[colspec(0), colspec(1), colspec(2), colspec(3),
                  pl.BlockSpec((1, CHUNK, LANES), lambda bi, gi, ci: (bi, ci, gi)),
                  cwspec(0), cwspec(1), cwspec(2), vec, vec,
                  pl.BlockSpec((1, HEAD_DIM), lambda bi, gi, ci: (0, 0))],
        out_specs=pl.BlockSpec((1, CHUNK, w), lambda bi, gi, ci: (bi, ci, gi)),
        out_shape=jax.ShapeDtypeStruct((b, s, heads * HEAD_DIM), BF16),
        scratch_shapes=[pltpu.VMEM((hg // 2, 2 * HEAD_DIM, 2 * HEAD_DIM), F32)]
        + [pltpu.VMEM((8, w), F32)] * 3 + [pltpu.VMEM((8 + CHUNK, w), F32)] * 3,
        compiler_params=_params("parallel", "parallel", "arbitrary"),
        name="gated_delta_rule",
    )(proj_a, proj_a, proj_a, proj_a, ba, conv_w, conv_w, conv_w, alog_p, dtb_p,
      norm_g.reshape(1, HEAD_DIM))


def _rope_table_kernel(pos_ref, freq_ref, cos_ref, sin_ref):
    ang = pos_ref[...] * freq_ref[...]
    lane = lax.broadcasted_iota(jnp.int32, ang.shape, 1)
    sin = jnp.sin(ang)
    cos_ref[...] = jnp.cos(ang)
    sin_ref[...] = jnp.where(lane < ROT_DIM // 2, -sin, sin)


def _rope_table(pos_f):
    n = pos_f.shape[0]
    ts = _pick_tile(n, 1024, 8)
    half = ROT_DIM // 2
    inv_freq = np.power(np.float32(ROPE_THETA),
                        -np.arange(half, dtype=np.float32) * np.float32(2.0) / np.float32(ROT_DIM))
    freq = np.zeros((1, LANES), np.float32)
    freq[0, :half] = inv_freq
    freq[0, half:ROT_DIM] = inv_freq
    row = pl.BlockSpec((ts, LANES), lambda i: (i, 0))
    return pl.pallas_call(
        _rope_table_kernel,
        grid=(n // ts,),
        in_specs=[pl.BlockSpec((ts, 1), lambda i: (i, 0)), pl.BlockSpec((1, LANES), lambda i: (0, 0))],
        out_specs=[row, row],
        out_shape=[jax.ShapeDtypeStruct((n, LANES), F32)] * 2,
        compiler_params=_params("parallel"),
        name="rope_table",
    )(pos_f, jnp.asarray(freq))


def _proj_t_kernel(*refs, shift, rope, q_tiles, scales):
    refs = list(refs)
    x_ref, w_ref = refs[:2]
    w2_ref = refs[2] if shift else None
    o_ref, wb_ref = refs[-2 - bool(shift)], refs[-1 - bool(shift)]
    tn = wb_ref.shape[1]

    @pl.when(pl.program_id(1) == 0)
    def _():
        if shift:
            ws_ref = refs[-1]
            ws_ref[:tn - shift, :] = w_ref[shift:, :]
            ws_ref[tn - shift:, :] = w2_ref[:shift, :]
            wb_ref[...] = ws_ref[...].T.astype(BF16)
        else:
            wb_ref[...] = w_ref[...].T.astype(BF16)

    acc = jnp.dot(x_ref[...], wb_ref[...], preferred_element_type=F32)
    if not rope:
        o_ref[...] = acc.astype(o_ref.dtype)
        return
    cos_ref, sin_ref, g_ref = refs[2 + bool(shift):5 + bool(shift)]
    is_q = pl.program_id(0) < q_tiles
    g = jnp.where(is_q, g_ref[0:1, :], g_ref[1:2, :])
    scale = jnp.where(is_q, scales[0], scales[1])
    cos = cos_ref[...]
    sin_signed = sin_ref[...]
    lane = lax.broadcasted_iota(jnp.int32, cos.shape, 1)
    first_half = lane < ROT_DIM // 2
    for j in range(tn // HEAD_DIM):
        sl = slice(j * HEAD_DIM, (j + 1) * HEAD_DIM)
        x = acc[:, sl]
        xn = x * (lax.rsqrt(jnp.mean(x * x, axis=-1, keepdims=True) + NORM_EPS) * scale) * g
        partner = jnp.where(first_half, pltpu.roll(xn, HEAD_DIM - ROT_DIM // 2, 1),
                            pltpu.roll(xn, ROT_DIM // 2, 1))
        o_ref[:, sl] = (xn * cos + partner * sin_signed).astype(o_ref.dtype)


def _proj_t(x, wt, row0, nrows, out_dtype, rope=None, name="proj"):
    m, k = x.shape
    shift = row0 % LANES
    tn = _pick_tile(rope[4] if rope else nrows, 1024, LANES)
    tm = _pick_tile(m, 512, 8)
    assert (row0 - shift) % tn == 0 and nrows % tn == 0
    cb = (row0 - shift) // tn
    per = tn // LANES
    in_specs = [pl.BlockSpec((tm, k), lambda j, i: (i, 0)),
                pl.BlockSpec((tn, k), lambda j, i: (cb + j, 0))]
    args = [x, wt]
    scratch = [pltpu.VMEM((k, tn), BF16)]
    if shift:
        in_specs.append(pl.BlockSpec((LANES, k), lambda j, i: ((cb + j + 1) * per, 0)))
        args.append(wt)
        scratch.append(pltpu.VMEM((tn, k), F32))
    q_tiles, scales = 0, (1.0, 1.0)
    if rope:
        cos, sin_signed, gains, scales, q_cols = rope
        q_tiles = q_cols // tn
        in_specs += [pl.BlockSpec((tm, LANES), lambda j, i: (i, 0)),
                     pl.BlockSpec((tm, LANES), lambda j, i: (i, 0)),
                     pl.BlockSpec((2, HEAD_DIM), lambda j, i: (0, 0))]
        args += [cos, sin_signed, gains]
    return pl.pallas_call(
        functools.partial(_proj_t_kernel, shift=shift, rope=rope is not None, q_tiles=q_tiles,
                          scales=tuple(scales)),
        grid=(nrows // tn, m // tm),
        in_specs=in_specs,
        out_specs=pl.BlockSpec((tm, tn), lambda j, i: (i, j)),
        out_shape=jax.ShapeDtypeStruct((m, nrows), out_dtype),
        scratch_shapes=scratch,
        compiler_params=_params("parallel", "arbitrary"),
        name=name,
    )(*args)


def _diff_attn_kernel(q_ref, k_ref, v_ref, lq1_ref, lk1_ref, lq2_ref, lk2_ref, sg_ref, o_ref,
                      acc_ref, m_ref, l_ref, *, tq, lam_init):
    D = HEAD_DIM
    qi = pl.program_id(2)
    q1 = q_ref[0, :, :D]
    q2 = q_ref[0, :, D:]
    acc_ref[...] = jnp.zeros_like(acc_ref)
    m_ref[...] = jnp.full_like(m_ref, NEG_INF)
    l_ref[...] = jnp.zeros_like(l_ref)

    def lanes(x, width):
        return jnp.concatenate([x] * (width // LANES), axis=1)

    def step(start, mask):
        kblk = k_ref[0, pl.ds(start, tq), :]
        vblk = v_ref[0, pl.ds(start, tq), :].astype(BF16)
        s = jnp.concatenate([_dot_nt(q1, kblk[:, :D]), _dot_nt(q2, kblk[:, D:])], axis=0)
        if mask is not None:
            s = jnp.where(mask, s, NEG_INF)
        m_prev = m_ref[...]
        m_next = jnp.maximum(m_prev, jnp.max(s, axis=-1, keepdims=True))
        alpha = jnp.exp2(m_prev - m_next)
        p = jnp.exp2(s - lanes(m_next, tq))
        l_ref[...] = alpha * l_ref[...] + jnp.sum(p, axis=-1, keepdims=True)
        acc_ref[...] = lanes(alpha, 2 * D) * acc_ref[...] + jnp.dot(p.astype(BF16), vblk,
                                                                    preferred_element_type=F32)
        m_ref[...] = m_next

    def body(j, carry):
        step(pl.multiple_of(j * tq, tq), None)
        return carry

    lax.fori_loop(0, qi, body, 0)
    shift_c = CHUNK.bit_length() - 1
    row = lax.broadcasted_iota(jnp.int32, (2 * tq, tq), 0)
    row = jnp.right_shift(jnp.where(row >= tq, row - tq, row), shift_c)
    col = jnp.right_shift(lax.broadcasted_iota(jnp.int32, (2 * tq, tq), 1), shift_c)
    step(pl.multiple_of(qi * tq, tq), row >= col)

    lam = (jnp.exp(jnp.sum(lq1_ref[...] * lk1_ref[...], axis=-1, keepdims=True))
           - jnp.exp(jnp.sum(lq2_ref[...] * lk2_ref[...], axis=-1, keepdims=True)) + lam_init)
    a = acc_ref[...] / lanes(l_ref[...], 2 * D)
    o = a[:tq] - lam * a[tq:]
    on = o * lax.rsqrt(jnp.mean(o * o, axis=-1, keepdims=True) + NORM_EPS)
    o_ref[0] = (on * sg_ref[...] * (1.0 - lam_init)).astype(o_ref.dtype)


def _diff_attn(qk, v_src, v_col_block, lq1, lk1, lq2, lk2, subln_g, heads, lam_init):
    b, s, _ = qk.shape
    hw = 2 * HEAD_DIM
    tq = _pick_tile(s, ATTN_BLOCK, CHUNK)
    vec = pl.BlockSpec((1, HEAD_DIM), lambda bi, hi, qi: (0, 0))
    return pl.pallas_call(
        functools.partial(_diff_attn_kernel, tq=tq, lam_init=lam_init),
        grid=(b, heads, s // tq),
        in_specs=[pl.BlockSpec((1, tq, hw), lambda bi, hi, qi: (bi, qi, hi)),
                  pl.BlockSpec((1, s, hw), lambda bi, hi, qi: (bi, 0, heads + hi)),
                  pl.BlockSpec((1, s, hw), lambda bi, hi, qi: (bi, 0, v_col_block + hi)),
                  vec, vec, vec, vec,
                  pl.BlockSpec((1, hw), lambda bi, hi, qi: (0, 0))],
        out_specs=pl.BlockSpec((1, tq, hw), lambda bi, hi, qi: (bi, qi, hi)),
        out_shape=jax.ShapeDtypeStruct((b, s, heads * hw), BF16),
        scratch_shapes=[pltpu.VMEM((2 * tq, hw), F32)] + [pltpu.VMEM((2 * tq, LANES), F32)] * 2,
        compiler_params=_params("parallel", "parallel", "arbitrary"),
        name="diff_attention",
    )(qk, qk, v_src, lq1.reshape(1, -1), lk1.reshape(1, -1), lq2.reshape(1, -1), lk2.reshape(1, -1),
      subln_g.reshape(1, hw))


def _merge_kernel(oa_ref, ob_ref, wa_ref, wb_ref, ga_ref, gb_ref, ba_ref, bb_ref, o_ref, wa_s, wb_s):
    @pl.when(pl.program_id(1) == 0)
    def _():
        wa_s[...] = wa_ref[...].astype(BF16)
        wb_s[...] = wb_ref[...].astype(BF16)

    ya = jnp.dot(oa_ref[...], wa_s[...], preferred_element_type=F32)
    yb = jnp.dot(ob_ref[...], wb_s[...], preferred_element_type=F32)
    merged = _sigmoid(ga_ref[...] + ba_ref[...]) * ya + _sigmoid(gb_ref[...] + bb_ref[...]) * yb
    o_ref[...] = merged.astype(o_ref.dtype)


def _merge(oa, ob, w_a, w_b, proj_b, gate_col0, b_gate, d):
    m, ka = oa.shape
    kb = ob.shape[1]
    tn = _pick_tile(d, 512, LANES)
    tm = _pick_tile(m, 512, 8)
    assert gate_col0 % tn == 0
    ga = gate_col0 // tn
    gb = (gate_col0 + d) // tn
    nb = d // tn
    bg = b_gate.reshape(1, 2 * d)
    return pl.pallas_call(
        _merge_kernel,
        grid=(nb, m // tm),
        in_specs=[pl.BlockSpec((tm, ka), lambda j, i: (i, 0)),
                  pl.BlockSpec((tm, kb), lambda j, i: (i, 0)),
                  pl.BlockSpec((ka, tn), lambda j, i: (0, j)),
                  pl.BlockSpec((kb, tn), lambda j, i: (0, j)),
                  pl.BlockSpec((tm, tn), lambda j, i: (i, ga + j)),
                  pl.BlockSpec((tm, tn), lambda j, i: (i, gb + j)),
                  pl.BlockSpec((1, tn), lambda j, i: (0, j)),
                  pl.BlockSpec((1, tn), lambda j, i: (0, nb + j))],
        out_specs=pl.BlockSpec((tm, tn), lambda j, i: (i, j)),
        out_shape=jax.ShapeDtypeStruct((m, d), BF16),
        scratch_shapes=[pltpu.VMEM((ka, tn), BF16), pltpu.VMEM((kb, tn), BF16)],
        compiler_params=_params("parallel", "arbitrary"),
        name="branch_merge",
    )(oa, ob, w_a, w_b, proj_b, proj_b, bg, bg)


def _router_kernel(x_ref, g_ref, wr_ref, br_ref, h_ref, ti_ref, tw_ref, rk_ref, cnt_ref, carry_ref,
                   *, n_experts):
    @pl.when(pl.program_id(0) == 0)
    def _():
        carry_ref[...] = jnp.zeros_like(carry_ref)

    x = x_ref[...]
    h = x * lax.rsqrt(jnp.mean(x * x, axis=-1, keepdims=True) + NORM_EPS) * g_ref[...]
    h_ref[...] = h
    logits = jnp.dot(h, wr_ref[...], preferred_element_type=F32,
                     precision=lax.Precision.HIGHEST) + br_ref[...]
    tm = x.shape[0]
    lane = lax.broadcasted_iota(jnp.int32, (tm, LANES), 1)
    lane_f = lane.astype(F32)
    cur = jnp.where(lane < n_experts, logits, -jnp.inf)
    vals, hots = [], []
    ti = jnp.zeros((tm, LANES), F32)
    for kk in range(TOP_K):
        mx = jnp.max(cur, axis=-1, keepdims=True)
        idx = jnp.min(jnp.where(cur == mx, lane_f, float(LANES)), axis=-1, keepdims=True)
        hot = lane_f == idx
        vals.append(mx)
        hots.append(hot)
        ti = jnp.where(lane == kk, idx, ti)
        cur = jnp.where(hot, -jnp.inf, cur)
    exps = [jnp.exp(v - vals[0]) for v in vals]
    denom = exps[0] + exps[1] + exps[2] + exps[3]
    tw = jnp.zeros((tm, LANES), F32)
    for kk in range(TOP_K):
        tw = jnp.where(lane == kk, exps[kk] / denom, tw)
    multihot = jnp.zeros((tm, LANES), F32)
    for hot in hots:
        multihot = multihot + hot.astype(F32)
    r = lax.broadcasted_iota(jnp.int32, (tm, tm), 0)
    c = lax.broadcasted_iota(jnp.int32, (tm, tm), 1)
    prefix = jnp.dot((r > c).astype(BF16), multihot.astype(BF16), preferred_element_type=F32)
    prefix = prefix + carry_ref[...]
    rk = jnp.zeros((tm, LANES), F32)
    for kk in range(TOP_K):
        rk_k = jnp.sum(jnp.where(hots[kk], prefix, 0.0), axis=-1, keepdims=True)
        rk = jnp.where(lane == kk, rk_k, rk)
    carry = carry_ref[...] + jnp.sum(multihot, axis=0, keepdims=True)
    carry_ref[...] = carry
    cnt_ref[...] = carry.astype(jnp.int32)
    ti_ref[...] = ti.astype(jnp.int32)
    tw_ref[...] = tw
    rk_ref[...] = rk.astype(jnp.int32)


def _router(x1, g, w_router, b_router):
    n, d = x1.shape
    e = w_router.shape[1]
    tm = _pick_tile(n, 256, 8)
    wr = jnp.zeros((d, LANES), F32).at[:, :e].set(w_router)
    br = jnp.zeros((1, LANES), F32).at[0, :e].set(b_router)
    row = pl.BlockSpec((tm, LANES), lambda i: (i, 0))
    return pl.pallas_call(
        functools.partial(_router_kernel, n_experts=e),
        grid=(n // tm,),
        in_specs=[pl.BlockSpec((tm, d), lambda i: (i, 0)),
                  pl.BlockSpec((1, d), lambda i: (0, 0)),
                  pl.BlockSpec((d, LANES), lambda i: (0, 0)),
                  pl.BlockSpec((1, LANES), lambda i: (0, 0))],
        out_specs=[pl.BlockSpec((tm, d), lambda i: (i, 0)), row, row, row,
                   pl.BlockSpec((1, LANES), lambda i: (0, 0))],
        out_shape=[jax.ShapeDtypeStruct((n, d), F32),
                   jax.ShapeDtypeStruct((n, LANES), jnp.int32),
                   jax.ShapeDtypeStruct((n, LANES), F32),
                   jax.ShapeDtypeStruct((n, LANES), jnp.int32),
                   jax.ShapeDtypeStruct((1, LANES), jnp.int32)],
        scratch_shapes=[pltpu.VMEM((1, LANES), F32)],
        compiler_params=_params("arbitrary"),
        name="router_topk",
    )(x1, g.reshape(1, d), wr, br)


def _row_copy(src_ref, src_row, dst_ref, dst_row, sem):
    return pltpu.make_async_copy(src_ref.at[pl.ds(src_row, 1)], dst_ref.at[pl.ds(dst_row, 1)], sem)


def _expert_kernel(se_ref, sb0_ref, sn_ref, na_ref, *refs, rows, sbk, nf, nsuper, nblk):
    tok_refs = refs[:sbk]
    tokn_refs = refs[sbk:2 * sbk]
    (h_ref, wg_ref, wu_ref, wd_ref, bg_ref, bu_ref, bd_ref, ys_ref,
     xg_ref, xb_ref, acc_ref, wgb_ref, wub_ref, wdb_ref, gsem, osem) = refs[2 * sbk:]
    s = pl.program_id(0)
    f = pl.program_id(1)
    nsub = sn_ref[s]

    def gather(t_refs, count):
        for j in range(sbk):
            @pl.when(j < count)
            def _(j=j):
                def body(r, carry):
                    _row_copy(h_ref, t_refs[j][0, 0, r], xg_ref, j * rows + r, gsem).start()
                    return carry
                lax.fori_loop(0, rows, body, 0, unroll=8)

    def out_copy(j, blk0):
        return pltpu.make_async_copy(acc_ref.at[pl.ds(j * rows, rows)],
                                     ys_ref.at[pl.ds((blk0 + j) * rows, rows)], osem)

    n_act = na_ref[0]
    blocks_used = na_ref[1]

    def tail_block(j):
        return blocks_used + (s - n_act) * sbk + j

    def zero_copy(j):
        return pltpu.make_async_copy(acc_ref.at[pl.ds(0, rows)],
                                     ys_ref.at[pl.ds(tail_block(j) * rows, rows)], osem)

    @pl.when((f == 0) & (s == 0))
    def _():
        gather(tok_refs, nsub)

    @pl.when(f == 0)
    def _():
        for j in range(sbk):
            @pl.when(j < nsub)
            def _(j=j):
                pltpu.make_async_copy(h_ref.at[pl.ds(0, rows)], xg_ref.at[pl.ds(j * rows, rows)], gsem).wait()
        for j in range(sbk):
            @pl.when(j < nsub)
            def _(j=j):
                xb_ref[pl.ds(j * rows, rows), :] = xg_ref[pl.ds(j * rows, rows), :].astype(BF16)

        @pl.when(s + 1 < nsuper)
        def _():
            gather(tokn_refs, sn_ref[s + 1])

        @pl.when(s > 0)
        def _():
            for j in range(sbk):
                @pl.when(j < sn_ref[s - 1])
                def _(j=j):
                    out_copy(j, sb0_ref[s - 1]).wait()

        for j in range(sbk):
            @pl.when(j < nsub)
            def _(j=j):
                acc_ref[pl.ds(j * rows, rows), :] = jnp.broadcast_to(bd_ref[0], (rows, acc_ref.shape[1]))

        @pl.when(s == n_act)
        def _():
            acc_ref[pl.ds(0, rows), :] = jnp.zeros((rows, acc_ref.shape[1]), F32)

        for j in range(sbk):
            @pl.when((s >= n_act) & (tail_block(j) < nblk))
            def _(j=j):
                zero_copy(j).start()

    @pl.when(nsub > 0)
    def _():
        wgb_ref[...] = wg_ref[0].astype(BF16)
        wub_ref[...] = wu_ref[0].astype(BF16)
        wdb_ref[...] = wd_ref[0].astype(BF16)

    for j in range(sbk):
        @pl.when(j < nsub)
        def _(j=j):
            xb = xb_ref[pl.ds(j * rows, rows), :]
            g = jnp.dot(xb, wgb_ref[...], preferred_element_type=F32) + bg_ref[0]
            u = jnp.dot(xb, wub_ref[...], preferred_element_type=F32) + bu_ref[0]
            gate = jnp.minimum(g, SWIGLU_LIMIT)
            up = jnp.clip(u, -SWIGLU_LIMIT, SWIGLU_LIMIT)
            act = (up + 1.0) * gate * _sigmoid(SWIGLU_ALPHA * gate)
            acc_ref[pl.ds(j * rows, rows), :] += jnp.dot(act.astype(BF16), wdb_ref[...],
                                                         preferred_element_type=F32)

    @pl.when(f == nf - 1)
    def _():
        for j in range(sbk):
            @pl.when((s >= n_act) & (tail_block(j) < nblk))
            def _(j=j):
                zero_copy(j).wait()

        for j in range(sbk):
            @pl.when(j < nsub)
            def _(j=j):
                out_copy(j, sb0_ref[s]).start()

        @pl.when(s == nsuper - 1)
        def _():
            for j in range(sbk):
                @pl.when(j < nsub)
                def _(j=j):
                    out_copy(j, sb0_ref[s]).wait()


def _experts(h, slot_token, sb_expert, sb_blk0, sb_nsub, n_active, w_gate_up, b_gate_up, w_down, b_down,
             rows, sbk):
    n, d = h.shape
    p = slot_token.shape[0]
    e, _, dff2 = w_gate_up.shape
    dff = dff2 // 2
    tf = _pick_tile(dff, 512, LANES)
    nf = dff // tf
    nblk = p // rows
    nsuper = sb_expert.shape[0]
    tok3 = slot_token.reshape(nblk, 1, rows)

    def frozen_f(s, f, na):
        return jnp.where(s < na[0], f, nf - 1)

    def tok_spec(j, ahead):
        def index(s, f, se, sb0, sn, na):
            si = jnp.minimum(s + ahead, nsuper - 1)
            return (jnp.minimum(sb0[si] + j, nblk - 1), 0, 0)
        return pl.BlockSpec((1, 1, rows), index, memory_space=pltpu.SMEM)

    grid_spec = pltpu.PrefetchScalarGridSpec(
        num_scalar_prefetch=4,
        grid=(nsuper, nf),
        in_specs=[tok_spec(j, 0) for j in range(sbk)] + [tok_spec(j, 1) for j in range(sbk)] + [
            pl.BlockSpec(memory_space=pl.ANY),
            pl.BlockSpec((1, d, tf), lambda s, f, se, sb0, sn, na: (se[s], 0, frozen_f(s, f, na))),
            pl.BlockSpec((1, d, tf), lambda s, f, se, sb0, sn, na: (se[s], 0, nf + frozen_f(s, f, na))),
            pl.BlockSpec((1, tf, d), lambda s, f, se, sb0, sn, na: (se[s], frozen_f(s, f, na), 0)),
            pl.BlockSpec((1, 1, tf), lambda s, f, se, sb0, sn, na: (se[s], 0, frozen_f(s, f, na))),
            pl.BlockSpec((1, 1, tf), lambda s, f, se, sb0, sn, na: (se[s], 0, nf + frozen_f(s, f, na))),
            pl.BlockSpec((1, 1, d), lambda s, f, se, sb0, sn, na: (se[s], 0, 0))],
        out_specs=pl.BlockSpec(memory_space=pl.ANY),
        scratch_shapes=[pltpu.VMEM((sbk * rows, d), F32), pltpu.VMEM((sbk * rows, d), BF16),
                        pltpu.VMEM((sbk * rows, d), F32),
                        pltpu.VMEM((d, tf), BF16), pltpu.VMEM((d, tf), BF16), pltpu.VMEM((tf, d), BF16),
                        pltpu.SemaphoreType.DMA(()), pltpu.SemaphoreType.DMA(())],
    )
    return pl.pallas_call(
        functools.partial(_expert_kernel, rows=rows, sbk=sbk, nf=nf, nsuper=nsuper, nblk=nblk),
        grid_spec=grid_spec,
        out_shape=jax.ShapeDtypeStruct((p, d), F32),
        compiler_params=_params("arbitrary", "arbitrary"),
        name="expert_mlp",
    )(sb_expert, sb_blk0, sb_nsub, n_active, *([tok3] * (2 * sbk)), h, w_gate_up, w_gate_up, w_down,
      b_gate_up.reshape(e, 1, dff2), b_gate_up.reshape(e, 1, dff2), b_down.reshape(e, 1, d))


def _combine_kernel(dest_ref, destn_ref, x_ref, tw_ref, ys_ref, o_ref, buf_ref, sem, *, tm, nt):
    i = pl.program_id(0)
    slot = i % 2

    def gather(d_ref, s):
        def body(t, carry):
            for kk in range(TOP_K):
                _row_copy(ys_ref, d_ref[0, 0, t * TOP_K + kk], buf_ref.at[s, kk], t, sem.at[s]).start()
            return carry
        lax.fori_loop(0, tm, body, 0, unroll=2)

    @pl.when(i == 0)
    def _():
        gather(dest_ref, 0)

    @pl.when(i + 1 < nt)
    def _():
        gather(destn_ref, 1 - slot)

    for kk in range(TOP_K):
        pltpu.make_async_copy(ys_ref.at[pl.ds(0, tm)], buf_ref.at[slot, kk], sem.at[slot]).wait()
    tw = tw_ref[...]
    y = buf_ref[slot, 0] * tw[:, 0:1]
    for kk in range(1, TOP_K):
        y = y + buf_ref[slot, kk] * tw[:, kk:kk + 1]
    o_ref[...] = x_ref[...] + y


def _combine(x1, topw, dest, ys):
    n, d = x1.shape
    tm = _pick_tile(n, 128, 8)
    nt = n // tm
    dest3 = dest.reshape(nt, 1, tm * TOP_K)
    return pl.pallas_call(
        functools.partial(_combine_kernel, tm=tm, nt=nt),
        grid=(nt,),
        in_specs=[pl.BlockSpec((1, 1, tm * TOP_K), lambda i: (i, 0, 0), memory_space=pltpu.SMEM),
                  pl.BlockSpec((1, 1, tm * TOP_K), lambda i: (jnp.minimum(i + 1, nt - 1), 0, 0),
                               memory_space=pltpu.SMEM),
                  pl.BlockSpec((tm, d), lambda i: (i, 0)),
                  pl.BlockSpec((tm, LANES), lambda i: (i, 0)),
                  pl.BlockSpec(memory_space=pl.ANY)],
        out_specs=pl.BlockSpec((tm, d), lambda i: (i, 0)),
        out_shape=jax.ShapeDtypeStruct((n, d), F32),
        scratch_shapes=[pltpu.VMEM((2, TOP_K, tm, d), F32), pltpu.SemaphoreType.DMA((2,))],
        compiler_params=_params("arbitrary"),
        name="moe_combine",
    )(dest3, dest3, x1, topw, ys)


ATTN_BLOCK = 512
GDN_HEADS_PER_STEP = 16
GDN_GROUP = 4
EXPERT_ROWS = 512
EXPERT_SUBBLOCKS = 2


def _layer(x, positions, layer_idx, norm1_g, w_in, b_gate, conv_w, a_log, dt_bias, gdn_norm_g,
           q_norm_g, k_norm_g, lambda_q1, lambda_k1, lambda_q2, lambda_k2, subln_g,
           w_branch_a, w_branch_b, w_out, norm2_g, w_router, b_router,
           w_gate_up, b_gate_up, w_down, b_down):
    b, s, d = x.shape
    n = b * s
    gdn_heads = a_log.shape[0]
    gdn_dim = gdn_heads * HEAD_DIM
    diff_v = w_branch_b.shape[0]
    diff_heads = diff_v // (2 * HEAD_DIM)
    diff_qk = 2 * diff_heads * HEAD_DIM
    off_beta = 4 * gdn_dim
    off_alpha = off_beta + gdn_heads
    off_qb = off_alpha + gdn_heads
    off_gate = off_qb + 2 * diff_qk + diff_v
    assert w_in.shape[1] == off_gate + 2 * d
    n_experts = w_router.shape[1]

    x2 = x.reshape(n, d)
    h = _rmsnorm(x2, norm1_g, BF16)

    wt = jnp.swapaxes(w_in, 0, 1)
    proj_a = _proj_t(h, wt, 0, off_beta, F32, name="proj_gdn")
    hg = min(GDN_HEADS_PER_STEP, gdn_heads)
    ngroups = gdn_heads // hg
    row_pad = jnp.zeros((LANES - 2 * hg, d), F32)
    w_ba = jnp.concatenate(
        [part for gi in range(ngroups) for part in (
            wt[off_beta + gi * hg:off_beta + (gi + 1) * hg],
            wt[off_alpha + gi * hg:off_alpha + (gi + 1) * hg], row_pad)], axis=0)
    ba = _proj_t(h, w_ba, 0, ngroups * LANES, F32, name="proj_beta_alpha")
    cos_t, sin_t = _rope_table(positions.astype(F32).reshape(n, 1))
    qk = _proj_t(h, wt, off_qb, 2 * diff_qk, BF16, name="proj_qk_rope",
                 rope=(cos_t, sin_t, jnp.stack([q_norm_g, k_norm_g]),
                       (HEAD_DIM ** -0.5 * math.log2(math.e), 1.0), diff_qk))
    vg = _proj_t(h, wt, off_qb + 2 * diff_qk, diff_v + 2 * d, F32, name="proj_v_gate")

    alog_p = jnp.zeros((ngroups, 1, LANES), F32).at[:, 0, hg:2 * hg].set(a_log.reshape(ngroups, hg))
    dtb_p = jnp.zeros((ngroups, 1, LANES), F32).at[:, 0, hg:2 * hg].set(dt_bias.reshape(ngroups, hg))
    oa = _gdn(proj_a.reshape(b, s, off_beta), ba.reshape(b, s, ngroups * LANES), conv_w,
              alog_p, dtb_p, gdn_norm_g, gdn_heads, hg)

    lam_init = 0.8 - 0.6 * math.exp(-0.3 * layer_idx)
    ob = _diff_attn(qk.reshape(b, s, 2 * diff_qk), vg.reshape(b, s, diff_v + 2 * d), 0,
                    lambda_q1, lambda_k1, lambda_q2, lambda_k2, subln_g, diff_heads, lam_init)

    merged = _merge(oa.reshape(n, gdn_dim), ob.reshape(n, diff_v), w_branch_a, w_branch_b,
                    vg, diff_v, b_gate, d)
    x1 = _matmul(merged, w_out, 0, d, F32, res=x2, name="out_proj")

    h2, topi, topw, rank, counts = _router(x1, norm2_g, w_router, b_router)
    rows, sbk = EXPERT_ROWS, EXPERT_SUBBLOCKS
    top_e = topi[:, :TOP_K]
    cnt = counts[0, :n_experts]
    nb = (cnt + rows - 1) // rows
    blk_end = jnp.cumsum(nb)
    blk_start = blk_end - nb
    dest = (blk_start[top_e] * rows + rank[:, :TOP_K]).reshape(n * TOP_K)
    p_rows = n * TOP_K + n_experts * rows
    nblk = p_rows // rows
    slot_token = jnp.zeros((p_rows,), jnp.int32).at[dest].set(
        jnp.repeat(jnp.arange(n, dtype=jnp.int32), TOP_K), unique_indices=True, mode="promise_in_bounds")
    ns = (nb + sbk - 1) // sbk
    ns_end = jnp.cumsum(ns)
    ns_start = ns_end - ns
    n_active = ns_end[-1]
    nsuper = nblk // sbk + n_experts
    s_idx = jnp.arange(nsuper, dtype=jnp.int32)
    active = s_idx < n_active
    s_src = jnp.where(active, s_idx, jnp.maximum(n_active - 1, 0))
    sb_expert = jnp.minimum(jnp.sum(s_src[:, None] >= ns_end[None, :], axis=1), n_experts - 1)
    sb_local = s_src - ns_start[sb_expert]
    sb_blk0 = blk_start[sb_expert] + sb_local * sbk
    sb_nsub = jnp.where(active, jnp.clip(nb[sb_expert] - sb_local * sbk, 0, sbk), 0)
    ys = _experts(h2, slot_token, sb_expert.astype(jnp.int32), sb_blk0.astype(jnp.int32),
                  sb_nsub.astype(jnp.int32), jnp.stack([n_active, blk_end[-1]]).astype(jnp.int32),
                  w_gate_up, b_gate_up, w_down, b_down, rows, sbk)
    out = _combine(x1, topw, dest, ys)
    return out.reshape(b, s, d)


def kernel(x, positions, norm1_g, w_in, b_gate, conv_w, a_log, dt_bias, gdn_norm_g, q_norm_g, k_norm_g, lambda_q1, lambda_k1, lambda_q2, lambda_k2, subln_g, w_branch_a, w_branch_b, w_out, norm2_g, w_router, b_router, w_gate_up, b_gate_up, w_down, b_down):
    depth = norm1_g.shape[0]
    for l in range(depth):
        x = _layer(x, positions, l, norm1_g[l], w_in[l], b_gate[l], conv_w[l], a_log[l], dt_bias[l],
                   gdn_norm_g[l], q_norm_g[l], k_norm_g[l], lambda_q1[l], lambda_k1[l], lambda_q2[l],
                   lambda_k2[l], subln_g[l], w_branch_a[l], w_branch_b[l], w_out[l], norm2_g[l],
                   w_router[l], b_router[l], w_gate_up[l], b_gate_up[l], w_down[l], b_down[l])
    return x
```

```python
import functools
import math

import numpy as np
import jax
import jax.numpy as jnp
from jax import lax
from jax.experimental import pallas as pl
from jax.experimental.pallas import tpu as pltpu

F32 = jnp.float32
BF16 = jnp.bfloat16

NORM_EPS = 1e-6
NEG_INF = -1e30
CHUNK = 64
HEAD_DIM = 128
ROT_DIM = HEAD_DIM // 4
ROPE_THETA = 500000.0
TOP_K = 4
SWIGLU_ALPHA = 1.702
SWIGLU_LIMIT = 7.0
LANES = 128
VMEM_LIMIT_BYTES = 60 * 1024 * 1024


def _pick_tile(n, target, quantum):
    if n <= target:
        return n
    t = (target // quantum) * quantum
    while t > quantum and n % t:
        t -= quantum
    assert n % t == 0, (n, target, quantum)
    return t


def _params(*sem):
    return pltpu.CompilerParams(dimension_semantics=sem, vmem_limit_bytes=VMEM_LIMIT_BYTES)


def _sigmoid(x):
    return 1.0 / (1.0 + jnp.exp(-x))


def _softplus(x):
    return jnp.maximum(x, 0.0) + jnp.log(1.0 + jnp.exp(-jnp.abs(x)))


def _dot(a, b):
    return jnp.dot(a.astype(BF16), b.astype(BF16), preferred_element_type=F32)


def _dot_nt(a, b):
    return lax.dot_general(a.astype(BF16), b.astype(BF16), (((1,), (1,)), ((), ())),
                           preferred_element_type=F32)


def _dot_tn(a, b):
    return lax.dot_general(a.astype(BF16), b.astype(BF16), (((0,), (0,)), ((), ())),
                           preferred_element_type=F32)


def _rmsnorm_kernel(x_ref, g_ref, o_ref):
    x = x_ref[...]
    y = x * lax.rsqrt(jnp.mean(x * x, axis=-1, keepdims=True) + NORM_EPS)
    o_ref[...] = (y * g_ref[...]).astype(o_ref.dtype)


def _rmsnorm(x2d, g, out_dtype):
    n, d = x2d.shape
    tm = _pick_tile(n, 512, 8)
    return pl.pallas_call(
        _rmsnorm_kernel,
        grid=(n // tm,),
        in_specs=[pl.BlockSpec((tm, d), lambda i: (i, 0)), pl.BlockSpec((1, d), lambda i: (0, 0))],
        out_specs=pl.BlockSpec((tm, d), lambda i: (i, 0)),
        out_shape=jax.ShapeDtypeStruct((n, d), out_dtype),
        compiler_params=_params("parallel"),
        name="rmsnorm",
    )(x2d, g.reshape(1, d))


def _matmul_kernel(*refs, has_res):
    if has_res:
        x_ref, w_ref, r_ref, o_ref, wb_ref = refs
    else:
        x_ref, w_ref, o_ref, wb_ref = refs

    @pl.when(pl.program_id(1) == 0)
    def _():
        wb_ref[...] = w_ref[...].astype(BF16)

    acc = jnp.dot(x_ref[...], wb_ref[...], preferred_element_type=F32)
    if has_res:
        acc = acc + r_ref[...]
    o_ref[...] = acc.astype(o_ref.dtype)


def _matmul(x, w, col0, ncols, out_dtype, res=None, name="matmul"):
    m, k = x.shape
    tn = _pick_tile(ncols, 1024, LANES)
    tm = _pick_tile(m, MATMUL_ROWS, 8)
    assert col0 % tn == 0
    cb = col0 // tn
    in_specs = [pl.BlockSpec((tm, k), lambda j, i: (i, 0)),
                pl.BlockSpec((k, tn), lambda j, i: (0, cb + j))]
    args = [x, w]
    if res is not None:
        in_specs.append(pl.BlockSpec((tm, tn), lambda j, i: (i, j)))
        args.append(res)
    return pl.pallas_call(
        functools.partial(_matmul_kernel, has_res=res is not None),
        grid=(ncols // tn, m // tm),
        in_specs=in_specs,
        out_specs=pl.BlockSpec((tm, tn), lambda j, i: (i, j)),
        out_shape=jax.ShapeDtypeStruct((m, ncols), out_dtype),
        scratch_shapes=[pltpu.VMEM((k, tn), BF16)],
        compiler_params=_params("parallel", "arbitrary"),
        name=name,
    )(*args)


def _gdn_kernel(q_ref, k_ref, v_ref, z_ref, ba_ref, cwq_ref, cwk_ref, cwv_ref, alog_ref, dtb_ref,
                ng_ref, o_ref, state_ref, tq_ref, tk_ref, tv_ref, eq_ref, ek_ref, ev_ref, *, hg):
    C = CHUNK
    D = HEAD_DIM

    @pl.when(pl.program_id(2) == 0)
    def _():
        state_ref[...] = jnp.zeros_like(state_ref)
        tq_ref[...] = jnp.zeros_like(tq_ref)
        tk_ref[...] = jnp.zeros_like(tk_ref)
        tv_ref[...] = jnp.zeros_like(tv_ref)

    def conv_silu(u_ref, tail_ref, ext_ref, w_ref):
        u = u_ref[0]
        ext_ref[0:8, :] = tail_ref[...]
        ext_ref[8:8 + C, :] = u
        w = w_ref[...]
        y = ext_ref[5:5 + C, :] * w[0:1, :]
        y = y + ext_ref[6:6 + C, :] * w[1:2, :]
        y = y + ext_ref[7:7 + C, :] * w[2:3, :]
        y = y + u * w[3:4, :]
        tail_ref[...] = u[C - 8:C, :]
        return y * _sigmoid(y)

    def l2norm_heads(x, scale):
        parts = []
        for i in range(hg):
            xi = x[:, i * D:(i + 1) * D]
            parts.append(xi * (lax.rsqrt(jnp.sum(xi * xi, axis=-1, keepdims=True) + NORM_EPS) * scale))
        return jnp.concatenate(parts, axis=1)

    qc = l2norm_heads(conv_silu(q_ref, tq_ref, eq_ref, cwq_ref), D ** -0.5)
    kc = l2norm_heads(conv_silu(k_ref, tk_ref, ek_ref, cwk_ref), 1.0)
    vc = conv_silu(v_ref, tv_ref, ev_ref, cwv_ref)

    G = GDN_GROUP
    R = G * C
    shift_c = C.bit_length() - 1

    ba = ba_ref[0]
    beta_full = _sigmoid(ba)
    g_full = -jnp.exp(alog_ref[0]) * _softplus(ba + dtb_ref[0])
    r64 = lax.broadcasted_iota(jnp.int32, (C, C), 0)
    c64 = lax.broadcasted_iota(jnp.int32, (C, C), 1)
    gcum_full = jnp.dot((r64 >= c64).astype(F32), g_full, preferred_element_type=F32,
                        precision=lax.Precision.HIGHEST)
    gcum_t = gcum_full.T
    row = lax.broadcasted_iota(jnp.int32, (R, R), 0)
    col = lax.broadcasted_iota(jnp.int32, (R, R), 1)
    same_head = jnp.right_shift(row, shift_c) == jnp.right_shift(col, shift_c)
    tril = same_head & (row >= col)
    strict = same_head & (row > col)
    lane2 = lax.broadcasted_iota(jnp.int32, (1, 2 * D), 1)
    r2 = lax.broadcasted_iota(jnp.int32, (2 * D, 2 * D), 0)
    c2 = lax.broadcasted_iota(jnp.int32, (2 * D, 2 * D), 1)
    pair_block = (r2 < D) == (c2 < D)

    def lanes_of(x, i0):
        return jnp.concatenate([x[i0 * C:(i0 + 1) * C], x[(i0 + 1) * C:(i0 + 2) * C]], axis=1)

    for h0 in range(0, hg, G):
        heads = range(h0, h0 + G)

        def rows_of(x):
            return jnp.concatenate([x[:, i * D:(i + 1) * D] for i in heads], axis=0)

        beta_r = jnp.concatenate([beta_full[:, i:i + 1] for i in heads], axis=0)
        gc_r = jnp.concatenate([gcum_full[:, hg + i:hg + i + 1] for i in heads], axis=0)
        gc_l = jnp.concatenate([gcum_t[hg + i:hg + i + 1, :] for i in heads], axis=1)
        g_last = [gcum_full[C - 1:C, hg + i:hg + i + 1] for i in heads]
        gl_r = jnp.concatenate([jnp.broadcast_to(g, (C, 1)) for g in g_last], axis=0)

        qn, kn, v_r = rows_of(qc), rows_of(kc), rows_of(vc)
        decay = jnp.where(tril, jnp.exp(jnp.where(tril, gc_r - gc_l, 0.0)), 0.0)
        eg = jnp.exp(gc_r)
        kb = kn * beta_r
        gram = _dot_nt(jnp.concatenate([kb, qn], axis=0), kn)
        nmat = jnp.where(strict, gram[:R] * decay, 0.0)
        qk = gram[R:] * decay
        kbg_r = kb * eg
        qg_r = qn * eg
        kdec_r = kn * jnp.exp(gl_r - gc_r)
        states, ks_rows, o1_rows = [], [], []
        for pr in range(G // 2):
            s = state_ref[h0 // 2 + pr]
            xs = _dot(jnp.concatenate([lanes_of(kbg_r, 2 * pr), lanes_of(qg_r, 2 * pr)], axis=0), s)
            states.append(s)
            ks_rows += [xs[:C, :D], xs[:C, D:]]
            o1_rows += [xs[C:, :D], xs[C:, D:]]
        y = v_r * beta_r - jnp.concatenate(ks_rows, axis=0)
        m = -nmat
        for _ in range(shift_c - 1):
            mb = m.astype(BF16)
            x = jnp.dot(mb, jnp.concatenate([mb, y.astype(BF16)], axis=1), preferred_element_type=F32)
            m = x[:, :R]
            y = y + x[:, R:]
        vnew_r = y + _dot(m, y)
        for pr in range(G // 2):
            upd = _dot_tn(lanes_of(kdec_r, 2 * pr), lanes_of(vnew_r, 2 * pr))
            gl_lanes = jnp.where(lane2 < D, jnp.exp(g_last[2 * pr]), jnp.exp(g_last[2 * pr + 1]))
            state_ref[h0 // 2 + pr] = states[pr] * gl_lanes + jnp.where(pair_block, upd, 0.0)
        o = jnp.concatenate(o1_rows, axis=0) + _dot(qk, vnew_r)
        on = o * lax.rsqrt(jnp.mean(o * o, axis=-1, keepdims=True) + NORM_EPS) * ng_ref[...]
        z_r = rows_of(z_ref[0])
        out = (on * (z_r * _sigmoid(z_r))).astype(o_ref.dtype)
        for j, i in enumerate(heads):
            o_ref[0, :, i * D:(i + 1) * D] = out[j * C:(j + 1) * C]


def _gdn(proj_a, ba, conv_w, alog_p, dtb_p, norm_g, heads, hg):
    b, s, _ = proj_a.shape
    assert hg % 2 == 0 and heads % hg == 0
    ngroups = heads // hg
    w = hg * HEAD_DIM
    nc = s // CHUNK

    def colspec(off):
        return pl.BlockSpec((1, CHUNK, w), lambda bi, gi, ci, off=off: (bi, ci, off * ngroups + gi))

    def cwspec(off):
        return pl.BlockSpec((4, w), lambda bi, gi, ci, off=off: (0, off * ngroups + gi))

    vec = pl.BlockSpec((1, 1, LANES), lambda bi, gi, ci: (gi, 0, 0))
    return pl.pallas_call(
        functools.partial(_gdn_kernel, hg=hg),
        grid=(b, ngroups, nc),
        in_specs=[colspec(0), colspec(1), colspec(2), colspec(3),
                  pl.BlockSpec((1, CHUNK, LANES), lambda bi, gi, ci: (bi, ci, gi)),
                  cwspec(0), cwspec(1), cwspec(2), vec, vec,
                  pl.BlockSpec((1, HEAD_DIM), lambda bi, gi, ci: (0, 0))],
        out_specs=pl.BlockSpec((1, CHUNK, w), lambda bi, gi, ci: (bi, ci, gi)),
        out_shape=jax.ShapeDtypeStruct((b, s, heads * HEAD_DIM), BF16),
        scratch_shapes=[pltpu.VMEM((hg // 2, 2 * HEAD_DIM, 2 * HEAD_DIM), F32)]
        + [pltpu.VMEM((8, w), F32)] * 3 + [pltpu.VMEM((8 + CHUNK, w), F32)] * 3,
        compiler_params=_params("parallel", "parallel", "arbitrary"),
        name="gated_delta_rule",
    )(proj_a, proj_a, proj_a, proj_a, ba, conv_w, conv_w, conv_w, alog_p, dtb_p,
      norm_g.reshape(1, HEAD_DIM))


def _rope_table_kernel(pos_ref, freq_ref, cos_ref, sin_ref):
    ang = pos_ref[...] * freq_ref[...]
    lane = lax.broadcasted_iota(jnp.int32, ang.shape, 1)
    sin = jnp.sin(ang)
    cos_ref[...] = jnp.cos(ang)
    sin_ref[...] = jnp.where(lane < ROT_DIM // 2, -sin, sin)


def _rope_table(pos_f):
    n = pos_f.shape[0]
    ts = _pick_tile(n, 1024, 8)
    half = ROT_DIM // 2
    inv_freq = np.power(np.float32(ROPE_THETA),
                        -np.arange(half, dtype=np.float32) * np.float32(2.0) / np.float32(ROT_DIM))
    freq = np.zeros((1, LANES), np.float32)
    freq[0, :half] = inv_freq
    freq[0, half:ROT_DIM] = inv_freq
    row = pl.BlockSpec((ts, LANES), lambda i: (i, 0))
    return pl.pallas_call(
        _rope_table_kernel,
        grid=(n // ts,),
        in_specs=[pl.BlockSpec((ts, 1), lambda i: (i, 0)), pl.BlockSpec((1, LANES), lambda i: (0, 0))],
        out_specs=[row, row],
        out_shape=[jax.ShapeDtypeStruct((n, LANES), F32)] * 2,
        compiler_params=_params("parallel"),
        name="rope_table",
    )(pos_f, jnp.asarray(freq))


def _proj_t_kernel(*refs, shift, rope, q_tiles, scales):
    refs = list(refs)
    x_ref, w_ref = refs[:2]
    w2_ref = refs[2] if shift else None
    o_ref, wb_ref = refs[-2 - bool(shift)], refs[-1 - bool(shift)]
    tn = wb_ref.shape[1]

    @pl.when(pl.program_id(1) == 0)
    def _():
        if shift:
            ws_ref = refs[-1]
            ws_ref[:tn - shift, :] = w_ref[shift:, :]
            ws_ref[tn - shift:, :] = w2_ref[:shift, :]
            wb_ref[...] = ws_ref[...].T.astype(BF16)
        else:
            wb_ref[...] = w_ref[...].T.astype(BF16)

    acc = jnp.dot(x_ref[...], wb_ref[...], preferred_element_type=F32)
    if not rope:
        o_ref[...] = acc.astype(o_ref.dtype)
        return
    cos_ref, sin_ref, g_ref = refs[2 + bool(shift):5 + bool(shift)]
    is_q = pl.program_id(0) < q_tiles
    g = jnp.where(is_q, g_ref[0:1, :], g_ref[1:2, :])
    scale = jnp.where(is_q, scales[0], scales[1])
    cos = cos_ref[...]
    sin_signed = sin_ref[...]
    lane = lax.broadcasted_iota(jnp.int32, cos.shape, 1)
    first_half = lane < ROT_DIM // 2
    for j in range(tn // HEAD_DIM):
        sl = slice(j * HEAD_DIM, (j + 1) * HEAD_DIM)
        x = acc[:, sl]
        xn = x * (lax.rsqrt(jnp.mean(x * x, axis=-1, keepdims=True) + NORM_EPS) * scale) * g
        partner = jnp.where(first_half, pltpu.roll(xn, HEAD_DIM - ROT_DIM // 2, 1),
                            pltpu.roll(xn, ROT_DIM // 2, 1))
        o_ref[:, sl] = (xn * cos + partner * sin_signed).astype(o_ref.dtype)


def _proj_t(x, wt, row0, nrows, out_dtype, rope=None, name="proj"):
    m, k = x.shape
    shift = row0 % LANES
    tn = _pick_tile(rope[4] if rope else nrows, 1024, LANES)
    tm = _pick_tile(m, MATMUL_ROWS // 2 if rope else MATMUL_ROWS, 8)
    assert (row0 - shift) % tn == 0 and nrows % tn == 0
    cb = (row0 - shift) // tn
    per = tn // LANES
    in_specs = [pl.BlockSpec((tm, k), lambda j, i: (i, 0)),
                pl.BlockSpec((tn, k), lambda j, i: (cb + j, 0))]
    args = [x, wt]
    scratch = [pltpu.VMEM((k, tn), BF16)]
    if shift:
        in_specs.append(pl.BlockSpec((LANES, k), lambda j, i: ((cb + j + 1) * per, 0)))
        args.append(wt)
        scratch.append(pltpu.VMEM((tn, k), F32))
    q_tiles, scales = 0, (1.0, 1.0)
    if rope:
        cos, sin_signed, gains, scales, q_cols = rope
        q_tiles = q_cols // tn
        in_specs += [pl.BlockSpec((tm, LANES), lambda j, i: (i, 0)),
                     pl.BlockSpec((tm, LANES), lambda j, i: (i, 0)),
                     pl.BlockSpec((2, HEAD_DIM), lambda j, i: (0, 0))]
        args += [cos, sin_signed, gains]
    return pl.pallas_call(
        functools.partial(_proj_t_kernel, shift=shift, rope=rope is not None, q_tiles=q_tiles,
                          scales=tuple(scales)),
        grid=(nrows // tn, m // tm),
        in_specs=in_specs,
        out_specs=pl.BlockSpec((tm, tn), lambda j, i: (i, j)),
        out_shape=jax.ShapeDtypeStruct((m, nrows), out_dtype),
        scratch_shapes=scratch,
        compiler_params=_params("parallel", "arbitrary"),
        name=name,
    )(*args)


def _diff_attn_kernel(q_ref, k_ref, v_ref, lq1_ref, lk1_ref, lq2_ref, lk2_ref, sg_ref, o_ref,
                      acc_ref, m_ref, l_ref, *, tq, lam_init):
    D = HEAD_DIM
    qi = pl.program_id(2)
    q1 = q_ref[0, :, :D]
    q2 = q_ref[0, :, D:]
    acc_ref[...] = jnp.zeros_like(acc_ref)
    m_ref[...] = jnp.full_like(m_ref, NEG_INF)
    l_ref[...] = jnp.zeros_like(l_ref)

    def lanes(x, width):
        return jnp.concatenate([x] * (width // LANES), axis=1)

    def step(start, mask):
        kblk = k_ref[0, pl.ds(start, tq), :]
        vblk = v_ref[0, pl.ds(start, tq), :].astype(BF16)
        s = jnp.concatenate([_dot_nt(q1, kblk[:, :D]), _dot_nt(q2, kblk[:, D:])], axis=0)
        if mask is not None:
            s = jnp.where(mask, s, NEG_INF)
        m_prev = m_ref[...]
        m_next = jnp.maximum(m_prev, jnp.max(s, axis=-1, keepdims=True))
        alpha = jnp.exp2(m_prev - m_next)
        p = jnp.exp2(s - lanes(m_next, tq))
        l_ref[...] = alpha * l_ref[...] + jnp.sum(p, axis=-1, keepdims=True)
        acc_ref[...] = lanes(alpha, 2 * D) * acc_ref[...] + jnp.dot(p.astype(BF16), vblk,
                                                                    preferred_element_type=F32)
        m_ref[...] = m_next

    def body(j, carry):
        step(pl.multiple_of(j * tq, tq), None)
        return carry

    lax.fori_loop(0, qi, body, 0)
    shift_c = CHUNK.bit_length() - 1
    row = lax.broadcasted_iota(jnp.int32, (2 * tq, tq), 0)
    row = jnp.right_shift(jnp.where(row >= tq, row - tq, row), shift_c)
    col = jnp.right_shift(lax.broadcasted_iota(jnp.int32, (2 * tq, tq), 1), shift_c)
    step(pl.multiple_of(qi * tq, tq), row >= col)

    lam = (jnp.exp(jnp.sum(lq1_ref[...] * lk1_ref[...], axis=-1, keepdims=True))
           - jnp.exp(jnp.sum(lq2_ref[...] * lk2_ref[...], axis=-1, keepdims=True)) + lam_init)
    a = acc_ref[...] / lanes(l_ref[...], 2 * D)
    o = a[:tq] - lam * a[tq:]
    on = o * lax.rsqrt(jnp.mean(o * o, axis=-1, keepdims=True) + NORM_EPS)
    o_ref[0] = (on * sg_ref[...] * (1.0 - lam_init)).astype(o_ref.dtype)


def _diff_attn(qk, v_src, v_col_block, lq1, lk1, lq2, lk2, subln_g, heads, lam_init):
    b, s, _ = qk.shape
    hw = 2 * HEAD_DIM
    tq = _pick_tile(s, ATTN_BLOCK, CHUNK)
    vec = pl.BlockSpec((1, HEAD_DIM), lambda bi, hi, qi: (0, 0))
    return pl.pallas_call(
        functools.partial(_diff_attn_kernel, tq=tq, lam_init=lam_init),
        grid=(b, heads, s // tq),
        in_specs=[pl.BlockSpec((1, tq, hw), lambda bi, hi, qi: (bi, qi, hi)),
                  pl.BlockSpec((1, s, hw), lambda bi, hi, qi: (bi, 0, heads + hi)),
                  pl.BlockSpec((1, s, hw), lambda bi, hi, qi: (bi, 0, v_col_block + hi)),
                  vec, vec, vec, vec,
                  pl.BlockSpec((1, hw), lambda bi, hi, qi: (0, 0))],
        out_specs=pl.BlockSpec((1, tq, hw), lambda bi, hi, qi: (bi, qi, hi)),
        out_shape=jax.ShapeDtypeStruct((b, s, heads * hw), BF16),
        scratch_shapes=[pltpu.VMEM((2 * tq, hw), F32)] + [pltpu.VMEM((2 * tq, LANES), F32)] * 2,
        compiler_params=_params("parallel", "parallel", "arbitrary"),
        name="diff_attention",
    )(qk, qk, v_src, lq1.reshape(1, -1), lk1.reshape(1, -1), lq2.reshape(1, -1), lk2.reshape(1, -1),
      subln_g.reshape(1, hw))


def _merge_kernel(oa_ref, ob_ref, wa_ref, wb_ref, ga_ref, gb_ref, ba_ref, bb_ref, o_ref, wa_s, wb_s):
    @pl.when(pl.program_id(1) == 0)
    def _():
        wa_s[...] = wa_ref[...].astype(BF16)
        wb_s[...] = wb_ref[...].astype(BF16)

    ya = jnp.dot(oa_ref[...], wa_s[...], preferred_element_type=F32)
    yb = jnp.dot(ob_ref[...], wb_s[...], preferred_element_type=F32)
    merged = _sigmoid(ga_ref[...] + ba_ref[...]) * ya + _sigmoid(gb_ref[...] + bb_ref[...]) * yb
    o_ref[...] = merged.astype(o_ref.dtype)


def _merge(oa, ob, w_a, w_b, proj_b, gate_col0, b_gate, d):
    m, ka = oa.shape
    kb = ob.shape[1]
    tn = _pick_tile(d, 512, LANES)
    tm = _pick_tile(m, MATMUL_ROWS, 8)
    assert gate_col0 % tn == 0
    ga = gate_col0 // tn
    gb = (gate_col0 + d) // tn
    nb = d // tn
    bg = b_gate.reshape(1, 2 * d)
    return pl.pallas_call(
        _merge_kernel,
        grid=(nb, m // tm),
        in_specs=[pl.BlockSpec((tm, ka), lambda j, i: (i, 0)),
                  pl.BlockSpec((tm, kb), lambda j, i: (i, 0)),
                  pl.BlockSpec((ka, tn), lambda j, i: (0, j)),
                  pl.BlockSpec((kb, tn), lambda j, i: (0, j)),
                  pl.BlockSpec((tm, tn), lambda j, i: (i, ga + j)),
                  pl.BlockSpec((tm, tn), lambda j, i: (i, gb + j)),
                  pl.BlockSpec((1, tn), lambda j, i: (0, j)),
                  pl.BlockSpec((1, tn), lambda j, i: (0, nb + j))],
        out_specs=pl.BlockSpec((tm, tn), lambda j, i: (i, j)),
        out_shape=jax.ShapeDtypeStruct((m, d), BF16),
        scratch_shapes=[pltpu.VMEM((ka, tn), BF16), pltpu.VMEM((kb, tn), BF16)],
        compiler_params=_params("parallel", "arbitrary"),
        name="branch_merge",
    )(oa, ob, w_a, w_b, proj_b, proj_b, bg, bg)


def _router_kernel(x_ref, g_ref, wr_ref, br_ref, h_ref, ti_ref, tw_ref, rk_ref, cnt_ref, carry_ref,
                   *, n_experts):
    @pl.when(pl.program_id(0) == 0)
    def _():
        carry_ref[...] = jnp.zeros_like(carry_ref)

    x = x_ref[...]
    h = x * lax.rsqrt(jnp.mean(x * x, axis=-1, keepdims=True) + NORM_EPS) * g_ref[...]
    h_ref[...] = h
    logits = jnp.dot(h, wr_ref[...], preferred_element_type=F32,
                     precision=lax.Precision.HIGHEST) + br_ref[...]
    tm = x.shape[0]
    lane = lax.broadcasted_iota(jnp.int32, (tm, LANES), 1)
    lane_f = lane.astype(F32)
    cur = jnp.where(lane < n_experts, logits, -jnp.inf)
    vals, hots = [], []
    ti = jnp.zeros((tm, LANES), F32)
    for kk in range(TOP_K):
        mx = jnp.max(cur, axis=-1, keepdims=True)
        idx = jnp.min(jnp.where(cur == mx, lane_f, float(LANES)), axis=-1, keepdims=True)
        hot = lane_f == idx
        vals.append(mx)
        hots.append(hot)
        ti = jnp.where(lane == kk, idx, ti)
        cur = jnp.where(hot, -jnp.inf, cur)
    exps = [jnp.exp(v - vals[0]) for v in vals]
    denom = exps[0] + exps[1] + exps[2] + exps[3]
    tw = jnp.zeros((tm, LANES), F32)
    for kk in range(TOP_K):
        tw = jnp.where(lane == kk, exps[kk] / denom, tw)
    multihot = jnp.zeros((tm, LANES), F32)
    for hot in hots:
        multihot = multihot + hot.astype(F32)
    r = lax.broadcasted_iota(jnp.int32, (tm, tm), 0)
    c = lax.broadcasted_iota(jnp.int32, (tm, tm), 1)
    prefix = jnp.dot((r > c).astype(BF16), multihot.astype(BF16), preferred_element_type=F32)
    prefix = prefix + carry_ref[...]
    rk = jnp.zeros((tm, LANES), F32)
    for kk in range(TOP_K):
        rk_k = jnp.sum(jnp.where(hots[kk], prefix, 0.0), axis=-1, keepdims=True)
        rk = jnp.where(lane == kk, rk_k, rk)
    carry = carry_ref[...] + jnp.sum(multihot, axis=0, keepdims=True)
    carry_ref[...] = carry
    cnt_ref[...] = carry.astype(jnp.int32)
    ti_ref[...] = ti.astype(jnp.int32)
    tw_ref[...] = tw
    rk_ref[...] = rk.astype(jnp.int32)


def _router(x1, g, w_router, b_router):
    n, d = x1.shape
    e = w_router.shape[1]
    tm = _pick_tile(n, 512, 8)
    wr = jnp.zeros((d, LANES), F32).at[:, :e].set(w_router)
    br = jnp.zeros((1, LANES), F32).at[0, :e].set(b_router)
    row = pl.BlockSpec((tm, LANES), lambda i: (i, 0))
    return pl.pallas_call(
        functools.partial(_router_kernel, n_experts=e),
        grid=(n // tm,),
        in_specs=[pl.BlockSpec((tm, d), lambda i: (i, 0)),
                  pl.BlockSpec((1, d), lambda i: (0, 0)),
                  pl.BlockSpec((d, LANES), lambda i: (0, 0)),
                  pl.BlockSpec((1, LANES), lambda i: (0, 0))],
        out_specs=[pl.BlockSpec((tm, d), lambda i: (i, 0)), row, row, row,
                   pl.BlockSpec((1, LANES), lambda i: (0, 0))],
        out_shape=[jax.ShapeDtypeStruct((n, d), F32),
                   jax.ShapeDtypeStruct((n, LANES), jnp.int32),
                   jax.ShapeDtypeStruct((n, LANES), F32),
                   jax.ShapeDtypeStruct((n, LANES), jnp.int32),
                   jax.ShapeDtypeStruct((1, LANES), jnp.int32)],
        scratch_shapes=[pltpu.VMEM((1, LANES), F32)],
        compiler_params=_params("arbitrary"),
        name="router_topk",
    )(x1, g.reshape(1, d), wr, br)


def _row_copy(src_ref, src_row, dst_ref, dst_row, sem):
    return pltpu.make_async_copy(src_ref.at[pl.ds(src_row, 1)], dst_ref.at[pl.ds(dst_row, 1)], sem)


def _expert_kernel(se_ref, sb0_ref, sn_ref, na_ref, *refs, rows, sbk, nf, nsuper, nblk):
    tok_refs = refs[:sbk]
    tokn_refs = refs[sbk:2 * sbk]
    (h_ref, wg_ref, wu_ref, wd_ref, bg_ref, bu_ref, bd_ref, ys_ref,
     xg_ref, xb_ref, acc_ref, wgb_ref, wub_ref, wdb_ref, gsem, osem) = refs[2 * sbk:]
    s = pl.program_id(0)
    f = pl.program_id(1)
    nsub = sn_ref[s]

    def gather(t_refs, count):
        for j in range(sbk):
            @pl.when(j < count)
            def _(j=j):
                def body(r, carry):
                    _row_copy(h_ref, t_refs[j][0, 0, r], xg_ref, j * rows + r, gsem).start()
                    return carry
                lax.fori_loop(0, rows, body, 0, unroll=8)

    def out_copy(j, blk0):
        return pltpu.make_async_copy(acc_ref.at[pl.ds(j * rows, rows)],
                                     ys_ref.at[pl.ds((blk0 + j) * rows, rows)], osem)

    n_act = na_ref[0]
    blocks_used = na_ref[1]

    def tail_block(j):
        return blocks_used + (s - n_act) * sbk + j

    def zero_copy(j):
        return pltpu.make_async_copy(acc_ref.at[pl.ds(0, rows)],
                                     ys_ref.at[pl.ds(tail_block(j) * rows, rows)], osem)

    @pl.when((f == 0) & (s == 0))
    def _():
        gather(tok_refs, nsub)

    @pl.when(f == 0)
    def _():
        for j in range(sbk):
            @pl.when(j < nsub)
            def _(j=j):
                pltpu.make_async_copy(h_ref.at[pl.ds(0, rows)], xg_ref.at[pl.ds(j * rows, rows)], gsem).wait()
        for j in range(sbk):
            @pl.when(j < nsub)
            def _(j=j):
                xb_ref[pl.ds(j * rows, rows), :] = xg_ref[pl.ds(j * rows, rows), :].astype(BF16)

        @pl.when(s + 1 < nsuper)
        def _():
            gather(tokn_refs, sn_ref[s + 1])

        @pl.when(s > 0)
        def _():
            for j in range(sbk):
                @pl.when(j < sn_ref[s - 1])
                def _(j=j):
                    out_copy(j, sb0_ref[s - 1]).wait()

        for j in range(sbk):
            @pl.when(j < nsub)
            def _(j=j):
                acc_ref[pl.ds(j * rows, rows), :] = jnp.broadcast_to(bd_ref[0], (rows, acc_ref.shape[1]))

        @pl.when(s == n_act)
        def _():
            acc_ref[pl.ds(0, rows), :] = jnp.zeros((rows, acc_ref.shape[1]), F32)

        for j in range(sbk):
            @pl.when((s >= n_act) & (tail_block(j) < nblk))
            def _(j=j):
                zero_copy(j).start()

    @pl.when(nsub > 0)
    def _():
        wgb_ref[...] = wg_ref[0].astype(BF16)
        wub_ref[...] = wu_ref[0].astype(BF16)
        wdb_ref[...] = wd_ref[0].astype(BF16)

    for j in range(sbk):
        @pl.when(j < nsub)
        def _(j=j):
            xb = xb_ref[pl.ds(j * rows, rows), :]
            g = jnp.dot(xb, wgb_ref[...], preferred_element_type=F32) + bg_ref[0]
            u = jnp.dot(xb, wub_ref[...], preferred_element_type=F32) + bu_ref[0]
            gate = jnp.minimum(g, SWIGLU_LIMIT)
            up = jnp.clip(u, -SWIGLU_LIMIT, SWIGLU_LIMIT)
            act = (up + 1.0) * gate * _sigmoid(SWIGLU_ALPHA * gate)
            acc_ref[pl.ds(j * rows, rows), :] += jnp.dot(act.astype(BF16), wdb_ref[...],
                                                         preferred_element_type=F32)

    @pl.when(f == nf - 1)
    def _():
        for j in range(sbk):
            @pl.when((s >= n_act) & (tail_block(j) < nblk))
            def _(j=j):
                zero_copy(j).wait()

        for j in range(sbk):
            @pl.when(j < nsub)
            def _(j=j):
                out_copy(j, sb0_ref[s]).start()

        @pl.when(s == nsuper - 1)
        def _():
            for j in range(sbk):
                @pl.when(j < nsub)
                def _(j=j):
                    out_copy(j, sb0_ref[s]).wait()


def _experts(h, slot_token, sb_expert, sb_blk0, sb_nsub, n_active, w_gate_up, b_gate_up, w_down, b_down,
             rows, sbk):
    n, d = h.shape
    p = slot_token.shape[0]
    e, _, dff2 = w_gate_up.shape
    dff = dff2 // 2
    tf = _pick_tile(dff, EXPERT_FF_TILE, LANES)
    nf = dff // tf
    nblk = p // rows
    nsuper = sb_expert.shape[0]
    tok3 = slot_token.reshape(nblk, 1, rows)

    def frozen_f(s, f, na):
        return jnp.where(s < na[0], f, nf - 1)

    def tok_spec(j, ahead):
        def index(s, f, se, sb0, sn, na):
            si = jnp.minimum(s + ahead, nsuper - 1)
            return (jnp.minimum(sb0[si] + j, nblk - 1), 0, 0)
        return pl.BlockSpec((1, 1, rows), index, memory_space=pltpu.SMEM)

    grid_spec = pltpu.PrefetchScalarGridSpec(
        num_scalar_prefetch=4,
        grid=(nsuper, nf),
        in_specs=[tok_spec(j, 0) for j in range(sbk)] + [tok_spec(j, 1) for j in range(sbk)] + [
            pl.BlockSpec(memory_space=pl.ANY),
            pl.BlockSpec((1, d, tf), lambda s, f, se, sb0, sn, na: (se[s], 0, frozen_f(s, f, na))),
            pl.BlockSpec((1, d, tf), lambda s, f, se, sb0, sn, na: (se[s], 0, nf + frozen_f(s, f, na))),
            pl.BlockSpec((1, tf, d), lambda s, f, se, sb0, sn, na: (se[s], frozen_f(s, f, na), 0)),
            pl.BlockSpec((1, 1, tf), lambda s, f, se, sb0, sn, na: (se[s], 0, frozen_f(s, f, na))),
            pl.BlockSpec((1, 1, tf), lambda s, f, se, sb0, sn, na: (se[s], 0, nf + frozen_f(s, f, na))),
            pl.BlockSpec((1, 1, d), lambda s, f, se, sb0, sn, na: (se[s], 0, 0))],
        out_specs=pl.BlockSpec(memory_space=pl.ANY),
        scratch_shapes=[pltpu.VMEM((sbk * rows, d), F32), pltpu.VMEM((sbk * rows, d), BF16),
                        pltpu.VMEM((sbk * rows, d), F32),
                        pltpu.VMEM((d, tf), BF16), pltpu.VMEM((d, tf), BF16), pltpu.VMEM((tf, d), BF16),
                        pltpu.SemaphoreType.DMA(()), pltpu.SemaphoreType.DMA(())],
    )
    return pl.pallas_call(
        functools.partial(_expert_kernel, rows=rows, sbk=sbk, nf=nf, nsuper=nsuper, nblk=nblk),
        grid_spec=grid_spec,
        out_shape=jax.ShapeDtypeStruct((p, d), F32),
        compiler_params=_params("arbitrary", "arbitrary"),
        name="expert_mlp",
    )(sb_expert, sb_blk0, sb_nsub, n_active, *([tok3] * (2 * sbk)), h, w_gate_up, w_gate_up, w_down,
      b_gate_up.reshape(e, 1, dff2), b_gate_up.reshape(e, 1, dff2), b_down.reshape(e, 1, d))


def _combine_kernel(dest_ref, destn_ref, x_ref, tw_ref, ys_ref, o_ref, buf_ref, sem, *, tm, nt):
    i = pl.program_id(0)
    slot = i % 2

    def gather(d_ref, s):
        def body(t, carry):
            for kk in range(TOP_K):
                _row_copy(ys_ref, d_ref[0, 0, t * TOP_K + kk], buf_ref.at[s, kk], t, sem.at[s]).start()
            return carry
        lax.fori_loop(0, tm, body, 0, unroll=2)

    @pl.when(i == 0)
    def _():
        gather(dest_ref, 0)

    @pl.when(i + 1 < nt)
    def _():
        gather(destn_ref, 1 - slot)

    for kk in range(TOP_K):
        pltpu.make_async_copy(ys_ref.at[pl.ds(0, tm)], buf_ref.at[slot, kk], sem.at[slot]).wait()
    tw = tw_ref[...]
    y = buf_ref[slot, 0] * tw[:, 0:1]
    for kk in range(1, TOP_K):
        y = y + buf_ref[slot, kk] * tw[:, kk:kk + 1]
    o_ref[...] = x_ref[...] + y


def _combine(x1, topw, dest, ys):
    n, d = x1.shape
    tm = _pick_tile(n, 256, 8)
    nt = n // tm
    dest3 = dest.reshape(nt, 1, tm * TOP_K)
    return pl.pallas_call(
        functools.partial(_combine_kernel, tm=tm, nt=nt),
        grid=(nt,),
        in_specs=[pl.BlockSpec((1, 1, tm * TOP_K), lambda i: (i, 0, 0), memory_space=pltpu.SMEM),
                  pl.BlockSpec((1, 1, tm * TOP_K), lambda i: (jnp.minimum(i + 1, nt - 1), 0, 0),
                               memory_space=pltpu.SMEM),
                  pl.BlockSpec((tm, d), lambda i: (i, 0)),
                  pl.BlockSpec((tm, LANES), lambda i: (i, 0)),
                  pl.BlockSpec(memory_space=pl.ANY)],
        out_specs=pl.BlockSpec((tm, d), lambda i: (i, 0)),
        out_shape=jax.ShapeDtypeStruct((n, d), F32),
        scratch_shapes=[pltpu.VMEM((2, TOP_K, tm, d), F32), pltpu.SemaphoreType.DMA((2,))],
        compiler_params=_params("arbitrary"),
        name="moe_combine",
    )(dest3, dest3, x1, topw, ys)


MATMUL_ROWS = 1024
ATTN_BLOCK = 1024
GDN_HEADS_PER_STEP = 16
GDN_GROUP = 4
EXPERT_ROWS = 512
EXPERT_SUBBLOCKS = 2
EXPERT_FF_TILE = 512


def _layer(x, positions, layer_idx, norm1_g, w_in, b_gate, conv_w, a_log, dt_bias, gdn_norm_g,
           q_norm_g, k_norm_g, lambda_q1, lambda_k1, lambda_q2, lambda_k2, subln_g,
           w_branch_a, w_branch_b, w_out, norm2_g, w_router, b_router,
           w_gate_up, b_gate_up, w_down, b_down):
    b, s, d = x.shape
    n = b * s
    gdn_heads = a_log.shape[0]
    gdn_dim = gdn_heads * HEAD_DIM
    diff_v = w_branch_b.shape[0]
    diff_heads = diff_v // (2 * HEAD_DIM)
    diff_qk = 2 * diff_heads * HEAD_DIM
    off_beta = 4 * gdn_dim
    off_alpha = off_beta + gdn_heads
    off_qb = off_alpha + gdn_heads
    off_gate = off_qb + 2 * diff_qk + diff_v
    assert w_in.shape[1] == off_gate + 2 * d
    n_experts = w_router.shape[1]

    x2 = x.reshape(n, d)
    h = _rmsnorm(x2, norm1_g, BF16)

    wt = jnp.swapaxes(w_in, 0, 1)
    proj_a = _proj_t(h, wt, 0, off_beta, F32, name="proj_gdn")
    hg = min(GDN_HEADS_PER_STEP, gdn_heads)
    ngroups = gdn_heads // hg
    row_pad = jnp.zeros((LANES - 2 * hg, d), F32)
    w_ba = jnp.concatenate(
        [part for gi in range(ngroups) for part in (
            wt[off_beta + gi * hg:off_beta + (gi + 1) * hg],
            wt[off_alpha + gi * hg:off_alpha + (gi + 1) * hg], row_pad)], axis=0)
    ba = _proj_t(h, w_ba, 0, ngroups * LANES, F32, name="proj_beta_alpha")
    cos_t, sin_t = _rope_table(positions.astype(F32).reshape(n, 1))
    qk = _proj_t(h, wt, off_qb, 2 * diff_qk, BF16, name="proj_qk_rope",
                 rope=(cos_t, sin_t, jnp.stack([q_norm_g, k_norm_g]),
                       (HEAD_DIM ** -0.5 * math.log2(math.e), 1.0), diff_qk))
    vg = _proj_t(h, wt, off_qb + 2 * diff_qk, diff_v + 2 * d, F32, name="proj_v_gate")

    alog_p = jnp.zeros((ngroups, 1, LANES), F32).at[:, 0, hg:2 * hg].set(a_log.reshape(ngroups, hg))
    dtb_p = jnp.zeros((ngroups, 1, LANES), F32).at[:, 0, hg:2 * hg].set(dt_bias.reshape(ngroups, hg))
    oa = _gdn(proj_a.reshape(b, s, off_beta), ba.reshape(b, s, ngroups * LANES), conv_w,
              alog_p, dtb_p, gdn_norm_g, gdn_heads, hg)

    lam_init = 0.8 - 0.6 * math.exp(-0.3 * layer_idx)
    ob = _diff_attn(qk.reshape(b, s, 2 * diff_qk), vg.reshape(b, s, diff_v + 2 * d), 0,
                    lambda_q1, lambda_k1, lambda_q2, lambda_k2, subln_g, diff_heads, lam_init)

    merged = _merge(oa.reshape(n, gdn_dim), ob.reshape(n, diff_v), w_branch_a, w_branch_b,
                    vg, diff_v, b_gate, d)
    x1 = _matmul(merged, w_out, 0, d, F32, res=x2, name="out_proj")

    h2, topi, topw, rank, counts = _router(x1, norm2_g, w_router, b_router)
    rows, sbk = EXPERT_ROWS, EXPERT_SUBBLOCKS
    top_e = topi[:, :TOP_K]
    cnt = counts[0, :n_experts]
    nb = (cnt + rows - 1) // rows
    blk_end = jnp.cumsum(nb)
    blk_start = blk_end - nb
    dest = (blk_start[top_e] * rows + rank[:, :TOP_K]).reshape(n * TOP_K)
    p_rows = n * TOP_K + n_experts * rows
    nblk = p_rows // rows
    slot_token = jnp.zeros((p_rows,), jnp.int32).at[dest].set(
        jnp.repeat(jnp.arange(n, dtype=jnp.int32), TOP_K), unique_indices=True, mode="promise_in_bounds")
    ns = (nb + sbk - 1) // sbk
    ns_end = jnp.cumsum(ns)
    ns_start = ns_end - ns
    n_active = ns_end[-1]
    nsuper = nblk // sbk + n_experts
    s_idx = jnp.arange(nsuper, dtype=jnp.int32)
    active = s_idx < n_active
    s_src = jnp.where(active, s_idx, jnp.maximum(n_active - 1, 0))
    sb_expert = jnp.minimum(jnp.sum(s_src[:, None] >= ns_end[None, :], axis=1), n_experts - 1)
    sb_local = s_src - ns_start[sb_expert]
    sb_blk0 = blk_start[sb_expert] + sb_local * sbk
    sb_nsub = jnp.where(active, jnp.clip(nb[sb_expert] - sb_local * sbk, 0, sbk), 0)
    ys = _experts(h2, slot_token, sb_expert.astype(jnp.int32), sb_blk0.astype(jnp.int32),
                  sb_nsub.astype(jnp.int32), jnp.stack([n_active, blk_end[-1]]).astype(jnp.int32),
                  w_gate_up, b_gate_up, w_down, b_down, rows, sbk)
    out = _combine(x1, topw, dest, ys)
    return out.reshape(b, s, d)


def kernel(x, positions, norm1_g, w_in, b_gate, conv_w, a_log, dt_bias, gdn_norm_g, q_norm_g, k_norm_g, lambda_q1, lambda_k1, lambda_q2, lambda_k2, subln_g, w_branch_a, w_branch_b, w_out, norm2_g, w_router, b_router, w_gate_up, b_gate_up, w_down, b_down):
    depth = norm1_g.shape[0]
    for l in range(depth):
        x = _layer(x, positions, l, norm1_g[l], w_in[l], b_gate[l], conv_w[l], a_log[l], dt_bias[l],
                   gdn_norm_g[l], q_norm_g[l], k_norm_g[l], lambda_q1[l], lambda_k1[l], lambda_q2[l],
                   lambda_k2[l], subln_g[l], w_branch_a[l], w_branch_b[l], w_out[l], norm2_g[l],
                   w_router[l], b_router[l], w_gate_up[l], b_gate_up[l], w_down[l], b_down[l])
    return x
```

```python
import functools
import math

import numpy as np
import jax
import jax.numpy as jnp
from jax import lax
from jax.experimental import pallas as pl
from jax.experimental.pallas import tpu as pltpu

F32 = jnp.float32
BF16 = jnp.bfloat16

NORM_EPS = 1e-6
NEG_INF = -1e30
CHUNK = 64
HEAD_DIM = 128
ROT_DIM = HEAD_DIM // 4
ROPE_THETA = 500000.0
TOP_K = 4
SWIGLU_ALPHA = 1.702
SWIGLU_LIMIT = 7.0
LANES = 128
VMEM_LIMIT_BYTES = 60 * 1024 * 1024


def _pick_tile(n, target, quantum):
    if n <= target:
        return n
    t = (target // quantum) * quantum
    while t > quantum and n % t:
        t -= quantum
    assert n % t == 0, (n, target, quantum)
    return t


def _params(*sem):
    return pltpu.CompilerParams(dimension_semantics=sem, vmem_limit_bytes=VMEM_LIMIT_BYTES)


def _sigmoid(x):
    return 1.0 / (1.0 + jnp.exp(-x))


def _softplus(x):
    return jnp.maximum(x, 0.0) + jnp.log(1.0 + jnp.exp(-jnp.abs(x)))


def _dot(a, b):
    return jnp.dot(a.astype(BF16), b.astype(BF16), preferred_element_type=F32)


def _dot_nt(a, b):
    return lax.dot_general(a.astype(BF16), b.astype(BF16), (((1,), (1,)), ((), ())),
                           preferred_element_type=F32)


def _dot_tn(a, b):
    return lax.dot_general(a.astype(BF16), b.astype(BF16), (((0,), (0,)), ((), ())),
                           preferred_element_type=F32)


def _rmsnorm_kernel(x_ref, g_ref, o_ref):
    x = x_ref[...]
    y = x * lax.rsqrt(jnp.mean(x * x, axis=-1, keepdims=True) + NORM_EPS)
    o_ref[...] = (y * g_ref[...]).astype(o_ref.dtype)


def _rmsnorm(x2d, g, out_dtype):
    n, d = x2d.shape
    tm = _pick_tile(n, 512, 8)
    return pl.pallas_call(
        _rmsnorm_kernel,
        grid=(n // tm,),
        in_specs=[pl.BlockSpec((tm, d), lambda i: (i, 0)), pl.BlockSpec((1, d), lambda i: (0, 0))],
        out_specs=pl.BlockSpec((tm, d), lambda i: (i, 0)),
        out_shape=jax.ShapeDtypeStruct((n, d), out_dtype),
        compiler_params=_params("parallel"),
        name="rmsnorm",
    )(x2d, g.reshape(1, d))


def _matmul_kernel(*refs, has_res):
    if has_res:
        x_ref, w_ref, r_ref, o_ref, wb_ref = refs
    else:
        x_ref, w_ref, o_ref, wb_ref = refs

    @pl.when(pl.program_id(1) == 0)
    def _():
        wb_ref[...] = w_ref[...].astype(BF16)

    acc = jnp.dot(x_ref[...], wb_ref[...], preferred_element_type=F32)
    if has_res:
        acc = acc + r_ref[...]
    o_ref[...] = acc.astype(o_ref.dtype)


def _matmul(x, w, col0, ncols, out_dtype, res=None, name="matmul"):
    m, k = x.shape
    tn = _pick_tile(ncols, 1024, LANES)
    tm = _pick_tile(m, MATMUL_ROWS, 8)
    assert col0 % tn == 0
    cb = col0 // tn
    in_specs = [pl.BlockSpec((tm, k), lambda j, i: (i, 0)),
                pl.BlockSpec((k, tn), lambda j, i: (0, cb + j))]
    args = [x, w]
    if res is not None:
        in_specs.append(pl.BlockSpec((tm, tn), lambda j, i: (i, j)))
        args.append(res)
    return pl.pallas_call(
        functools.partial(_matmul_kernel, has_res=res is not None),
        grid=(ncols // tn, m // tm),
        in_specs=in_specs,
        out_specs=pl.BlockSpec((tm, tn), lambda j, i: (i, j)),
        out_shape=jax.ShapeDtypeStruct((m, ncols), out_dtype),
        scratch_shapes=[pltpu.VMEM((k, tn), BF16)],
        compiler_params=_params("parallel", "arbitrary"),
        name=name,
    )(*args)


def _gdn_kernel(q_ref, k_ref, v_ref, z_ref, ba_ref, cwq_ref, cwk_ref, cwv_ref, alog_ref, dtb_ref,
                ng_ref, o_ref, state_ref, tq_ref, tk_ref, tv_ref, eq_ref, ek_ref, ev_ref, *, hg):
    C = CHUNK
    D = HEAD_DIM

    @pl.when(pl.program_id(2) == 0)
    def _():
        state_ref[...] = jnp.zeros_like(state_ref)
        tq_ref[...] = jnp.zeros_like(tq_ref)
        tk_ref[...] = jnp.zeros_like(tk_ref)
        tv_ref[...] = jnp.zeros_like(tv_ref)

    def conv_silu(u_ref, tail_ref, ext_ref, w_ref):
        u = u_ref[0]
        ext_ref[0:8, :] = tail_ref[...]
        ext_ref[8:8 + C, :] = u
        w = w_ref[...]
        y = ext_ref[5:5 + C, :] * w[0:1, :]
        y = y + ext_ref[6:6 + C, :] * w[1:2, :]
        y = y + ext_ref[7:7 + C, :] * w[2:3, :]
        y = y + u * w[3:4, :]
        tail_ref[...] = u[C - 8:C, :]
        return y * _sigmoid(y)

    def l2norm_heads(x, scale):
        parts = []
        for i in range(hg):
            xi = x[:, i * D:(i + 1) * D]
            parts.append(xi * (lax.rsqrt(jnp.sum(xi * xi, axis=-1, keepdims=True) + NORM_EPS) * scale))
        return jnp.concatenate(parts, axis=1)

    qc = l2norm_heads(conv_silu(q_ref, tq_ref, eq_ref, cwq_ref), D ** -0.5)
    kc = l2norm_heads(conv_silu(k_ref, tk_ref, ek_ref, cwk_ref), 1.0)
    vc = conv_silu(v_ref, tv_ref, ev_ref, cwv_ref)

    G = GDN_GROUP
    R = G * C
    shift_c = C.bit_length() - 1

    ba = ba_ref[0]
    beta_full = _sigmoid(ba)
    g_full = -jnp.exp(alog_ref[0]) * _softplus(ba + dtb_ref[0])
    r64 = lax.broadcasted_iota(jnp.int32, (C, C), 0)
    c64 = lax.broadcasted_iota(jnp.int32, (C, C), 1)
    gcum_full = jnp.dot((r64 >= c64).astype(F32), g_full, preferred_element_type=F32,
                        precision=lax.Precision.HIGHEST)
    gcum_t = gcum_full.T
    row = lax.broadcasted_iota(jnp.int32, (R, R), 0)
    col = lax.broadcasted_iota(jnp.int32, (R, R), 1)
    same_head = jnp.right_shift(row, shift_c) == jnp.right_shift(col, shift_c)
    tril = same_head & (row >= col)
    strict = same_head & (row > col)
    lane2 = lax.broadcasted_iota(jnp.int32, (1, 2 * D), 1)
    r2 = lax.broadcasted_iota(jnp.int32, (2 * D, 2 * D), 0)
    c2 = lax.broadcasted_iota(jnp.int32, (2 * D, 2 * D), 1)
    pair_block = (r2 < D) == (c2 < D)

    def lanes_of(x, i0):
        return jnp.concatenate([x[i0 * C:(i0 + 1) * C], x[(i0 + 1) * C:(i0 + 2) * C]], axis=1)

    for h0 in range(0, hg, G):
        heads = range(h0, h0 + G)

        def rows_of(x):
            return jnp.concatenate([x[:, i * D:(i + 1) * D] for i in heads], axis=0)

        beta_r = jnp.concatenate([beta_full[:, i:i + 1] for i in heads], axis=0)
        gc_r = jnp.concatenate([gcum_full[:, hg + i:hg + i + 1] for i in heads], axis=0)
        gc_l = jnp.concatenate([gcum_t[hg + i:hg + i + 1, :] for i in heads], axis=1)
        g_last = [gcum_full[C - 1:C, hg + i:hg + i + 1] for i in heads]
        gl_r = jnp.concatenate([jnp.broadcast_to(g, (C, 1)) for g in g_last], axis=0)

        qn, kn, v_r = rows_of(qc), rows_of(kc), rows_of(vc)
        decay = jnp.where(tril, jnp.exp(jnp.where(tril, gc_r - gc_l, 0.0)), 0.0)
        eg = jnp.exp(gc_r)
        kb = kn * beta_r
        gram = _dot_nt(jnp.concatenate([kb, qn], axis=0), kn)
        nmat = jnp.where(strict, gram[:R] * decay, 0.0)
        qk = gram[R:] * decay
        kbg_r = kb * eg
        qg_r = qn * eg
        kdec_r = kn * jnp.exp(gl_r - gc_r)
        states, ks_rows, o1_rows = [], [], []
        for pr in range(G // 2):
            s = state_ref[h0 // 2 + pr]
            xs = _dot(jnp.concatenate([lanes_of(kbg_r, 2 * pr), lanes_of(qg_r, 2 * pr)], axis=0), s)
            states.append(s)
            ks_rows += [xs[:C, :D], xs[:C, D:]]
            o1_rows += [xs[C:, :D], xs[C:, D:]]
        y = v_r * beta_r - jnp.concatenate(ks_rows, axis=0)
        m = -nmat
        for _ in range(shift_c - 1):
            mb = m.astype(BF16)
            x = jnp.dot(mb, jnp.concatenate([mb, y.astype(BF16)], axis=1), preferred_element_type=F32)
            m = x[:, :R]
            y = y + x[:, R:]
        vnew_r = y + _dot(m, y)
        for pr in range(G // 2):
            upd = _dot_tn(lanes_of(kdec_r, 2 * pr), lanes_of(vnew_r, 2 * pr))
            gl_lanes = jnp.where(lane2 < D, jnp.exp(g_last[2 * pr]), jnp.exp(g_last[2 * pr + 1]))
            state_ref[h0 // 2 + pr] = states[pr] * gl_lanes + jnp.where(pair_block, upd, 0.0)
        o = jnp.concatenate(o1_rows, axis=0) + _dot(qk, vnew_r)
        on = o * lax.rsqrt(jnp.mean(o * o, axis=-1, keepdims=True) + NORM_EPS) * ng_ref[...]
        z_r = rows_of(z_ref[0])
        out = (on * (z_r * _sigmoid(z_r))).astype(o_ref.dtype)
        for j, i in enumerate(heads):
            o_ref[0, :, i * D:(i + 1) * D] = out[j * C:(j + 1) * C]


def _gdn(proj_a, ba, conv_w, alog_p, dtb_p, norm_g, heads, hg):
    b, s, _ = proj_a.shape
    assert hg % 2 == 0 and heads % hg == 0
    ngroups = heads // hg
    w = hg * HEAD_DIM
    nc = s // CHUNK

    def colspec(off):
        return pl.BlockSpec((1, CHUNK, w), lambda bi, gi, ci, off=off: (bi, ci, off * ngroups + gi))

    def cwspec(off):
        return pl.BlockSpec((4, w), lambda bi, gi, ci, off=off: (0, off * ngroups + gi))

    vec = pl.BlockSpec((1, 1, LANES), lambda bi, gi, ci: (gi, 0, 0))
    return pl.pallas_call(
        functools.partial(_gdn_kernel, hg=hg),
        grid=(b, ngroups, nc),
        in_specs=[colspec(0), colspec(1), colspec(2), colspec(3),
                  pl.BlockSpec((1, CHUNK, LANES), lambda bi, gi, ci: (bi, ci, gi)),
                  cwspec(0), cwspec(1), cwspec(2), vec, vec,
                  pl.BlockSpec((1, HEAD_DIM), lambda bi, gi, ci: (0, 0))],
        out_specs=pl.BlockSpec((1, CHUNK, w), lambda bi, gi, ci: (bi, ci, gi)),
        out_shape=jax.ShapeDtypeStruct((b, s, heads * HEAD_DIM), BF16),
        scratch_shapes=[pltpu.VMEM((hg // 2, 2 * HEAD_DIM, 2 * HEAD_DIM), F32)]
        + [pltpu.VMEM((8, w), F32)] * 3 + [pltpu.VMEM((8 + CHUNK, w), F32)] * 3,
        compiler_params=_params("parallel", "parallel", "arbitrary"),
        name="gated_delta_rule",
    )(proj_a, proj_a, proj_a, proj_a, ba, conv_w, conv_w, conv_w, alog_p, dtb_p,
      norm_g.reshape(1, HEAD_DIM))


def _rope_table_kernel(pos_ref, freq_ref, cos_ref, sin_ref):
    ang = pos_ref[...] * freq_ref[...]
    lane = lax.broadcasted_iota(jnp.int32, ang.shape, 1)
    sin = jnp.sin(ang)
    cos_ref[...] = jnp.cos(ang)
    sin_ref[...] = jnp.where(lane < ROT_DIM // 2, -sin, sin)


def _rope_table(pos_f):
    n = pos_f.shape[0]
    ts = _pick_tile(n, 1024, 8)
    half = ROT_DIM // 2
    inv_freq = np.power(np.float32(ROPE_THETA),
                        -np.arange(half, dtype=np.float32) * np.float32(2.0) / np.float32(ROT_DIM))
    freq = np.zeros((1, LANES), np.float32)
    freq[0, :half] = inv_freq
    freq[0, half:ROT_DIM] = inv_freq
    row = pl.BlockSpec((ts, LANES), lambda i: (i, 0))
    return pl.pallas_call(
        _rope_table_kernel,
        grid=(n // ts,),
        in_specs=[pl.BlockSpec((ts, 1), lambda i: (i, 0)), pl.BlockSpec((1, LANES), lambda i: (0, 0))],
        out_specs=[row, row],
        out_shape=[jax.ShapeDtypeStruct((n, LANES), F32)] * 2,
        compiler_params=_params("parallel"),
        name="rope_table",
    )(pos_f, jnp.asarray(freq))


def _proj_t_kernel(*refs, shift, rope, q_tiles, scales):
    refs = list(refs)
    x_ref, w_ref = refs[:2]
    w2_ref = refs[2] if shift else None
    o_ref, wb_ref = refs[-2 - bool(shift)], refs[-1 - bool(shift)]
    tn = wb_ref.shape[1]

    @pl.when(pl.program_id(1) == 0)
    def _():
        if shift:
            ws_ref = refs[-1]
            ws_ref[:tn - shift, :] = w_ref[shift:, :]
            ws_ref[tn - shift:, :] = w2_ref[:shift, :]
            wb_ref[...] = ws_ref[...].T.astype(BF16)
        else:
            wb_ref[...] = w_ref[...].T.astype(BF16)

    acc = jnp.dot(x_ref[...], wb_ref[...], preferred_element_type=F32)
    if not rope:
        o_ref[...] = acc.astype(o_ref.dtype)
        return
    cos_ref, sin_ref, g_ref = refs[2 + bool(shift):5 + bool(shift)]
    is_q = pl.program_id(0) < q_tiles
    g = jnp.where(is_q, g_ref[0:1, :], g_ref[1:2, :])
    scale = jnp.where(is_q, scales[0], scales[1])
    cos = cos_ref[...]
    sin_signed = sin_ref[...]
    lane = lax.broadcasted_iota(jnp.int32, cos.shape, 1)
    first_half = lane < ROT_DIM // 2
    for j in range(tn // HEAD_DIM):
        sl = slice(j * HEAD_DIM, (j + 1) * HEAD_DIM)
        x = acc[:, sl]
        xn = x * (lax.rsqrt(jnp.mean(x * x, axis=-1, keepdims=True) + NORM_EPS) * scale) * g
        partner = jnp.where(first_half, pltpu.roll(xn, HEAD_DIM - ROT_DIM // 2, 1),
                            pltpu.roll(xn, ROT_DIM // 2, 1))
        o_ref[:, sl] = (xn * cos + partner * sin_signed).astype(o_ref.dtype)


def _proj_t(x, wt, row0, nrows, out_dtype, rope=None, name="proj"):
    m, k = x.shape
    shift = row0 % LANES
    tn = _pick_tile(rope[4] if rope else nrows, 1024, LANES)
    tm = _pick_tile(m, MATMUL_ROWS // 2 if rope else MATMUL_ROWS, 8)
    assert (row0 - shift) % tn == 0 and nrows % tn == 0
    cb = (row0 - shift) // tn
    per = tn // LANES
    in_specs = [pl.BlockSpec((tm, k), lambda j, i: (i, 0)),
                pl.BlockSpec((tn, k), lambda j, i: (cb + j, 0))]
    args = [x, wt]
    scratch = [pltpu.VMEM((k, tn), BF16)]
    if shift:
        in_specs.append(pl.BlockSpec((LANES, k), lambda j, i: ((cb + j + 1) * per, 0)))
        args.append(wt)
        scratch.append(pltpu.VMEM((tn, k), F32))
    q_tiles, scales = 0, (1.0, 1.0)
    if rope:
        cos, sin_signed, gains, scales, q_cols = rope
        q_tiles = q_cols // tn
        in_specs += [pl.BlockSpec((tm, LANES), lambda j, i: (i, 0)),
                     pl.BlockSpec((tm, LANES), lambda j, i: (i, 0)),
                     pl.BlockSpec((2, HEAD_DIM), lambda j, i: (0, 0))]
        args += [cos, sin_signed, gains]
    return pl.pallas_call(
        functools.partial(_proj_t_kernel, shift=shift, rope=rope is not None, q_tiles=q_tiles,
                          scales=tuple(scales)),
        grid=(nrows // tn, m // tm),
        in_specs=in_specs,
        out_specs=pl.BlockSpec((tm, tn), lambda j, i: (i, j)),
        out_shape=jax.ShapeDtypeStruct((m, nrows), out_dtype),
        scratch_shapes=scratch,
        compiler_params=_params("parallel", "arbitrary"),
        name=name,
    )(*args)


def _diff_attn_kernel(q_ref, k_ref, v_ref, lq1_ref, lk1_ref, lq2_ref, lk2_ref, sg_ref, o_ref,
                      acc_ref, m_ref, l_ref, *, tq, lam_init):
    D = HEAD_DIM
    qi = pl.program_id(2)
    q1 = q_ref[0, :, :D]
    q2 = q_ref[0, :, D:]
    acc_ref[...] = jnp.zeros_like(acc_ref)
    m_ref[...] = jnp.full_like(m_ref, NEG_INF)
    l_ref[...] = jnp.zeros_like(l_ref)

    def lanes(x, width):
        return jnp.concatenate([x] * (width // LANES), axis=1)

    def step(start, mask):
        kblk = k_ref[0, pl.ds(start, tq), :]
        vblk = v_ref[0, pl.ds(start, tq), :].astype(BF16)
        s = jnp.concatenate([_dot_nt(q1, kblk[:, :D]), _dot_nt(q2, kblk[:, D:])], axis=0)
        if mask is not None:
            s = jnp.where(mask, s, NEG_INF)
        m_prev = m_ref[...]
        m_next = jnp.maximum(m_prev, jnp.max(s, axis=-1, keepdims=True))
        alpha = jnp.exp2(m_prev - m_next)
        p = jnp.exp2(s - lanes(m_next, tq))
        l_ref[...] = alpha * l_ref[...] + jnp.sum(p, axis=-1, keepdims=True)
        acc_ref[...] = lanes(alpha, 2 * D) * acc_ref[...] + jnp.dot(p.astype(BF16), vblk,
                                                                    preferred_element_type=F32)
        m_ref[...] = m_next

    def body(j, carry):
        step(pl.multiple_of(j * tq, tq), None)
        return carry

    lax.fori_loop(0, qi, body, 0)
    shift_c = CHUNK.bit_length() - 1
    row = lax.broadcasted_iota(jnp.int32, (2 * tq, tq), 0)
    row = jnp.right_shift(jnp.where(row >= tq, row - tq, row), shift_c)
    col = jnp.right_shift(lax.broadcasted_iota(jnp.int32, (2 * tq, tq), 1), shift_c)
    step(pl.multiple_of(qi * tq, tq), row >= col)

    lam = (jnp.exp(jnp.sum(lq1_ref[...] * lk1_ref[...], axis=-1, keepdims=True))
           - jnp.exp(jnp.sum(lq2_ref[...] * lk2_ref[...], axis=-1, keepdims=True)) + lam_init)
    a = acc_ref[...] / lanes(l_ref[...], 2 * D)
    o = a[:tq] - lam * a[tq:]
    on = o * lax.rsqrt(jnp.mean(o * o, axis=-1, keepdims=True) + NORM_EPS)
    o_ref[0] = (on * sg_ref[...] * (1.0 - lam_init)).astype(o_ref.dtype)


def _diff_attn(qk, v_src, v_col_block, lq1, lk1, lq2, lk2, subln_g, heads, lam_init):
    b, s, _ = qk.shape
    hw = 2 * HEAD_DIM
    tq = _pick_tile(s, ATTN_BLOCK, CHUNK)
    vec = pl.BlockSpec((1, HEAD_DIM), lambda bi, hi, qi: (0, 0))
    return pl.pallas_call(
        functools.partial(_diff_attn_kernel, tq=tq, lam_init=lam_init),
        grid=(b, heads, s // tq),
        in_specs=[pl.BlockSpec((1, tq, hw), lambda bi, hi, qi: (bi, qi, hi)),
                  pl.BlockSpec((1, s, hw), lambda bi, hi, qi: (bi, 0, heads + hi)),
                  pl.BlockSpec((1, s, hw), lambda bi, hi, qi: (bi, 0, v_col_block + hi)),
                  vec, vec, vec, vec,
                  pl.BlockSpec((1, hw), lambda bi, hi, qi: (0, 0))],
        out_specs=pl.BlockSpec((1, tq, hw), lambda bi, hi, qi: (bi, qi, hi)),
        out_shape=jax.ShapeDtypeStruct((b, s, heads * hw), BF16),
        scratch_shapes=[pltpu.VMEM((2 * tq, hw), F32)] + [pltpu.VMEM((2 * tq, LANES), F32)] * 2,
        compiler_params=_params("parallel", "parallel", "arbitrary"),
        name="diff_attention",
    )(qk, qk, v_src, lq1.reshape(1, -1), lk1.reshape(1, -1), lq2.reshape(1, -1), lk2.reshape(1, -1),
      subln_g.reshape(1, hw))


def _merge_kernel(oa_ref, ob_ref, wa_ref, wb_ref, ga_ref, gb_ref, ba_ref, bb_ref, o_ref, wa_s, wb_s):
    @pl.when(pl.program_id(1) == 0)
    def _():
        wa_s[...] = wa_ref[...].astype(BF16)
        wb_s[...] = wb_ref[...].astype(BF16)

    ya = jnp.dot(oa_ref[...], wa_s[...], preferred_element_type=F32)
    yb = jnp.dot(ob_ref[...], wb_s[...], preferred_element_type=F32)
    merged = _sigmoid(ga_ref[...] + ba_ref[...]) * ya + _sigmoid(gb_ref[...] + bb_ref[...]) * yb
    o_ref[...] = merged.astype(o_ref.dtype)


def _merge(oa, ob, w_a, w_b, proj_b, gate_col0, b_gate, d):
    m, ka = oa.shape
    kb = ob.shape[1]
    tn = _pick_tile(d, 512, LANES)
    tm = _pick_tile(m, MATMUL_ROWS, 8)
    assert gate_col0 % tn == 0
    ga = gate_col0 // tn
    gb = (gate_col0 + d) // tn
    nb = d // tn
    bg = b_gate.reshape(1, 2 * d)
    return pl.pallas_call(
        _merge_kernel,
        grid=(nb, m // tm),
        in_specs=[pl.BlockSpec((tm, ka), lambda j, i: (i, 0)),
                  pl.BlockSpec((tm, kb), lambda j, i: (i, 0)),
                  pl.BlockSpec((ka, tn), lambda j, i: (0, j)),
                  pl.BlockSpec((kb, tn), lambda j, i: (0, j)),
                  pl.BlockSpec((tm, tn), lambda j, i: (i, ga + j)),
                  pl.BlockSpec((tm, tn), lambda j, i: (i, gb + j)),
                  pl.BlockSpec((1, tn), lambda j, i: (0, j)),
                  pl.BlockSpec((1, tn), lambda j, i: (0, nb + j))],
        out_specs=pl.BlockSpec((tm, tn), lambda j, i: (i, j)),
        out_shape=jax.ShapeDtypeStruct((m, d), BF16),
        scratch_shapes=[pltpu.VMEM((ka, tn), BF16), pltpu.VMEM((kb, tn), BF16)],
        compiler_params=_params("parallel", "arbitrary"),
        name="branch_merge",
    )(oa, ob, w_a, w_b, proj_b, proj_b, bg, bg)


def _router_kernel(x_ref, g_ref, wr_ref, br_ref, h_ref, ti_ref, tw_ref, rk_ref, cnt_ref, carry_ref,
                   *, n_experts):
    @pl.when(pl.program_id(0) == 0)
    def _():
        carry_ref[...] = jnp.zeros_like(carry_ref)

    x = x_ref[...]
    h = x * lax.rsqrt(jnp.mean(x * x, axis=-1, keepdims=True) + NORM_EPS) * g_ref[...]
    h_ref[...] = h
    logits = jnp.dot(h, wr_ref[...], preferred_element_type=F32,
                     precision=lax.Precision.HIGHEST) + br_ref[...]
    tm = x.shape[0]
    lane = lax.broadcasted_iota(jnp.int32, (tm, LANES), 1)
    lane_f = lane.astype(F32)
    cur = jnp.where(lane < n_experts, logits, -jnp.inf)
    vals, hots = [], []
    ti = jnp.zeros((tm, LANES), F32)
    for kk in range(TOP_K):
        mx = jnp.max(cur, axis=-1, keepdims=True)
        idx = jnp.min(jnp.where(cur == mx, lane_f, float(LANES)), axis=-1, keepdims=True)
        hot = lane_f == idx
        vals.append(mx)
        hots.append(hot)
        ti = jnp.where(lane == kk, idx, ti)
        cur = jnp.where(hot, -jnp.inf, cur)
    exps = [jnp.exp(v - vals[0]) for v in vals]
    denom = exps[0] + exps[1] + exps[2] + exps[3]
    tw = jnp.zeros((tm, LANES), F32)
    for kk in range(TOP_K):
        tw = jnp.where(lane == kk, exps[kk] / denom, tw)
    multihot = jnp.zeros((tm, LANES), F32)
    for hot in hots:
        multihot = multihot + hot.astype(F32)
    r = lax.broadcasted_iota(jnp.int32, (tm, tm), 0)
    c = lax.broadcasted_iota(jnp.int32, (tm, tm), 1)
    prefix = jnp.dot((r > c).astype(BF16), multihot.astype(BF16), preferred_element_type=F32)
    prefix = prefix + carry_ref[...]
    rk = jnp.zeros((tm, LANES), F32)
    for kk in range(TOP_K):
        rk_k = jnp.sum(jnp.where(hots[kk], prefix, 0.0), axis=-1, keepdims=True)
        rk = jnp.where(lane == kk, rk_k, rk)
    carry = carry_ref[...] + jnp.sum(multihot, axis=0, keepdims=True)
    carry_ref[...] = carry
    cnt_ref[...] = carry.astype(jnp.int32)
    ti_ref[...] = ti.astype(jnp.int32)
    tw_ref[...] = tw
    rk_ref[...] = rk.astype(jnp.int32)


def _router(x1, g, w_router, b_router):
    n, d = x1.shape
    e = w_router.shape[1]
    tm = _pick_tile(n, 512, 8)
    wr = jnp.zeros((d, LANES), F32).at[:, :e].set(w_router)
    br = jnp.zeros((1, LANES), F32).at[0, :e].set(b_router)
    row = pl.BlockSpec((tm, LANES), lambda i: (i, 0))
    return pl.pallas_call(
        functools.partial(_router_kernel, n_experts=e),
        grid=(n // tm,),
        in_specs=[pl.BlockSpec((tm, d), lambda i: (i, 0)),
                  pl.BlockSpec((1, d), lambda i: (0, 0)),
                  pl.BlockSpec((d, LANES), lambda i: (0, 0)),
                  pl.BlockSpec((1, LANES), lambda i: (0, 0))],
        out_specs=[pl.BlockSpec((tm, d), lambda i: (i, 0)), row, row, row,
                   pl.BlockSpec((1, LANES), lambda i: (0, 0))],
        out_shape=[jax.ShapeDtypeStruct((n, d), F32),
                   jax.ShapeDtypeStruct((n, LANES), jnp.int32),
                   jax.ShapeDtypeStruct((n, LANES), F32),
                   jax.ShapeDtypeStruct((n, LANES), jnp.int32),
                   jax.ShapeDtypeStruct((1, LANES), jnp.int32)],
        scratch_shapes=[pltpu.VMEM((1, LANES), F32)],
        compiler_params=_params("arbitrary"),
        name="router_topk",
    )(x1, g.reshape(1, d), wr, br)


def _row_copy(src_ref, src_row, dst_ref, dst_row, sem):
    return pltpu.make_async_copy(src_ref.at[pl.ds(src_row, 1)], dst_ref.at[pl.ds(dst_row, 1)], sem)


def _expert_kernel(se_ref, sb0_ref, sn_ref, na_ref, *refs, rows, sbk, nf, nsuper, nblk):
    tok_refs = refs[:sbk]
    tokn_refs = refs[sbk:2 * sbk]
    (h_ref, wg_ref, wu_ref, wd_ref, bg_ref, bu_ref, bd_ref, ys_ref,
     xg_ref, xb_ref, acc_ref, wgb_ref, wub_ref, wdb_ref, gsem, osem) = refs[2 * sbk:]
    s = pl.program_id(0)
    f = pl.program_id(1)
    nsub = sn_ref[s]

    def gather(t_refs, count):
        for j in range(sbk):
            @pl.when(j < count)
            def _(j=j):
                def body(r, carry):
                    _row_copy(h_ref, t_refs[j][0, 0, r], xg_ref, j * rows + r, gsem).start()
                    return carry
                lax.fori_loop(0, rows, body, 0, unroll=8)

    def out_copy(j, blk0):
        return pltpu.make_async_copy(acc_ref.at[pl.ds(j * rows, rows)],
                                     ys_ref.at[pl.ds((blk0 + j) * rows, rows)], osem)

    n_act = na_ref[0]
    blocks_used = na_ref[1]

    def tail_block(j):
        return blocks_used + (s - n_act) * sbk + j

    def zero_copy(j):
        return pltpu.make_async_copy(acc_ref.at[pl.ds(0, rows)],
                                     ys_ref.at[pl.ds(tail_block(j) * rows, rows)], osem)

    @pl.when((f == 0) & (s == 0))
    def _():
        gather(tok_refs, nsub)

    @pl.when(f == 0)
    def _():
        for j in range(sbk):
            @pl.when(j < nsub)
            def _(j=j):
                pltpu.make_async_copy(h_ref.at[pl.ds(0, rows)], xg_ref.at[pl.ds(j * rows, rows)], gsem).wait()
        for j in range(sbk):
            @pl.when(j < nsub)
            def _(j=j):
                xb_ref[pl.ds(j * rows, rows), :] = xg_ref[pl.ds(j * rows, rows), :].astype(BF16)

        @pl.when(s + 1 < nsuper)
        def _():
            gather(tokn_refs, sn_ref[s + 1])

        @pl.when(s > 0)
        def _():
            for j in range(sbk):
                @pl.when(j < sn_ref[s - 1])
                def _(j=j):
                    out_copy(j, sb0_ref[s - 1]).wait()

        for j in range(sbk):
            @pl.when(j < nsub)
            def _(j=j):
                acc_ref[pl.ds(j * rows, rows), :] = jnp.broadcast_to(bd_ref[0], (rows, acc_ref.shape[1]))

        @pl.when(s == n_act)
        def _():
            acc_ref[pl.ds(0, rows), :] = jnp.zeros((rows, acc_ref.shape[1]), F32)

        for j in range(sbk):
            @pl.when((s >= n_act) & (tail_block(j) < nblk))
            def _(j=j):
                zero_copy(j).start()

    @pl.when(nsub > 0)
    def _():
        wgb_ref[...] = wg_ref[0].astype(BF16)
        wub_ref[...] = wu_ref[0].astype(BF16)
        wdb_ref[...] = wd_ref[0].astype(BF16)

    for j in range(sbk):
        @pl.when(j < nsub)
        def _(j=j):
            xb = xb_ref[pl.ds(j * rows, rows), :]
            g = jnp.dot(xb, wgb_ref[...], preferred_element_type=F32) + bg_ref[0]
            u = jnp.dot(xb, wub_ref[...], preferred_element_type=F32) + bu_ref[0]
            gate = jnp.minimum(g, SWIGLU_LIMIT)
            up = jnp.clip(u, -SWIGLU_LIMIT, SWIGLU_LIMIT)
            act = (up + 1.0) * gate * _sigmoid(SWIGLU_ALPHA * gate)
            acc_ref[pl.ds(j * rows, rows), :] += jnp.dot(act.astype(BF16), wdb_ref[...],
                                                         preferred_element_type=F32)

    @pl.when(f == nf - 1)
    def _():
        for j in range(sbk):
            @pl.when((s >= n_act) & (tail_block(j) < nblk))
            def _(j=j):
                zero_copy(j).wait()

        for j in range(sbk):
            @pl.when(j < nsub)
            def _(j=j):
                out_copy(j, sb0_ref[s]).start()

        @pl.when(s == nsuper - 1)
        def _():
            for j in range(sbk):
                @pl.when(j < nsub)
                def _(j=j):
                    out_copy(j, sb0_ref[s]).wait()


def _experts(h, slot_token, sb_expert, sb_blk0, sb_nsub, n_active, w_gate_up, b_gate_up, w_down, b_down,
             rows, sbk):
    n, d = h.shape
    p = slot_token.shape[0]
    e, _, dff2 = w_gate_up.shape
    dff = dff2 // 2
    tf = _pick_tile(dff, EXPERT_FF_TILE, LANES)
    nf = dff // tf
    nblk = p // rows
    nsuper = sb_expert.shape[0]
    tok3 = slot_token.reshape(nblk, 1, rows)

    def frozen_f(s, f, na):
        return jnp.where(s < na[0], f, nf - 1)

    def tok_spec(j, ahead):
        def index(s, f, se, sb0, sn, na):
            si = jnp.minimum(s + ahead, nsuper - 1)
            return (jnp.minimum(sb0[si] + j, nblk - 1), 0, 0)
        return pl.BlockSpec((1, 1, rows), index, memory_space=pltpu.SMEM)

    grid_spec = pltpu.PrefetchScalarGridSpec(
        num_scalar_prefetch=4,
        grid=(nsuper, nf),
        in_specs=[tok_spec(j, 0) for j in range(sbk)] + [tok_spec(j, 1) for j in range(sbk)] + [
            pl.BlockSpec(memory_space=pl.ANY),
            pl.BlockSpec((1, d, tf), lambda s, f, se, sb0, sn, na: (se[s], 0, frozen_f(s, f, na))),
            pl.BlockSpec((1, d, tf), lambda s, f, se, sb0, sn, na: (se[s], 0, nf + frozen_f(s, f, na))),
            pl.BlockSpec((1, tf, d), lambda s, f, se, sb0, sn, na: (se[s], frozen_f(s, f, na), 0)),
            pl.BlockSpec((1, 1, tf), lambda s, f, se, sb0, sn, na: (se[s], 0, frozen_f(s, f, na))),
            pl.BlockSpec((1, 1, tf), lambda s, f, se, sb0, sn, na: (se[s], 0, nf + frozen_f(s, f, na))),
            pl.BlockSpec((1, 1, d), lambda s, f, se, sb0, sn, na: (se[s], 0, 0))],
        out_specs=pl.BlockSpec(memory_space=pl.ANY),
        scratch_shapes=[pltpu.VMEM((sbk * rows, d), F32), pltpu.VMEM((sbk * rows, d), BF16),
                        pltpu.VMEM((sbk * rows, d), F32),
                        pltpu.VMEM((d, tf), BF16), pltpu.VMEM((d, tf), BF16), pltpu.VMEM((tf, d), BF16),
                        pltpu.SemaphoreType.DMA(()), pltpu.SemaphoreType.DMA(())],
    )
    return pl.pallas_call(
        functools.partial(_expert_kernel, rows=rows, sbk=sbk, nf=nf, nsuper=nsuper, nblk=nblk),
        grid_spec=grid_spec,
        out_shape=jax.ShapeDtypeStruct((p, d), F32),
        compiler_params=_params("arbitrary", "arbitrary"),
        name="expert_mlp",
    )(sb_expert, sb_blk0, sb_nsub, n_active, *([tok3] * (2 * sbk)), h, w_gate_up, w_gate_up, w_down,
      b_gate_up.reshape(e, 1, dff2), b_gate_up.reshape(e, 1, dff2), b_down.reshape(e, 1, d))


def _combine_kernel(dest_ref, destn_ref, x_ref, tw_ref, ys_ref, o_ref, buf_ref, sem, *, tm, nt):
    i = pl.program_id(0)
    slot = i % 2

    def gather(d_ref, s):
        def body(t, carry):
            for kk in range(TOP_K):
                _row_copy(ys_ref, d_ref[0, 0, t * TOP_K + kk], buf_ref.at[s, kk], t, sem.at[s]).start()
            return carry
        lax.fori_loop(0, tm, body, 0, unroll=2)

    @pl.when(i == 0)
    def _():
        gather(dest_ref, 0)

    @pl.when(i + 1 < nt)
    def _():
        gather(destn_ref, 1 - slot)

    for kk in range(TOP_K):
        pltpu.make_async_copy(ys_ref.at[pl.ds(0, tm)], buf_ref.at[slot, kk], sem.at[slot]).wait()
    tw = tw_ref[...]
    y = buf_ref[slot, 0] * tw[:, 0:1]
    for kk in range(1, TOP_K):
        y = y + buf_ref[slot, kk] * tw[:, kk:kk + 1]
    o_ref[...] = x_ref[...] + y


def _combine(x1, topw, dest, ys):
    n, d = x1.shape
    tm = _pick_tile(n, 128, 8)
    nt = n // tm
    dest3 = dest.reshape(nt, 1, tm * TOP_K)
    return pl.pallas_call(
        functools.partial(_combine_kernel, tm=tm, nt=nt),
        grid=(nt,),
        in_specs=[pl.BlockSpec((1, 1, tm * TOP_K), lambda i: (i, 0, 0), memory_space=pltpu.SMEM),
                  pl.BlockSpec((1, 1, tm * TOP_K), lambda i: (jnp.minimum(i + 1, nt - 1), 0, 0),
                               memory_space=pltpu.SMEM),
                  pl.BlockSpec((tm, d), lambda i: (i, 0)),
                  pl.BlockSpec((tm, LANES), lambda i: (i, 0)),
                  pl.BlockSpec(memory_space=pl.ANY)],
        out_specs=pl.BlockSpec((tm, d), lambda i: (i, 0)),
        out_shape=jax.ShapeDtypeStruct((n, d), F32),
        scratch_shapes=[pltpu.VMEM((2, TOP_K, tm, d), F32), pltpu.SemaphoreType.DMA((2,))],
        compiler_params=_params("arbitrary"),
        name="moe_combine",
    )(dest3, dest3, x1, topw, ys)


MATMUL_ROWS = 1024
ATTN_BLOCK = 1024
GDN_HEADS_PER_STEP = 16
GDN_GROUP = 4
EXPERT_ROWS = 512
EXPERT_SUBBLOCKS = 2
EXPERT_FF_TILE = 512


def _layer(x, positions, layer_idx, norm1_g, w_in, b_gate, conv_w, a_log, dt_bias, gdn_norm_g,
           q_norm_g, k_norm_g, lambda_q1, lambda_k1, lambda_q2, lambda_k2, subln_g,
           w_branch_a, w_branch_b, w_out, norm2_g, w_router, b_router,
           w_gate_up, b_gate_up, w_down, b_down):
    b, s, d = x.shape
    n = b * s
    gdn_heads = a_log.shape[0]
    gdn_dim = gdn_heads * HEAD_DIM
    diff_v = w_branch_b.shape[0]
    diff_heads = diff_v // (2 * HEAD_DIM)
    diff_qk = 2 * diff_heads * HEAD_DIM
    off_beta = 4 * gdn_dim
    off_alpha = off_beta + gdn_heads
    off_qb = off_alpha + gdn_heads
    off_gate = off_qb + 2 * diff_qk + diff_v
    assert w_in.shape[1] == off_gate + 2 * d
    n_experts = w_router.shape[1]

    x2 = x.reshape(n, d)
    h = _rmsnorm(x2, norm1_g, BF16)

    wt = jnp.swapaxes(w_in, 0, 1)
    proj_a = _proj_t(h, wt, 0, off_beta, F32, name="proj_gdn")
    hg = min(GDN_HEADS_PER_STEP, gdn_heads)
    ngroups = gdn_heads // hg
    row_pad = jnp.zeros((LANES - 2 * hg, d), F32)
    w_ba = jnp.concatenate(
        [part for gi in range(ngroups) for part in (
            wt[off_beta + gi * hg:off_beta + (gi + 1) * hg],
            wt[off_alpha + gi * hg:off_alpha + (gi + 1) * hg], row_pad)], axis=0)
    ba = _proj_t(h, w_ba, 0, ngroups * LANES, F32, name="proj_beta_alpha")
    cos_t, sin_t = _rope_table(positions.astype(F32).reshape(n, 1))
    qk = _proj_t(h, wt, off_qb, 2 * diff_qk, BF16, name="proj_qk_rope",
                 rope=(cos_t, sin_t, jnp.stack([q_norm_g, k_norm_g]),
                       (HEAD_DIM ** -0.5 * math.log2(math.e), 1.0), diff_qk))
    vg = _proj_t(h, wt, off_qb + 2 * diff_qk, diff_v + 2 * d, F32, name="proj_v_gate")

    alog_p = jnp.zeros((ngroups, 1, LANES), F32).at[:, 0, hg:2 * hg].set(a_log.reshape(ngroups, hg))
    dtb_p = jnp.zeros((ngroups, 1, LANES), F32).at[:, 0, hg:2 * hg].set(dt_bias.reshape(ngroups, hg))
    oa = _gdn(proj_a.reshape(b, s, off_beta), ba.reshape(b, s, ngroups * LANES), conv_w,
              alog_p, dtb_p, gdn_norm_g, gdn_heads, hg)

    lam_init = 0.8 - 0.6 * math.exp(-0.3 * layer_idx)
    ob = _diff_attn(qk.reshape(b, s, 2 * diff_qk), vg.reshape(b, s, diff_v + 2 * d), 0,
                    lambda_q1, lambda_k1, lambda_q2, lambda_k2, subln_g, diff_heads, lam_init)

    merged = _merge(oa.reshape(n, gdn_dim), ob.reshape(n, diff_v), w_branch_a, w_branch_b,
                    vg, diff_v, b_gate, d)
    x1 = _matmul(merged, w_out, 0, d, F32, res=x2, name="out_proj")

    h2, topi, topw, rank, counts = _router(x1, norm2_g, w_router, b_router)
    rows, sbk = EXPERT_ROWS, EXPERT_SUBBLOCKS
    top_e = topi[:, :TOP_K]
    cnt = counts[0, :n_experts]
    nb = (cnt + rows - 1) // rows
    blk_end = jnp.cumsum(nb)
    blk_start = blk_end - nb
    dest = (blk_start[top_e] * rows + rank[:, :TOP_K]).reshape(n * TOP_K)
    p_rows = n * TOP_K + n_experts * rows
    nblk = p_rows // rows
    slot_token = jnp.zeros((p_rows,), jnp.int32).at[dest].set(
        jnp.repeat(jnp.arange(n, dtype=jnp.int32), TOP_K), unique_indices=True, mode="promise_in_bounds")
    ns = (nb + sbk - 1) // sbk
    ns_end = jnp.cumsum(ns)
    ns_start = ns_end - ns
    n_active = ns_end[-1]
    nsuper = nblk // sbk + n_experts
    s_idx = jnp.arange(nsuper, dtype=jnp.int32)
    active = s_idx < n_active
    s_src = jnp.where(active, s_idx, jnp.maximum(n_active - 1, 0))
    sb_expert = jnp.minimum(jnp.sum(s_src[:, None] >= ns_end[None, :], axis=1), n_experts - 1)
    sb_local = s_src - ns_start[sb_expert]
    sb_blk0 = blk_start[sb_expert] + sb_local * sbk
    sb_nsub = jnp.where(active, jnp.clip(nb[sb_expert] - sb_local * sbk, 0, sbk), 0)
    ys = _experts(h2, slot_token, sb_expert.astype(jnp.int32), sb_blk0.astype(jnp.int32),
                  sb_nsub.astype(jnp.int32), jnp.stack([n_active, blk_end[-1]]).astype(jnp.int32),
                  w_gate_up, b_gate_up, w_down, b_down, rows, sbk)
    out = _combine(x1, topw, dest, ys)
    return out.reshape(b, s, d)


def kernel(x, positions, norm1_g, w_in, b_gate, conv_w, a_log, dt_bias, gdn_norm_g, q_norm_g, k_norm_g, lambda_q1, lambda_k1, lambda_q2, lambda_k2, subln_g, w_branch_a, w_branch_b, w_out, norm2_g, w_router, b_router, w_gate_up, b_gate_up, w_down, b_down):
    depth = norm1_g.shape[0]
    for l in range(depth):
        x = _layer(x, positions, l, norm1_g[l], w_in[l], b_gate[l], conv_w[l], a_log[l], dt_bias[l],
                   gdn_norm_g[l], q_norm_g[l], k_norm_g[l], lambda_q1[l], lambda_k1[l], lambda_q2[l],
                   lambda_k2[l], subln_g[l], w_branch_a[l], w_branch_b[l], w_out[l], norm2_g[l],
                   w_router[l], b_router[l], w_gate_up[l], b_gate_up[l], w_down[l], b_down[l])
    return x
```

```python
import functools
import math

import numpy as np
import jax
import jax.numpy as jnp
from jax import lax
from jax.experimental import pallas as pl
from jax.experimental.pallas import tpu as pltpu

F32 = jnp.float32
BF16 = jnp.bfloat16

NORM_EPS = 1e-6
NEG_INF = -1e30
CHUNK = 64
HEAD_DIM = 128
ROT_DIM = HEAD_DIM // 4
ROPE_THETA = 500000.0
TOP_K = 4
SWIGLU_ALPHA = 1.702
SWIGLU_LIMIT = 7.0
LANES = 128
VMEM_LIMIT_BYTES = 60 * 1024 * 1024


def _pick_tile(n, target, quantum):
    if n <= target:
        return n
    t = (target // quantum) * quantum
    while t > quantum and n % t:
        t -= quantum
    assert n % t == 0, (n, target, quantum)
    return t


def _params(*sem):
    return pltpu.CompilerParams(dimension_semantics=sem, vmem_limit_bytes=VMEM_LIMIT_BYTES)


def _sigmoid(x):
    return 1.0 / (1.0 + jnp.exp(-x))


def _softplus(x):
    return jnp.maximum(x, 0.0) + jnp.log(1.0 + jnp.exp(-jnp.abs(x)))


def _dot(a, b):
    return jnp.dot(a.astype(BF16), b.astype(BF16), preferred_element_type=F32)


def _dot_nt(a, b):
    return lax.dot_general(a.astype(BF16), b.astype(BF16), (((1,), (1,)), ((), ())),
                           preferred_element_type=F32)


def _dot_tn(a, b):
    return lax.dot_general(a.astype(BF16), b.astype(BF16), (((0,), (0,)), ((), ())),
                           preferred_element_type=F32)


def _rmsnorm_kernel(x_ref, g_ref, o_ref):
    x = x_ref[...]
    y = x * lax.rsqrt(jnp.mean(x * x, axis=-1, keepdims=True) + NORM_EPS)
    o_ref[...] = (y * g_ref[...]).astype(o_ref.dtype)


def _rmsnorm(x2d, g, out_dtype):
    n, d = x2d.shape
    tm = _pick_tile(n, 512, 8)
    return pl.pallas_call(
        _rmsnorm_kernel,
        grid=(n // tm,),
        in_specs=[pl.BlockSpec((tm, d), lambda i: (i, 0)), pl.BlockSpec((1, d), lambda i: (0, 0))],
        out_specs=pl.BlockSpec((tm, d), lambda i: (i, 0)),
        out_shape=jax.ShapeDtypeStruct((n, d), out_dtype),
        compiler_params=_params("parallel"),
        name="rmsnorm",
    )(x2d, g.reshape(1, d))


def _matmul_kernel(*refs, has_res):
    if has_res:
        x_ref, w_ref, r_ref, o_ref, wb_ref = refs
    else:
        x_ref, w_ref, o_ref, wb_ref = refs

    @pl.when(pl.program_id(1) == 0)
    def _():
        wb_ref[...] = w_ref[...].astype(BF16)

    acc = jnp.dot(x_ref[...], wb_ref[...], preferred_element_type=F32)
    if has_res:
        acc = acc + r_ref[...]
    o_ref[...] = acc.astype(o_ref.dtype)


def _matmul(x, w, col0, ncols, out_dtype, res=None, name="matmul"):
    m, k = x.shape
    tn = _pick_tile(ncols, 1024, LANES)
    tm = _pick_tile(m, MATMUL_ROWS, 8)
    assert col0 % tn == 0
    cb = col0 // tn
    in_specs = [pl.BlockSpec((tm, k), lambda j, i: (i, 0)),
                pl.BlockSpec((k, tn), lambda j, i: (0, cb + j))]
    args = [x, w]
    if res is not None:
        in_specs.append(pl.BlockSpec((tm, tn), lambda j, i: (i, j)))
        args.append(res)
    return pl.pallas_call(
        functools.partial(_matmul_kernel, has_res=res is not None),
        grid=(ncols // tn, m // tm),
        in_specs=in_specs,
        out_specs=pl.BlockSpec((tm, tn), lambda j, i: (i, j)),
        out_shape=jax.ShapeDtypeStruct((m, ncols), out_dtype),
        scratch_shapes=[pltpu.VMEM((k, tn), BF16)],
        compiler_params=_params("parallel", "arbitrary"),
        name=name,
    )(*args)


def _gdn_kernel(q_ref, k_ref, v_ref, z_ref, ba_ref, cwq_ref, cwk_ref, cwv_ref, alog_ref, dtb_ref,
                ng_ref, o_ref, state_ref, tq_ref, tk_ref, tv_ref, eq_ref, ek_ref, ev_ref, *, hg):
    C = CHUNK
    D = HEAD_DIM

    @pl.when(pl.program_id(2) == 0)
    def _():
        state_ref[...] = jnp.zeros_like(state_ref)
        tq_ref[...] = jnp.zeros_like(tq_ref)
        tk_ref[...] = jnp.zeros_like(tk_ref)
        tv_ref[...] = jnp.zeros_like(tv_ref)

    def conv_silu(u_ref, tail_ref, ext_ref, w_ref):
        u = u_ref[0]
        ext_ref[0:8, :] = tail_ref[...]
        ext_ref[8:8 + C, :] = u
        w = w_ref[...]
        y = ext_ref[5:5 + C, :] * w[0:1, :]
        y = y + ext_ref[6:6 + C, :] * w[1:2, :]
        y = y + ext_ref[7:7 + C, :] * w[2:3, :]
        y = y + u * w[3:4, :]
        tail_ref[...] = u[C - 8:C, :]
        return y * _sigmoid(y)

    def l2norm_heads(x, scale):
        parts = []
        for i in range(hg):
            xi = x[:, i * D:(i + 1) * D]
            parts.append(xi * (lax.rsqrt(jnp.sum(xi * xi, axis=-1, keepdims=True) + NORM_EPS) * scale))
        return jnp.concatenate(parts, axis=1)

    qc = l2norm_heads(conv_silu(q_ref, tq_ref, eq_ref, cwq_ref), D ** -0.5)
    kc = l2norm_heads(conv_silu(k_ref, tk_ref, ek_ref, cwk_ref), 1.0)
    vc = conv_silu(v_ref, tv_ref, ev_ref, cwv_ref)

    G = GDN_GROUP
    R = G * C
    shift_c = C.bit_length() - 1

    ba = ba_ref[0]
    beta_full = _sigmoid(ba)
    g_full = -jnp.exp(alog_ref[0]) * _softplus(ba + dtb_ref[0])
    r64 = lax.broadcasted_iota(jnp.int32, (C, C), 0)
    c64 = lax.broadcasted_iota(jnp.int32, (C, C), 1)
    gcum_full = jnp.dot((r64 >= c64).astype(F32), g_full, preferred_element_type=F32,
                        precision=lax.Precision.HIGHEST)
    gcum_t = gcum_full.T
    row = lax.broadcasted_iota(jnp.int32, (R, R), 0)
    col = lax.broadcasted_iota(jnp.int32, (R, R), 1)
    same_head = jnp.right_shift(row, shift_c) == jnp.right_shift(col, shift_c)
    tril = same_head & (row >= col)
    strict = same_head & (row > col)
    lane2 = lax.broadcasted_iota(jnp.int32, (1, 2 * D), 1)
    r2 = lax.broadcasted_iota(jnp.int32, (2 * D, 2 * D), 0)
    c2 = lax.broadcasted_iota(jnp.int32, (2 * D, 2 * D), 1)
    pair_block = (r2 < D) == (c2 < D)

    def lanes_of(x, i0):
        return jnp.concatenate([x[i0 * C:(i0 + 1) * C], x[(i0 + 1) * C:(i0 + 2) * C]], axis=1)

    for h0 in range(0, hg, G):
        heads = range(h0, h0 + G)

        def rows_of(x):
            return jnp.concatenate([x[:, i * D:(i + 1) * D] for i in heads], axis=0)

        beta_r = jnp.concatenate([beta_full[:, i:i + 1] for i in heads], axis=0)
        gc_r = jnp.concatenate([gcum_full[:, hg + i:hg + i + 1] for i in heads], axis=0)
        gc_l = jnp.concatenate([gcum_t[hg + i:hg + i + 1, :] for i in heads], axis=1)
        g_last = [gcum_full[C - 1:C, hg + i:hg + i + 1] for i in heads]
        gl_r = jnp.concatenate([jnp.broadcast_to(g, (C, 1)) for g in g_last], axis=0)

        qn, kn, v_r = rows_of(qc), rows_of(kc), rows_of(vc)
        decay = jnp.where(tril, jnp.exp(jnp.where(tril, gc_r - gc_l, 0.0)), 0.0)
        eg = jnp.exp(gc_r)
        kb = kn * beta_r
        gram = _dot_nt(jnp.concatenate([kb, qn], axis=0), kn)
        nmat = jnp.where(strict, gram[:R] * decay, 0.0)
        qk = gram[R:] * decay
        kbg_r = kb * eg
        qg_r = qn * eg
        kdec_r = kn * jnp.exp(gl_r - gc_r)
        states, ks_rows, o1_rows = [], [], []
        for pr in range(G // 2):
            s = state_ref[h0 // 2 + pr]
            xs = _dot(jnp.concatenate([lanes_of(kbg_r, 2 * pr), lanes_of(qg_r, 2 * pr)], axis=0), s)
            states.append(s)
            ks_rows += [xs[:C, :D], xs[:C, D:]]
            o1_rows += [xs[C:, :D], xs[C:, D:]]
        y = v_r * beta_r - jnp.concatenate(ks_rows, axis=0)
        m = -nmat
        for _ in range(shift_c - 1):
            mb = m.astype(BF16)
            x = jnp.dot(mb, jnp.concatenate([mb, y.astype(BF16)], axis=1), preferred_element_type=F32)
            m = x[:, :R]
            y = y + x[:, R:]
        vnew_r = y + _dot(m, y)
        for pr in range(G // 2):
            upd = _dot_tn(lanes_of(kdec_r, 2 * pr), lanes_of(vnew_r, 2 * pr))
            gl_lanes = jnp.where(lane2 < D, jnp.exp(g_last[2 * pr]), jnp.exp(g_last[2 * pr + 1]))
            state_ref[h0 // 2 + pr] = states[pr] * gl_lanes + jnp.where(pair_block, upd, 0.0)
        o = jnp.concatenate(o1_rows, axis=0) + _dot(qk, vnew_r)
        on = o * lax.rsqrt(jnp.mean(o * o, axis=-1, keepdims=True) + NORM_EPS) * ng_ref[...]
        z_r = rows_of(z_ref[0])
        out = (on * (z_r * _sigmoid(z_r))).astype(o_ref.dtype)
        for j, i in enumerate(heads):
            o_ref[0, :, i * D:(i + 1) * D] = out[j * C:(j + 1) * C]


def _gdn(proj_a, ba, conv_w, alog_p, dtb_p, norm_g, heads, hg):
    b, s, _ = proj_a.shape
    assert hg % 2 == 0 and heads % hg == 0
    ngroups = heads // hg
    w = hg * HEAD_DIM
    nc = s // CHUNK

    def colspec(off):
        return pl.BlockSpec((1, CHUNK, w), lambda bi, gi, ci, off=off: (bi, ci, off * ngroups + gi))

    def cwspec(off):
        return pl.BlockSpec((4, w), lambda bi, gi, ci, off=off: (0, off * ngroups + gi))

    vec = pl.BlockSpec((1, 1, LANES), lambda bi, gi, ci: (gi, 0, 0))
    return pl.pallas_call(
        functools.partial(_gdn_kernel, hg=hg),
        grid=(b, ngroups, nc),
        in_specs=[colspec(0), colspec(1), colspec(2), colspec(3),
                  pl.BlockSpec((1, CHUNK, LANES), lambda bi, gi, ci: (bi, ci, gi)),
                  cwspec(0), cwspec(1), cwspec(2), vec, vec,
                  pl.BlockSpec((1, HEAD_DIM), lambda bi, gi, ci: (0, 0))],
        out_specs=pl.BlockSpec((1, CHUNK, w), lambda bi, gi, ci: (bi, ci, gi)),
        out_shape=jax.ShapeDtypeStruct((b, s, heads * HEAD_DIM), BF16),
        scratch_shapes=[pltpu.VMEM((hg // 2, 2 * HEAD_DIM, 2 * HEAD_DIM), F32)]
        + [pltpu.VMEM((8, w), F32)] * 3 + [pltpu.VMEM((8 + CHUNK, w), F32)] * 3,
        compiler_params=_params("parallel", "parallel", "arbitrary"),
        name="gated_delta_rule",
    )(proj_a, proj_a, proj_a, proj_a, ba, conv_w, conv_w, conv_w, alog_p, dtb_p,
      norm_g.reshape(1, HEAD_DIM))


def _rope_table_kernel(pos_ref, freq_ref, cos_ref, sin_ref):
    ang = pos_ref[...] * freq_ref[...]
    lane = lax.broadcasted_iota(jnp.int32, ang.shape, 1)
    sin = jnp.sin(ang)
    cos_ref[...] = jnp.cos(ang)
    sin_ref[...] = jnp.where(lane < ROT_DIM // 2, -sin, sin)


def _rope_table(pos_f):
    n = pos_f.shape[0]
    ts = _pick_tile(n, 1024, 8)
    half = ROT_DIM // 2
    inv_freq = np.power(np.float32(ROPE_THETA),
                        -np.arange(half, dtype=np.float32) * np.float32(2.0) / np.float32(ROT_DIM))
    freq = np.zeros((1, LANES), np.float32)
    freq[0, :half] = inv_freq
    freq[0, half:ROT_DIM] = inv_freq
    row = pl.BlockSpec((ts, LANES), lambda i: (i, 0))
    return pl.pallas_call(
        _rope_table_kernel,
        grid=(n // ts,),
        in_specs=[pl.BlockSpec((ts, 1), lambda i: (i, 0)), pl.BlockSpec((1, LANES), lambda i: (0, 0))],
        out_specs=[row, row],
        out_shape=[jax.ShapeDtypeStruct((n, LANES), F32)] * 2,
        compiler_params=_params("parallel"),
        name="rope_table",
    )(pos_f, jnp.asarray(freq))


def _proj_t_kernel(*refs, shift, rope, q_tiles, scales):
    refs = list(refs)
    x_ref, w_ref = refs[:2]
    w2_ref = refs[2] if shift else None
    o_ref, wb_ref = refs[-2 - bool(shift)], refs[-1 - bool(shift)]
    tn = wb_ref.shape[1]

    @pl.when(pl.program_id(1) == 0)
    def _():
        if shift:
            ws_ref = refs[-1]
            ws_ref[:tn - shift, :] = w_ref[shift:, :]
            ws_ref[tn - shift:, :] = w2_ref[:shift, :]
            wb_ref[...] = ws_ref[...].T.astype(BF16)
        else:
            wb_ref[...] = w_ref[...].T.astype(BF16)

    acc = jnp.dot(x_ref[...], wb_ref[...], preferred_element_type=F32)
    if not rope:
        o_ref[...] = acc.astype(o_ref.dtype)
        return
    cos_ref, sin_ref, g_ref = refs[2 + bool(shift):5 + bool(shift)]
    is_q = pl.program_id(0) < q_tiles
    g = jnp.where(is_q, g_ref[0:1, :], g_ref[1:2, :])
    scale = jnp.where(is_q, scales[0], scales[1])
    cos = cos_ref[...]
    sin_signed = sin_ref[...]
    lane = lax.broadcasted_iota(jnp.int32, cos.shape, 1)
    first_half = lane < ROT_DIM // 2
    for j in range(tn // HEAD_DIM):
        sl = slice(j * HEAD_DIM, (j + 1) * HEAD_DIM)
        x = acc[:, sl]
        xn = x * (lax.rsqrt(jnp.mean(x * x, axis=-1, keepdims=True) + NORM_EPS) * scale) * g
        partner = jnp.where(first_half, pltpu.roll(xn, HEAD_DIM - ROT_DIM // 2, 1),
                            pltpu.roll(xn, ROT_DIM // 2, 1))
        o_ref[:, sl] = (xn * cos + partner * sin_signed).astype(o_ref.dtype)


def _proj_t(x, wt, row0, nrows, out_dtype, rope=None, name="proj"):
    m, k = x.shape
    shift = row0 % LANES
    tn = _pick_tile(rope[4] if rope else nrows, 1024, LANES)
    tm = _pick_tile(m, MATMUL_ROWS // 2 if rope else MATMUL_ROWS, 8)
    assert (row0 - shift) % tn == 0 and nrows % tn == 0
    cb = (row0 - shift) // tn
    per = tn // LANES
    in_specs = [pl.BlockSpec((tm, k), lambda j, i: (i, 0)),
                pl.BlockSpec((tn, k), lambda j, i: (cb + j, 0))]
    args = [x, wt]
    scratch = [pltpu.VMEM((k, tn), BF16)]
    if shift:
        in_specs.append(pl.BlockSpec((LANES, k), lambda j, i: ((cb + j + 1) * per, 0)))
        args.append(wt)
        scratch.append(pltpu.VMEM((tn, k), F32))
    q_tiles, scales = 0, (1.0, 1.0)
    if rope:
        cos, sin_signed, gains, scales, q_cols = rope
        q_tiles = q_cols // tn
        in_specs += [pl.BlockSpec((tm, LANES), lambda j, i: (i, 0)),
                     pl.BlockSpec((tm, LANES), lambda j, i: (i, 0)),
                     pl.BlockSpec((2, HEAD_DIM), lambda j, i: (0, 0))]
        args += [cos, sin_signed, gains]
    return pl.pallas_call(
        functools.partial(_proj_t_kernel, shift=shift, rope=rope is not None, q_tiles=q_tiles,
                          scales=tuple(scales)),
        grid=(nrows // tn, m // tm),
        in_specs=in_specs,
        out_specs=pl.BlockSpec((tm, tn), lambda j, i: (i, j)),
        out_shape=jax.ShapeDtypeStruct((m, nrows), out_dtype),
        scratch_shapes=scratch,
        compiler_params=_params("parallel", "arbitrary"),
        name=name,
    )(*args)


def _diff_attn_kernel(q_ref, k_ref, v_ref, lq1_ref, lk1_ref, lq2_ref, lk2_ref, sg_ref, o_ref,
                      acc_ref, m_ref, l_ref, *, tq, lam_init):
    D = HEAD_DIM
    qi = pl.program_id(2)
    q1 = q_ref[0, :, :D]
    q2 = q_ref[0, :, D:]
    acc_ref[...] = jnp.zeros_like(acc_ref)
    m_ref[...] = jnp.full_like(m_ref, NEG_INF)
    l_ref[...] = jnp.zeros_like(l_ref)

    def lanes(x, width):
        return jnp.concatenate([x] * (width // LANES), axis=1)

    def step(start, mask):
        kblk = k_ref[0, pl.ds(start, tq), :]
        vblk = v_ref[0, pl.ds(start, tq), :].astype(BF16)
        s = jnp.concatenate([_dot_nt(q1, kblk[:, :D]), _dot_nt(q2, kblk[:, D:])], axis=0)
        if mask is not None:
            s = jnp.where(mask, s, NEG_INF)
        m_prev = m_ref[...]
        m_next = jnp.maximum(m_prev, jnp.max(s, axis=-1, keepdims=True))
        alpha = jnp.exp2(m_prev - m_next)
        p = jnp.exp2(s - lanes(m_next, tq))
        l_ref[...] = alpha * l_ref[...] + jnp.sum(p, axis=-1, keepdims=True)
        acc_ref[...] = lanes(alpha, 2 * D) * acc_ref[...] + jnp.dot(p.astype(BF16), vblk,
                                                                    preferred_element_type=F32)
        m_ref[...] = m_next

    def body(j, carry):
        step(pl.multiple_of(j * tq, tq), None)
        return carry

    lax.fori_loop(0, qi, body, 0)
    shift_c = CHUNK.bit_length() - 1
    row = lax.broadcasted_iota(jnp.int32, (2 * tq, tq), 0)
    row = jnp.right_shift(jnp.where(row >= tq, row - tq, row), shift_c)
    col = jnp.right_shift(lax.broadcasted_iota(jnp.int32, (2 * tq, tq), 1), shift_c)
    step(pl.multiple_of(qi * tq, tq), row >= col)

    lam = (jnp.exp(jnp.sum(lq1_ref[...] * lk1_ref[...], axis=-1, keepdims=True))
           - jnp.exp(jnp.sum(lq2_ref[...] * lk2_ref[...], axis=-1, keepdims=True)) + lam_init)
    a = acc_ref[...] / lanes(l_ref[...], 2 * D)
    o = a[:tq] - lam * a[tq:]
    on = o * lax.rsqrt(jnp.mean(o * o, axis=-1, keepdims=True) + NORM_EPS)
    o_ref[0] = (on * sg_ref[...] * (1.0 - lam_init)).astype(o_ref.dtype)


def _diff_attn(qk, v_src, v_col_block, lq1, lk1, lq2, lk2, subln_g, heads, lam_init):
    b, s, _ = qk.shape
    hw = 2 * HEAD_DIM
    tq = _pick_tile(s, ATTN_BLOCK, CHUNK)
    vec = pl.BlockSpec((1, HEAD_DIM), lambda bi, hi, qi: (0, 0))
    return pl.pallas_call(
        functools.partial(_diff_attn_kernel, tq=tq, lam_init=lam_init),
        grid=(b, heads, s // tq),
        in_specs=[pl.BlockSpec((1, tq, hw), lambda bi, hi, qi: (bi, qi, hi)),
                  pl.BlockSpec((1, s, hw), lambda bi, hi, qi: (bi, 0, heads + hi)),
                  pl.BlockSpec((1, s, hw), lambda bi, hi, qi: (bi, 0, v_col_block + hi)),
                  vec, vec, vec, vec,
                  pl.BlockSpec((1, hw), lambda bi, hi, qi: (0, 0))],
        out_specs=pl.BlockSpec((1, tq, hw), lambda bi, hi, qi: (bi, qi, hi)),
        out_shape=jax.ShapeDtypeStruct((b, s, heads * hw), BF16),
        scratch_shapes=[pltpu.VMEM((2 * tq, hw), F32)] + [pltpu.VMEM((2 * tq, LANES), F32)] * 2,
        compiler_params=_params("parallel", "parallel", "arbitrary"),
        name="diff_attention",
    )(qk, qk, v_src, lq1.reshape(1, -1), lk1.reshape(1, -1), lq2.reshape(1, -1), lk2.reshape(1, -1),
      subln_g.reshape(1, hw))


def _merge_kernel(oa_ref, ob_ref, wa_ref, wb_ref, ga_ref, gb_ref, ba_ref, bb_ref, o_ref, wa_s, wb_s):
    @pl.when(pl.program_id(1) == 0)
    def _():
        wa_s[...] = wa_ref[...].astype(BF16)
        wb_s[...] = wb_ref[...].astype(BF16)

    ya = jnp.dot(oa_ref[...], wa_s[...], preferred_element_type=F32)
    yb = jnp.dot(ob_ref[...], wb_s[...], preferred_element_type=F32)
    merged = _sigmoid(ga_ref[...] + ba_ref[...]) * ya + _sigmoid(gb_ref[...] + bb_ref[...]) * yb
    o_ref[...] = merged.astype(o_ref.dtype)


def _merge(oa, ob, w_a, w_b, proj_b, gate_col0, b_gate, d):
    m, ka = oa.shape
    kb = ob.shape[1]
    tn = _pick_tile(d, 512, LANES)
    tm = _pick_tile(m, MATMUL_ROWS, 8)
    assert gate_col0 % tn == 0
    ga = gate_col0 // tn
    gb = (gate_col0 + d) // tn
    nb = d // tn
    bg = b_gate.reshape(1, 2 * d)
    return pl.pallas_call(
        _merge_kernel,
        grid=(nb, m // tm),
        in_specs=[pl.BlockSpec((tm, ka), lambda j, i: (i, 0)),
                  pl.BlockSpec((tm, kb), lambda j, i: (i, 0)),
                  pl.BlockSpec((ka, tn), lambda j, i: (0, j)),
                  pl.BlockSpec((kb, tn), lambda j, i: (0, j)),
                  pl.BlockSpec((tm, tn), lambda j, i: (i, ga + j)),
                  pl.BlockSpec((tm, tn), lambda j, i: (i, gb + j)),
                  pl.BlockSpec((1, tn), lambda j, i: (0, j)),
                  pl.BlockSpec((1, tn), lambda j, i: (0, nb + j))],
        out_specs=pl.BlockSpec((tm, tn), lambda j, i: (i, j)),
        out_shape=jax.ShapeDtypeStruct((m, d), BF16),
        scratch_shapes=[pltpu.VMEM((ka, tn), BF16), pltpu.VMEM((kb, tn), BF16)],
        compiler_params=_params("parallel", "arbitrary"),
        name="branch_merge",
    )(oa, ob, w_a, w_b, proj_b, proj_b, bg, bg)


def _router_kernel(x_ref, g_ref, wr_ref, br_ref, h_ref, ti_ref, tw_ref, rk_ref, cnt_ref, carry_ref,
                   *, n_experts):
    @pl.when(pl.program_id(0) == 0)
    def _():
        carry_ref[...] = jnp.zeros_like(carry_ref)

    x = x_ref[...]
    h = x * lax.rsqrt(jnp.mean(x * x, axis=-1, keepdims=True) + NORM_EPS) * g_ref[...]
    h_ref[...] = h
    logits = jnp.dot(h, wr_ref[...], preferred_element_type=F32,
                     precision=lax.Precision.HIGHEST) + br_ref[...]
    tm = x.shape[0]
    lane = lax.broadcasted_iota(jnp.int32, (tm, LANES), 1)
    lane_f = lane.astype(F32)
    cur = jnp.where(lane < n_experts, logits, -jnp.inf)
    vals, hots = [], []
    ti = jnp.zeros((tm, LANES), F32)
    for kk in range(TOP_K):
        mx = jnp.max(cur, axis=-1, keepdims=True)
        idx = jnp.min(jnp.where(cur == mx, lane_f, float(LANES)), axis=-1, keepdims=True)
        hot = lane_f == idx
        vals.append(mx)
        hots.append(hot)
        ti = jnp.where(lane == kk, idx, ti)
        cur = jnp.where(hot, -jnp.inf, cur)
    exps = [jnp.exp(v - vals[0]) for v in vals]
    denom = exps[0] + exps[1] + exps[2] + exps[3]
    tw = jnp.zeros((tm, LANES), F32)
    for kk in range(TOP_K):
        tw = jnp.where(lane == kk, exps[kk] / denom, tw)
    multihot = jnp.zeros((tm, LANES), F32)
    for hot in hots:
        multihot = multihot + hot.astype(F32)
    r = lax.broadcasted_iota(jnp.int32, (tm, tm), 0)
    c = lax.broadcasted_iota(jnp.int32, (tm, tm), 1)
    prefix = jnp.dot((r > c).astype(BF16), multihot.astype(BF16), preferred_element_type=F32)
    prefix = prefix + carry_ref[...]
    rk = jnp.zeros((tm, LANES), F32)
    for kk in range(TOP_K):
        rk_k = jnp.sum(jnp.where(hots[kk], prefix, 0.0), axis=-1, keepdims=True)
        rk = jnp.where(lane == kk, rk_k, rk)
    carry = carry_ref[...] + jnp.sum(multihot, axis=0, keepdims=True)
    carry_ref[...] = carry
    cnt_ref[...] = carry.astype(jnp.int32)
    ti_ref[...] = ti.astype(jnp.int32)
    tw_ref[...] = tw
    rk_ref[...] = rk.astype(jnp.int32)


def _router(x1, g, w_router, b_router):
    n, d = x1.shape
    e = w_router.shape[1]
    tm = _pick_tile(n, 512, 8)
    wr = jnp.zeros((d, LANES), F32).at[:, :e].set(w_router)
    br = jnp.zeros((1, LANES), F32).at[0, :e].set(b_router)
    row = pl.BlockSpec((tm, LANES), lambda i: (i, 0))
    return pl.pallas_call(
        functools.partial(_router_kernel, n_experts=e),
        grid=(n // tm,),
        in_specs=[pl.BlockSpec((tm, d), lambda i: (i, 0)),
                  pl.BlockSpec((1, d), lambda i: (0, 0)),
                  pl.BlockSpec((d, LANES), lambda i: (0, 0)),
                  pl.BlockSpec((1, LANES), lambda i: (0, 0))],
        out_specs=[pl.BlockSpec((tm, d), lambda i: (i, 0)), row, row, row,
                   pl.BlockSpec((1, LANES), lambda i: (0, 0))],
        out_shape=[jax.ShapeDtypeStruct((n, d), F32),
                   jax.ShapeDtypeStruct((n, LANES), jnp.int32),
                   jax.ShapeDtypeStruct((n, LANES), F32),
                   jax.ShapeDtypeStruct((n, LANES), jnp.int32),
                   jax.ShapeDtypeStruct((1, LANES), jnp.int32)],
        scratch_shapes=[pltpu.VMEM((1, LANES), F32)],
        compiler_params=_params("arbitrary"),
        name="router_topk",
    )(x1, g.reshape(1, d), wr, br)


def _row_copy(src_ref, src_row, dst_ref, dst_row, sem):
    return pltpu.make_async_copy(src_ref.at[pl.ds(src_row, 1)], dst_ref.at[pl.ds(dst_row, 1)], sem)


def _expert_kernel(se_ref, sb0_ref, sn_ref, na_ref, *refs, rows, sbk, nf, nsuper, nblk):
    tok_refs = refs[:sbk]
    tokn_refs = refs[sbk:2 * sbk]
    (h_ref, wg_ref, wu_ref, wd_ref, bg_ref, bu_ref, bd_ref, ys_ref,
     xg_ref, xb_ref, acc_ref, gsem, osem) = refs[2 * sbk:]
    s = pl.program_id(0)
    f = pl.program_id(1)
    nsub = sn_ref[s]

    def gather(t_refs, count):
        for j in range(sbk):
            @pl.when(j < count)
            def _(j=j):
                def body(r2, carry):
                    for prio in range(2):
                        r = 2 * r2 + prio
                        _row_copy(h_ref, t_refs[j][0, 0, r], xg_ref, j * rows + r, gsem).start(priority=prio)
                    return carry
                lax.fori_loop(0, rows // 2, body, 0, unroll=4)

    def out_copy(j, blk0):
        return pltpu.make_async_copy(acc_ref.at[pl.ds(j * rows, rows)],
                                     ys_ref.at[pl.ds((blk0 + j) * rows, rows)], osem)

    n_act = na_ref[0]
    blocks_used = na_ref[1]

    def tail_block(j):
        return blocks_used + (s - n_act) * sbk + j

    def zero_copy(j):
        return pltpu.make_async_copy(acc_ref.at[pl.ds(0, rows)],
                                     ys_ref.at[pl.ds(tail_block(j) * rows, rows)], osem)

    @pl.when((f == 0) & (s == 0))
    def _():
        gather(tok_refs, nsub)

    @pl.when(f == 0)
    def _():
        for j in range(sbk):
            @pl.when(j < nsub)
            def _(j=j):
                pltpu.make_async_copy(h_ref.at[pl.ds(0, rows)], xg_ref.at[pl.ds(j * rows, rows)], gsem).wait()
        for j in range(sbk):
            @pl.when(j < nsub)
            def _(j=j):
                xb_ref[pl.ds(j * rows, rows), :] = xg_ref[pl.ds(j * rows, rows), :].astype(BF16)

        @pl.when(s + 1 < nsuper)
        def _():
            gather(tokn_refs, sn_ref[s + 1])

        @pl.when(s > 0)
        def _():
            for j in range(sbk):
                @pl.when(j < sn_ref[s - 1])
                def _(j=j):
                    out_copy(j, sb0_ref[s - 1]).wait()

        for j in range(sbk):
            @pl.when(j < nsub)
            def _(j=j):
                acc_ref[pl.ds(j * rows, rows), :] = jnp.broadcast_to(bd_ref[0], (rows, acc_ref.shape[1]))

        @pl.when(s == n_act)
        def _():
            acc_ref[pl.ds(0, rows), :] = jnp.zeros((rows, acc_ref.shape[1]), F32)

        for j in range(sbk):
            @pl.when((s >= n_act) & (tail_block(j) < nblk))
            def _(j=j):
                zero_copy(j).start()

    for j in range(sbk):
        @pl.when(j < nsub)
        def _(j=j):
            xb = xb_ref[pl.ds(j * rows, rows), :]
            g = jnp.dot(xb, wg_ref[0].astype(BF16), preferred_element_type=F32) + bg_ref[0]
            u = jnp.dot(xb, wu_ref[0].astype(BF16), preferred_element_type=F32) + bu_ref[0]
            gate = jnp.minimum(g, SWIGLU_LIMIT)
            up = jnp.clip(u, -SWIGLU_LIMIT, SWIGLU_LIMIT)
            act = (up + 1.0) * gate * _sigmoid(SWIGLU_ALPHA * gate)
            acc_ref[pl.ds(j * rows, rows), :] += jnp.dot(act.astype(BF16), wd_ref[0].astype(BF16),
                                                         preferred_element_type=F32)

    @pl.when(f == nf - 1)
    def _():
        for j in range(sbk):
            @pl.when((s >= n_act) & (tail_block(j) < nblk))
            def _(j=j):
                zero_copy(j).wait()

        for j in range(sbk):
            @pl.when(j < nsub)
            def _(j=j):
                out_copy(j, sb0_ref[s]).start()

        @pl.when(s == nsuper - 1)
        def _():
            for j in range(sbk):
                @pl.when(j < nsub)
                def _(j=j):
                    out_copy(j, sb0_ref[s]).wait()


def _experts(h, slot_token, sb_expert, sb_blk0, sb_nsub, n_active, w_gate_up, b_gate_up, w_down, b_down,
             rows, sbk):
    n, d = h.shape
    p = slot_token.shape[0]
    e, _, dff2 = w_gate_up.shape
    dff = dff2 // 2
    tf = _pick_tile(dff, EXPERT_FF_TILE, LANES)
    nf = dff // tf
    nblk = p // rows
    nsuper = sb_expert.shape[0]
    tok3 = slot_token.reshape(nblk, 1, rows)

    def frozen_f(s, f, na):
        return jnp.where(s < na[0], f, nf - 1)

    def tok_spec(j, ahead):
        def index(s, f, se, sb0, sn, na):
            si = jnp.minimum(s + ahead, nsuper - 1)
            return (jnp.minimum(sb0[si] + j, nblk - 1), 0, 0)
        return pl.BlockSpec((1, 1, rows), index, memory_space=pltpu.SMEM)

    grid_spec = pltpu.PrefetchScalarGridSpec(
        num_scalar_prefetch=4,
        grid=(nsuper, nf),
        in_specs=[tok_spec(j, 0) for j in range(sbk)] + [tok_spec(j, 1) for j in range(sbk)] + [
            pl.BlockSpec(memory_space=pl.ANY),
            pl.BlockSpec((1, d, tf), lambda s, f, se, sb0, sn, na: (se[s], 0, frozen_f(s, f, na))),
            pl.BlockSpec((1, d, tf), lambda s, f, se, sb0, sn, na: (se[s], 0, nf + frozen_f(s, f, na))),
            pl.BlockSpec((1, tf, d), lambda s, f, se, sb0, sn, na: (se[s], frozen_f(s, f, na), 0)),
            pl.BlockSpec((1, 1, tf), lambda s, f, se, sb0, sn, na: (se[s], 0, frozen_f(s, f, na))),
            pl.BlockSpec((1, 1, tf), lambda s, f, se, sb0, sn, na: (se[s], 0, nf + frozen_f(s, f, na))),
            pl.BlockSpec((1, 1, d), lambda s, f, se, sb0, sn, na: (se[s], 0, 0))],
        out_specs=pl.BlockSpec(memory_space=pl.ANY),
        scratch_shapes=[pltpu.VMEM((sbk * rows, d), F32), pltpu.VMEM((sbk * rows, d), BF16),
                        pltpu.VMEM((sbk * rows, d), F32),
                        pltpu.SemaphoreType.DMA(()), pltpu.SemaphoreType.DMA(())],
    )
    return pl.pallas_call(
        functools.partial(_expert_kernel, rows=rows, sbk=sbk, nf=nf, nsuper=nsuper, nblk=nblk),
        grid_spec=grid_spec,
        out_shape=jax.ShapeDtypeStruct((p, d), F32),
        compiler_params=_params("arbitrary", "arbitrary"),
        name="expert_mlp",
    )(sb_expert, sb_blk0, sb_nsub, n_active, *([tok3] * (2 * sbk)), h, w_gate_up, w_gate_up, w_down,
      b_gate_up.reshape(e, 1, dff2), b_gate_up.reshape(e, 1, dff2), b_down.reshape(e, 1, d))


def _combine_kernel(dest_ref, destn_ref, x_ref, tw_ref, ys_ref, o_ref, buf_ref, sem, *, tm, nt):
    i = pl.program_id(0)
    slot = i % 2

    def gather(d_ref, s):
        def body(t, carry):
            for kk in range(TOP_K):
                _row_copy(ys_ref, d_ref[0, 0, t * TOP_K + kk], buf_ref.at[s, kk], t, sem.at[s]).start(priority=kk % 2)
            return carry
        lax.fori_loop(0, tm, body, 0, unroll=2)

    @pl.when(i == 0)
    def _():
        gather(dest_ref, 0)

    @pl.when(i + 1 < nt)
    def _():
        gather(destn_ref, 1 - slot)

    for kk in range(TOP_K):
        pltpu.make_async_copy(ys_ref.at[pl.ds(0, tm)], buf_ref.at[slot, kk], sem.at[slot]).wait()
    tw = tw_ref[...]
    y = buf_ref[slot, 0] * tw[:, 0:1]
    for kk in range(1, TOP_K):
        y = y + buf_ref[slot, kk] * tw[:, kk:kk + 1]
    o_ref[...] = x_ref[...] + y


def _combine(x1, topw, dest, ys):
    n, d = x1.shape
    tm = _pick_tile(n, 128, 8)
    nt = n // tm
    dest3 = dest.reshape(nt, 1, tm * TOP_K)
    return pl.pallas_call(
        functools.partial(_combine_kernel, tm=tm, nt=nt),
        grid=(nt,),
        in_specs=[pl.BlockSpec((1, 1, tm * TOP_K), lambda i: (i, 0, 0), memory_space=pltpu.SMEM),
                  pl.BlockSpec((1, 1, tm * TOP_K), lambda i: (jnp.minimum(i + 1, nt - 1), 0, 0),
                               memory_space=pltpu.SMEM),
                  pl.BlockSpec((tm, d), lambda i: (i, 0)),
                  pl.BlockSpec((tm, LANES), lambda i: (i, 0)),
                  pl.BlockSpec(memory_space=pl.ANY)],
        out_specs=pl.BlockSpec((tm, d), lambda i: (i, 0)),
        out_shape=jax.ShapeDtypeStruct((n, d), F32),
        scratch_shapes=[pltpu.VMEM((2, TOP_K, tm, d), F32), pltpu.SemaphoreType.DMA((2,))],
        compiler_params=_params("arbitrary"),
        name="moe_combine",
    )(dest3, dest3, x1, topw, ys)


MATMUL_ROWS = 1024
ATTN_BLOCK = 1024
GDN_HEADS_PER_STEP = 16
GDN_GROUP = 4
EXPERT_ROWS = 512
EXPERT_SUBBLOCKS = 2
EXPERT_FF_TILE = 512


def _layer(x, positions, layer_idx, norm1_g, w_in, b_gate, conv_w, a_log, dt_bias, gdn_norm_g,
           q_norm_g, k_norm_g, lambda_q1, lambda_k1, lambda_q2, lambda_k2, subln_g,
           w_branch_a, w_branch_b, w_out, norm2_g, w_router, b_router,
           w_gate_up, b_gate_up, w_down, b_down):
    b, s, d = x.shape
    n = b * s
    gdn_heads = a_log.shape[0]
    gdn_dim = gdn_heads * HEAD_DIM
    diff_v = w_branch_b.shape[0]
    diff_heads = diff_v // (2 * HEAD_DIM)
    diff_qk = 2 * diff_heads * HEAD_DIM
    off_beta = 4 * gdn_dim
    off_alpha = off_beta + gdn_heads
    off_qb = off_alpha + gdn_heads
    off_gate = off_qb + 2 * diff_qk + diff_v
    assert w_in.shape[1] == off_gate + 2 * d
    n_experts = w_router.shape[1]

    x2 = x.reshape(n, d)
    h = _rmsnorm(x2, norm1_g, BF16)

    wt = jnp.swapaxes(w_in, 0, 1)
    proj_a = _proj_t(h, wt, 0, off_beta, F32, name="proj_gdn")
    hg = min(GDN_HEADS_PER_STEP, gdn_heads)
    ngroups = gdn_heads // hg
    row_pad = jnp.zeros((LANES - 2 * hg, d), F32)
    w_ba = jnp.concatenate(
        [part for gi in range(ngroups) for part in (
            wt[off_beta + gi * hg:off_beta + (gi + 1) * hg],
            wt[off_alpha + gi * hg:off_alpha + (gi + 1) * hg], row_pad)], axis=0)
    ba = _proj_t(h, w_ba, 0, ngroups * LANES, F32, name="proj_beta_alpha")
    cos_t, sin_t = _rope_table(positions.astype(F32).reshape(n, 1))
    qk = _proj_t(h, wt, off_qb, 2 * diff_qk, BF16, name="proj_qk_rope",
                 rope=(cos_t, sin_t, jnp.stack([q_norm_g, k_norm_g]),
                       (HEAD_DIM ** -0.5 * math.log2(math.e), 1.0), diff_qk))
    vg = _proj_t(h, wt, off_qb + 2 * diff_qk, diff_v + 2 * d, F32, name="proj_v_gate")

    alog_p = jnp.zeros((ngroups, 1, LANES), F32).at[:, 0, hg:2 * hg].set(a_log.reshape(ngroups, hg))
    dtb_p = jnp.zeros((ngroups, 1, LANES), F32).at[:, 0, hg:2 * hg].set(dt_bias.reshape(ngroups, hg))
    oa = _gdn(proj_a.reshape(b, s, off_beta), ba.reshape(b, s, ngroups * LANES), conv_w,
              alog_p, dtb_p, gdn_norm_g, gdn_heads, hg)

    lam_init = 0.8 - 0.6 * math.exp(-0.3 * layer_idx)
    ob = _diff_attn(qk.reshape(b, s, 2 * diff_qk), vg.reshape(b, s, diff_v + 2 * d), 0,
                    lambda_q1, lambda_k1, lambda_q2, lambda_k2, subln_g, diff_heads, lam_init)

    merged = _merge(oa.reshape(n, gdn_dim), ob.reshape(n, diff_v), w_branch_a, w_branch_b,
                    vg, diff_v, b_gate, d)
    x1 = _matmul(merged, w_out, 0, d, F32, res=x2, name="out_proj")

    h2, topi, topw, rank, counts = _router(x1, norm2_g, w_router, b_router)
    rows, sbk = EXPERT_ROWS, EXPERT_SUBBLOCKS
    top_e = topi[:, :TOP_K]
    cnt = counts[0, :n_experts]
    nb = (cnt + rows - 1) // rows
    blk_end = jnp.cumsum(nb)
    blk_start = blk_end - nb
    dest = (blk_start[top_e] * rows + rank[:, :TOP_K]).reshape(n * TOP_K)
    p_rows = n * TOP_K + n_experts * rows
    nblk = p_rows // rows
    slot_token = jnp.zeros((p_rows,), jnp.int32).at[dest].set(
        jnp.repeat(jnp.arange(n, dtype=jnp.int32), TOP_K), unique_indices=True, mode="promise_in_bounds")
    ns = (nb + sbk - 1) // sbk
    ns_end = jnp.cumsum(ns)
    ns_start = ns_end - ns
    n_active = ns_end[-1]
    nsuper = nblk // sbk + n_experts
    s_idx = jnp.arange(nsuper, dtype=jnp.int32)
    active = s_idx < n_active
    s_src = jnp.where(active, s_idx, jnp.maximum(n_active - 1, 0))
    sb_expert = jnp.minimum(jnp.sum(s_src[:, None] >= ns_end[None, :], axis=1), n_experts - 1)
    sb_local = s_src - ns_start[sb_expert]
    sb_blk0 = blk_start[sb_expert] + sb_local * sbk
    sb_nsub = jnp.where(active, jnp.clip(nb[sb_expert] - sb_local * sbk, 0, sbk), 0)
    ys = _experts(h2, slot_token, sb_expert.astype(jnp.int32), sb_blk0.astype(jnp.int32),
                  sb_nsub.astype(jnp.int32), jnp.stack([n_active, blk_end[-1]]).astype(jnp.int32),
                  w_gate_up, b_gate_up, w_down, b_down, rows, sbk)
    out = _combine(x1, topw, dest, ys)
    return out.reshape(b, s, d)


def kernel(x, positions, norm1_g, w_in, b_gate, conv_w, a_log, dt_bias, gdn_norm_g, q_norm_g, k_norm_g, lambda_q1, lambda_k1, lambda_q2, lambda_k2, subln_g, w_branch_a, w_branch_b, w_out, norm2_g, w_router, b_router, w_gate_up, b_gate_up, w_down, b_down):
    depth = norm1_g.shape[0]
    for l in range(depth):
        x = _layer(x, positions, l, norm1_g[l], w_in[l], b_gate[l], conv_w[l], a_log[l], dt_bias[l],
                   gdn_norm_g[l], q_norm_g[l], k_norm_g[l], lambda_q1[l], lambda_k1[l], lambda_q2[l],
                   lambda_k2[l], subln_g[l], w_branch_a[l], w_branch_b[l], w_out[l], norm2_g[l],
                   w_router[l], b_router[l], w_gate_up[l], b_gate_up[l], w_down[l], b_down[l])
    return x
```
